```python
import jax, jax.numpy as jnp
from jax import lax
import numpy as np

D_MODEL = 1024
BATCH = 8
SEQ = 4096
DEPTH = 1

CHUNK = 64
Q_BLOCK = 128
NORM_EPS = 1e-6
FOX_HEADS = 8
FOX_HEAD_DIM = 64
FOX_WIDTH = FOX_HEADS * FOX_HEAD_DIM
MLA_HEADS = 8
MLA_NOPE_DIM = 64
MLA_ROPE_DIM = 32
MLA_V_DIM = 64
MLA_Q_RANK = 256
MLA_KV_RANK = 128
MLA_QK_DIM = MLA_NOPE_DIM + MLA_ROPE_DIM
MLA_WIDTH = MLA_HEADS * MLA_V_DIM
ROPE_THETA = 10000.0
N_BRANCHES = 2
N_EXPERTS = 32
TOP_K = 4
EXPERT_FF = 1024
SWIGLU_LIMIT = 7.0
SWIGLU_ALPHA = 1.702
EXPERT_BLOCK = 256

IN_SIZES = (FOX_WIDTH, FOX_WIDTH, FOX_WIDTH, FOX_HEADS, MLA_Q_RANK, MLA_KV_RANK + MLA_ROPE_DIM, N_BRANCHES * D_MODEL)
IN_COLS = FOX_WIDTH * 3 + FOX_HEADS + MLA_Q_RANK + MLA_KV_RANK + MLA_ROPE_DIM + N_BRANCHES * D_MODEL

kernel_name = 'hybrid_fox_mla_gated_moe_block'


def rms_norm(x, g):
    xf = x.astype(jnp.float32)
    y = xf * lax.rsqrt(jnp.mean(xf * xf, axis=-1, keepdims=True) + NORM_EPS)
    return (y * g.astype(jnp.float32)).astype(x.dtype)


def apply_rope(x, cos, sin):
    x1, x2 = jnp.split(x, 2, axis=-1)
    return jnp.concatenate([x1 * cos - x2 * sin, x2 * cos + x1 * sin], axis=-1).astype(x.dtype)


def blocked_attention(q, k, v, scale, log_f_cum=None):
    S = q.shape[1]
    neg = jnp.finfo(jnp.float32).min
    outs = []
    for i in range(S // Q_BLOCK):
        q0, q1 = i * Q_BLOCK, (i + 1) * Q_BLOCK
        s = jnp.einsum('bqhd,bkhd->bhqk', q[:, q0:q1], k[:, :q1]).astype(jnp.float32) * scale
        tq = jnp.arange(q0, q1)[:, None]
        tk = jnp.arange(q1)[None, :]
        if log_f_cum is None:
            allowed = (tk // CHUNK) <= (tq // CHUNK)
        else:
            allowed = tk <= tq
            s = s + (log_f_cum[:, :, q0:q1, None] - log_f_cum[:, :, None, :q1])
        p = jax.nn.softmax(jnp.where(allowed, s, neg), axis=-1)
        outs.append(jnp.einsum('bhqk,bkhd->bqhd', p.astype(v.dtype), v[:, :q1]))
    return jnp.concatenate(outs, axis=1)


def token_mixer(h, positions, w_in, b_fgate, g_q_a, w_q_b, g_kv_a, w_kv_b, w_fox_out, w_mla_out, b_merge, w_o):
    B, S, _ = h.shape
    proj = h @ w_in
    q_f, k_f, v_f, f_logit, q_lat, kv_lat, gate_logit = jnp.split(proj, list(np.cumsum(IN_SIZES)[:-1]), axis=-1)

    qf = q_f.reshape(B, S, FOX_HEADS, FOX_HEAD_DIM)
    kf = k_f.reshape(B, S, FOX_HEADS, FOX_HEAD_DIM)
    vf = v_f.reshape(B, S, FOX_HEADS, FOX_HEAD_DIM)
    log_f = jax.nn.log_sigmoid(f_logit.astype(jnp.float32) + b_fgate.astype(jnp.float32))
    log_f_cum = jnp.cumsum(log_f, axis=1).transpose(0, 2, 1)
    y_fox = blocked_attention(qf, kf, vf, FOX_HEAD_DIM ** -0.5, log_f_cum).reshape(B, S, FOX_WIDTH)

    q = (rms_norm(q_lat, g_q_a) @ w_q_b).reshape(B, S, MLA_HEADS, MLA_QK_DIM)
    q_nope, q_pe = q[..., :MLA_NOPE_DIM], q[..., MLA_NOPE_DIM:]
    c_kv, k_pe = kv_lat[..., :MLA_KV_RANK], kv_lat[..., MLA_KV_RANK:]
    kv = (rms_norm(c_kv, g_kv_a) @ w_kv_b).reshape(B, S, MLA_HEADS, MLA_NOPE_DIM + MLA_V_DIM)
    k_nope, v_m = kv[..., :MLA_NOPE_DIM], kv[..., MLA_NOPE_DIM:]
    half = MLA_ROPE_DIM // 2
    inv_freq = ROPE_THETA ** (-jnp.arange(half, dtype=jnp.float32) / half)
    ang = positions.astype(jnp.float32)[..., None] * inv_freq
    cos, sin = jnp.cos(ang), jnp.sin(ang)
    q_pe = apply_rope(q_pe, cos[:, :, None, :], sin[:, :, None, :])
    k_pe = apply_rope(k_pe, cos, sin)
    q_m = jnp.concatenate([q_nope, q_pe], axis=-1)
    k_m = jnp.concatenate([k_nope, jnp.broadcast_to(k_pe[:, :, None, :], (B, S, MLA_HEADS, MLA_ROPE_DIM))], axis=-1)
    y_mla = blocked_attention(q_m, k_m, v_m, MLA_QK_DIM ** -0.5, None).reshape(B, S, MLA_WIDTH)

    gates = jax.nn.sigmoid(gate_logit.reshape(B, S, N_BRANCHES, D_MODEL) + b_merge)
    merged = gates[:, :, 0] * (y_fox @ w_fox_out) + gates[:, :, 1] * (y_mla @ w_mla_out)
    return merged @ w_o


def expert_ffn(xb, w_gu, b_gu, w_d, b_d):
    gu = xb @ w_gu + b_gu
    gate = jnp.minimum(gu[..., :EXPERT_FF], SWIGLU_LIMIT)
    up = jnp.clip(gu[..., EXPERT_FF:], -SWIGLU_LIMIT, SWIGLU_LIMIT)
    glu = gate * jax.nn.sigmoid(SWIGLU_ALPHA * gate)
    return ((up + 1.0) * glu) @ w_d + b_d


def moe_ffn(h, w_router, b_router, w_gu, b_gu, w_d, b_d):
    B, S, D = h.shape
    N = B * S
    A = N * TOP_K
    t = h.reshape(N, D)
    logits = (t @ w_router).astype(jnp.float32) + b_router.astype(jnp.float32)
    top_val, top_idx = lax.top_k(logits, TOP_K)
    top_w = jax.nn.softmax(top_val, axis=-1)
    flat_e = top_idx.reshape(A)
    flat_w = top_w.reshape(A)
    order = jnp.argsort(flat_e)
    sorted_e = flat_e[order]
    sorted_tok = (order // TOP_K).astype(jnp.int32)
    counts = jnp.zeros((N_EXPERTS,), jnp.int32).at[flat_e].add(1)
    padded = (counts + EXPERT_BLOCK - 1) // EXPERT_BLOCK * EXPERT_BLOCK
    pad_end = jnp.cumsum(padded)
    pad_start = pad_end - padded
    start = jnp.cumsum(counts) - counts
    row = pad_start[sorted_e] + (jnp.arange(A, dtype=jnp.int32) - start[sorted_e])
    R = A + N_EXPERTS * EXPERT_BLOCK
    n_blocks = R // EXPERT_BLOCK
    row_tok = jnp.zeros((R,), jnp.int32).at[row].set(sorted_tok)
    row_w = jnp.zeros((R,), jnp.float32).at[row].set(flat_w[order])
    block_e = jnp.minimum(jnp.searchsorted(pad_end, jnp.arange(n_blocks, dtype=jnp.int32) * EXPERT_BLOCK, side='right'), N_EXPERTS - 1)
    xs = t[row_tok].reshape(n_blocks, EXPERT_BLOCK, D)
    ys = lax.map(lambda a: expert_ffn(a[0], w_gu[a[1]], b_gu[a[1]], w_d[a[1]], b_d[a[1]]), (xs, block_e))
    ys = ys.reshape(R, D) * row_w[:, None].astype(ys.dtype)
    return jax.ops.segment_sum(ys, row_tok, num_segments=N).reshape(B, S, D)


def setup_inputs(seed: int = 0) -> dict:
    key = jax.random.key(seed)
    ks = jax.random.split(key, 24)
    f32 = jnp.float32
    L = DEPTH

    def nrm(k, shape, scale):
        return jax.random.normal(k, shape, f32) * scale

    def gain(k, shape):
        return 1.0 + 0.05 * jax.random.normal(k, shape, f32)

    x = jax.random.normal(ks[0], (BATCH, SEQ, D_MODEL), f32)
    offsets = jax.random.randint(ks[1], (BATCH, 1), 0, 8192, dtype=jnp.int32)
    positions = offsets + jnp.arange(SEQ, dtype=jnp.int32)[None, :]
    return {
        'x': x,
        'positions': positions,
        'g_attn_norm': gain(ks[2], (L, D_MODEL)),
        'w_in': nrm(ks[3], (L, D_MODEL, IN_COLS), D_MODEL ** -0.5),
        'b_fgate': 3.0 + 0.5 * jax.random.normal(ks[4], (L, FOX_HEADS), f32),
        'g_q_a': gain(ks[5], (L, MLA_Q_RANK)),
        'w_q_b': nrm(ks[6], (L, MLA_Q_RANK, MLA_HEADS * MLA_QK_DIM), MLA_Q_RANK ** -0.5),
        'g_kv_a': gain(ks[7], (L, MLA_KV_RANK)),
        'w_kv_b': nrm(ks[8], (L, MLA_KV_RANK, MLA_HEADS * (MLA_NOPE_DIM + MLA_V_DIM)), MLA_KV_RANK ** -0.5),
        'w_fox_out': nrm(ks[9], (L, FOX_WIDTH, D_MODEL), FOX_WIDTH ** -0.5),
        'w_mla_out': nrm(ks[10], (L, MLA_WIDTH, D_MODEL), MLA_WIDTH ** -0.5),
        'b_merge': nrm(ks[11], (L, N_BRANCHES, D_MODEL), 0.1),
        'w_o': nrm(ks[12], (L, D_MODEL, D_MODEL), D_MODEL ** -0.5),
        'g_ffn_norm': gain(ks[13], (L, D_MODEL)),
        'w_router': nrm(ks[14], (L, D_MODEL, N_EXPERTS), D_MODEL ** -0.5),
        'b_router': nrm(ks[15], (L, N_EXPERTS), 0.01),
        'w_gu': nrm(ks[16], (L, N_EXPERTS, D_MODEL, 2 * EXPERT_FF), D_MODEL ** -0.5),
        'b_gu': nrm(ks[17], (L, N_EXPERTS, 2 * EXPERT_FF), 0.01),
        'w_down': nrm(ks[18], (L, N_EXPERTS, EXPERT_FF, D_MODEL), EXPERT_FF ** -0.5),
        'b_down': nrm(ks[19], (L, N_EXPERTS, D_MODEL), 0.01),
        'g_final': gain(ks[20], (D_MODEL,)),
    }


def reference(x, positions, g_attn_norm, w_in, b_fgate, g_q_a, w_q_b, g_kv_a, w_kv_b, w_fox_out, w_mla_out, b_merge, w_o, g_ffn_norm, w_router, b_router, w_gu, b_gu, w_down, b_down, g_final):
    for l in range(DEPTH):
        h = rms_norm(x, g_attn_norm[l])
        x = x + token_mixer(h, positions, w_in[l], b_fgate[l], g_q_a[l], w_q_b[l], g_kv_a[l], w_kv_b[l], w_fox_out[l], w_mla_out[l], b_merge[l], w_o[l])
        h = rms_norm(x, g_ffn_norm[l])
        x = x + moe_ffn(h, w_router[l], b_router[l], w_gu[l], b_gu[l], w_down[l], b_down[l])
    return rms_norm(x, g_final)
```

```python
import functools

import jax
import jax.numpy as jnp
import numpy as np
from jax import lax
from jax.experimental import pallas as pl
from jax.experimental.pallas import tpu as pltpu

F32 = jnp.float32
BF16 = jnp.bfloat16

LANES = 128
SUBLANES = 8
VMEM_LIMIT = 56 * 1024 * 1024

NORM_EPS = 1e-6
HEADS = 8
HEAD_DIM = 64
ROPE_DIM = 32
Q_RANK = 256
KV_RANK = 128
N_EXPERTS = 32
TOP_K = 4
EXPERT_FF = 1024
SWIGLU_LIMIT = 7.0
SWIGLU_ALPHA = 1.702
ROPE_THETA = 10000.0
CHUNK = 64

NEG = -1e30

ROW_TILE = 256
ATT_TILE = 512
EXPERT_ROWS = 256

PE_LO, PE_MID, PE_HI = 64, 80, 96
FG_LO = 96
AUG_LO = 64


def _dot(a, b):
    return jnp.dot(a, b, preferred_element_type=F32)


def _dot_nt(a, b):
    return lax.dot_general(a, b, (((1,), (1,)), ((), ())), preferred_element_type=F32)


def _split3(a):
    hi = a.astype(BF16)
    r1 = a - hi.astype(F32)
    mid = r1.astype(BF16)
    lo = (r1 - mid.astype(F32)).astype(BF16)
    return hi, mid, lo


def _rms(x, g):
    return x * lax.rsqrt(jnp.mean(x * x, axis=-1, keepdims=True) + NORM_EPS) * g


def _inproj_kernel(x_ref, pos_ref, g_ref, wq_ref, wk_ref, wv_ref, wlat_ref, wg_ref, bf_ref,
                   gq_ref, wqb_ref, gkv_ref, wkbk_ref, wkbv_ref, ltri_ref, place_ref, ones_ref,
                   freq_ref,
                   qf_ref, kf_ref, vf_ref, qm_ref, km_ref, vm_ref, gate_ref,
                   carry_ref, *, tiles_per_seq, mla_scale):
    i = pl.program_id(0)

    @pl.when(i % tiles_per_seq == 0)
    def _():
        carry_ref[...] = jnp.zeros_like(carry_ref)

    h = _rms(x_ref[...], g_ref[...]).astype(BF16)
    tm = h.shape[0]
    lane = lax.broadcasted_iota(jnp.int32, (tm, LANES), 1)

    gate_ref[...] = _dot(h, wg_ref[...]).astype(BF16)
    vf_ref[...] = _dot(h, wv_ref[...]).astype(BF16)

    lat = _dot(h, wlat_ref[...])
    q_lat = lat[:, :Q_RANK]
    c_kv = lat[:, Q_RANK:Q_RANK + KV_RANK]
    misc = lat[:, Q_RANK + KV_RANK:]

    z = misc + bf_ref[...]
    logf = jnp.minimum(z, 0.0) - jnp.log1p(jnp.exp(-jnp.abs(z)))
    fmask = (lane >= FG_LO) & (lane < FG_LO + HEADS)
    logf = jnp.where(fmask, logf, 0.0)
    l_hi, l_mid, l_lo = _split3(logf)
    ltri = ltri_ref[...]
    c = _dot(ltri, l_hi) + _dot(ltri, l_mid) + _dot(ltri, l_lo) + carry_ref[...]
    carry_ref[...] = c[tm - 1:tm, :]
    c_hi, c_mid, c_lo = _split3(c)
    c3 = (c_hi.astype(F32) + pltpu.roll(c_mid.astype(F32), HEADS, 1)
          + pltpu.roll(c_lo.astype(F32), 2 * HEADS, 1)).astype(BF16)
    aug = _dot(c3, place_ref[...]) + ones_ref[...]
    hw = HEADS * LANES
    qf_ref[...] = (_dot(h, wq_ref[...]) + aug[:, :hw]).astype(BF16)
    kf_ref[...] = (_dot(h, wk_ref[...]) + aug[:, hw:]).astype(BF16)

    qn = _rms(q_lat, gq_ref[...]).astype(BF16)
    kvn = _rms(c_kv, gkv_ref[...]).astype(BF16)
    qfull = _dot(qn, wqb_ref[...])
    knope = _dot(kvn, wkbk_ref[...])
    vm_ref[...] = _dot(kvn, wkbv_ref[...]).astype(BF16)

    ang = pos_ref[...].astype(F32) * freq_ref[...]
    cosv = jnp.cos(ang)
    sinv = jnp.sin(ang)
    s1 = jnp.where((lane >= PE_LO) & (lane < PE_MID), -sinv, 0.0)
    s2 = jnp.where((lane >= PE_MID) & (lane < PE_HI), sinv, 0.0)

    def rope(v):
        return v * cosv + pltpu.roll(v, LANES - 16, 1) * s1 + pltpu.roll(v, 16, 1) * s2

    kpe = jnp.where((lane >= PE_LO) & (lane < PE_HI), rope(misc), 0.0)
    for hd in range(HEADS):
        sl = slice(hd * LANES, (hd + 1) * LANES)
        qm_ref[:, sl] = (rope(qfull[:, sl]) * mla_scale).astype(BF16)
        km_ref[:, sl] = (knope[:, sl] + kpe).astype(BF16)


def _inproj(x2, pos2, g_attn, wq, wk, wv, wlat, wg, bf128, gq, wqb, gkv, wkbk, wkbv, ltri,
            place, ones, freq, *, seq_len):
    n, d = x2.shape
    tm = ROW_TILE
    hw = HEADS * LANES
    vw = HEADS * HEAD_DIM
    const = lambda i: (0, 0)
    row = lambda i: (i, 0)
    full = lambda a: pl.BlockSpec(a.shape, const)
    outs = [jax.ShapeDtypeStruct((n, hw), BF16), jax.ShapeDtypeStruct((n, hw), BF16),
            jax.ShapeDtypeStruct((n, vw), BF16), jax.ShapeDtypeStruct((n, hw), BF16),
            jax.ShapeDtypeStruct((n, hw), BF16), jax.ShapeDtypeStruct((n, vw), BF16),
            jax.ShapeDtypeStruct((n, wg.shape[1]), BF16)]
    consts = (g_attn, wq, wk, wv, wlat, wg, bf128, gq, wqb, gkv, wkbk, wkbv, ltri, place, ones, freq)
    return pl.pallas_call(
        functools.partial(_inproj_kernel, tiles_per_seq=seq_len // tm,
                          mla_scale=float((HEAD_DIM + ROPE_DIM) ** -0.5)),
        out_shape=outs,
        grid=(n // tm,),
        in_specs=[pl.BlockSpec((tm, d), row), pl.BlockSpec((tm, 1), row)] + [full(a) for a in consts],
        out_specs=[pl.BlockSpec((tm, o.shape[1]), row) for o in outs],
        scratch_shapes=[pltpu.VMEM((1, LANES), F32)],
        compiler_params=pltpu.CompilerParams(dimension_semantics=("arbitrary",),
                                             vmem_limit_bytes=VMEM_LIMIT),
        name="inproj",
    )(x2, pos2, *consts)


def _attn_kernel(q_ref, k_ref, v_ref, o_ref, *, chunk_mask):
    i = pl.program_id(2)
    t = q_ref.shape[0]
    row = lax.broadcasted_iota(jnp.int32, (t, t), 0)
    col = lax.broadcasted_iota(jnp.int32, (t, t), 1)
    if chunk_mask:
        allowed = (col // CHUNK) <= (row // CHUNK)
    else:
        allowed = col <= row
    lane = lax.broadcasted_iota(jnp.int32, (t, LANES), 1)

    outs = []
    for hd in range(2):
        sl = slice(hd * LANES, (hd + 1) * LANES)
        q = q_ref[:, sl]

        def step(j, carry, masked):
            m, l, acc = carry
            start = pl.multiple_of(j * t, t)
            kb = k_ref[pl.ds(start, t), sl]
            vb = v_ref[pl.ds(start, t), :]
            s = _dot_nt(q, kb)
            if masked:
                s = jnp.where(allowed, s, NEG)
            m_new = jnp.maximum(m, jnp.max(s, axis=-1, keepdims=True))
            alpha = jnp.exp(m - m_new)
            p = jnp.exp(s - m_new)
            l = alpha * l + jnp.sum(p, axis=-1, keepdims=True)
            acc = alpha * acc + _dot(p.astype(BF16), vb)
            return m_new, l, acc

        init = (jnp.full((t, 1), NEG, F32), jnp.zeros((t, 1), F32), jnp.zeros((t, LANES), F32))
        carry = lax.fori_loop(0, i, lambda j, c: step(j, c, False), init)
        m, l, acc = step(i, carry, True)
        outs.append(acc / l)
    o_ref[...] = jnp.where(lane < HEAD_DIM, outs[0], outs[1]).astype(o_ref.dtype)


def _attention(q, k, v, *, batch, seq_len, chunk_mask):
    n = q.shape[0]
    t = ATT_TILE
    nq = seq_len // t
    pairs = HEADS // 2
    return pl.pallas_call(
        functools.partial(_attn_kernel, chunk_mask=chunk_mask),
        out_shape=jax.ShapeDtypeStruct((n, HEADS * HEAD_DIM), BF16),
        grid=(batch, pairs, nq),
        in_specs=[pl.BlockSpec((t, 2 * LANES), lambda b, p, i: (b * nq + i, p)),
                  pl.BlockSpec((seq_len, 2 * LANES), lambda b, p, i: (b, p)),
                  pl.BlockSpec((seq_len, LANES), lambda b, p, i: (b, p))],
        out_specs=pl.BlockSpec((t, LANES), lambda b, p, i: (b * nq + i, p)),
        compiler_params=pltpu.CompilerParams(
            dimension_semantics=("arbitrary", "arbitrary", "arbitrary"),
            vmem_limit_bytes=VMEM_LIMIT),
        name="attn_mla" if chunk_mask else "attn_fox",
    )(q, k, v)


def _outproj_kernel(yf_ref, ym_ref, gate_ref, x_ref, bm_ref, wfo_ref, wmo_ref, wo_ref, gffn_ref,
                    wrh_ref, wrl_ref, br_ref, lstrict_ref,
                    x1_ref, h2_ref, route_ref, cnt_ref, carry_ref):
    i = pl.program_id(0)

    @pl.when(i == 0)
    def _():
        carry_ref[...] = jnp.zeros_like(carry_ref)

    d = x_ref.shape[1]
    tm = x_ref.shape[0]
    a = _dot(yf_ref[...], wfo_ref[...])
    b = _dot(ym_ref[...], wmo_ref[...])
    g = 1.0 / (1.0 + jnp.exp(-(gate_ref[...].astype(F32) + bm_ref[...])))
    merged = (g[:, :d] * a + g[:, d:] * b).astype(BF16)
    x1 = x_ref[...] + _dot(merged, wo_ref[...])
    x1_ref[...] = x1

    h2 = _rms(x1, gffn_ref[...])
    for s in range(d // LANES):
        h2_ref[pl.ds(s, tm, stride=SUBLANES), :] = h2[:, s * LANES:(s + 1) * LANES]

    hi = h2.astype(BF16)
    lo = (h2 - hi.astype(F32)).astype(BF16)
    wrh = wrh_ref[...]
    logits = _dot(hi, wrh) + _dot(lo, wrh) + _dot(hi, wrl_ref[...]) + br_ref[...]

    lane = lax.broadcasted_iota(jnp.int32, (tm, LANES), 1)
    vals = logits
    sels, tops, idxs = [], [], []
    for _ in range(TOP_K):
        mx = jnp.max(vals, axis=-1, keepdims=True)
        idx = jnp.min(jnp.where(vals == mx, lane, LANES), axis=-1, keepdims=True)
        sel = lane == idx
        vals = jnp.where(sel, NEG, vals)
        sels.append(sel)
        tops.append(mx)
        idxs.append(idx)
    es = [jnp.exp(tv - tops[0]) for tv in tops]
    den = es[0] + es[1] + es[2] + es[3]

    onehot = jnp.zeros((tm, LANES), F32)
    for sel in sels:
        onehot = onehot + sel.astype(F32)
    posfull = _dot(lstrict_ref[...], onehot.astype(BF16)) + carry_ref[...]
    carry_ref[...] = carry_ref[...] + jnp.sum(onehot, axis=0, keepdims=True)
    cnt_ref[...] = jnp.broadcast_to(carry_ref[...], cnt_ref.shape)

    route = jnp.zeros((tm, LANES), F32)
    for k in range(TOP_K):
        pos_k = jnp.sum(jnp.where(sels[k], posfull, 0.0), axis=-1, keepdims=True)
        route = jnp.where(lane == k, idxs[k].astype(F32), route)
        route = jnp.where(lane == TOP_K + k, pos_k, route)
        route = jnp.where(lane == 2 * TOP_K + k, es[k] / den, route)
    route_ref[...] = route


def _outproj(yf, ym, gates, x2, bm, wfo, wmo, wo, gffn, wrh, wrl, br, lstrict):
    n, d = x2.shape
    tm = ROW_TILE
    const = lambda i: (0, 0)
    row = lambda i: (i, 0)
    full = lambda a: pl.BlockSpec(a.shape, const)
    consts = (bm, wfo, wmo, wo, gffn, wrh, wrl, br, lstrict)
    sub = d // LANES
    return pl.pallas_call(
        _outproj_kernel,
        out_shape=[jax.ShapeDtypeStruct((n, d), F32),
                   jax.ShapeDtypeStruct((n * sub, LANES), F32),
                   jax.ShapeDtypeStruct((n, LANES), F32),
                   jax.ShapeDtypeStruct((SUBLANES, LANES), F32)],
        grid=(n // tm,),
        in_specs=[pl.BlockSpec((tm, yf.shape[1]), row), pl.BlockSpec((tm, ym.shape[1]), row),
                  pl.BlockSpec((tm, gates.shape[1]), row), pl.BlockSpec((tm, d), row)]
                 + [full(a) for a in consts],
        out_specs=[pl.BlockSpec((tm, d), row), pl.BlockSpec((tm * sub, LANES), row),
                   pl.BlockSpec((tm, LANES), row), pl.BlockSpec((SUBLANES, LANES), const)],
        scratch_shapes=[pltpu.VMEM((1, LANES), F32)],
        compiler_params=pltpu.CompilerParams(dimension_semantics=("arbitrary",),
                                             vmem_limit_bytes=VMEM_LIMIT),
        name="outproj_router",
    )(yf, ym, gates, x2, *consts)


def _expert_kernel(be_ref, nused_ref,
                   h2_hbm, tbl_hbm, wgu_ref, bgu_ref, wd_ref, bd_ref,
                   y_hbm,
                   xbuf, ybuf, tbl_smem, wgu_b, wd_b, gsem, ssem, isem):
    b = pl.program_id(0)
    nused = nused_ref[0]
    rows = EXPERT_ROWS
    sub = xbuf.shape[1] // rows

    def tbl_copy(blk):
        slot = blk % 3
        return pltpu.make_async_copy(tbl_hbm.at[pl.ds(blk * 2 * rows, 2 * rows)],
                                     tbl_smem.at[pl.ds(slot * 2 * rows, 2 * rows)], isem.at[slot])

    def start_gathers(blk):
        base = (blk % 3) * 2 * rows
        xs = blk % 2

        def body(r, _):
            tok = tbl_smem[base + r]
            pltpu.make_async_copy(h2_hbm.at[pl.ds(pl.multiple_of(tok * sub, sub), sub), :],
                                  xbuf.at[xs, pl.ds(pl.multiple_of(r * sub, sub), sub), :],
                                  gsem.at[xs]).start()
            return 0

        lax.fori_loop(0, rows, body, 0, unroll=8)

    def wait_gathers(blk):
        xs = blk % 2
        pltpu.make_async_copy(h2_hbm.at[pl.ds(0, rows * sub), :], xbuf.at[xs], gsem.at[xs]).wait()

    def start_scatters(blk):
        base = (blk % 3) * 2 * rows + rows
        ys = blk % 2

        def body(r, _):
            dst = tbl_smem[base + r]
            pltpu.make_async_copy(ybuf.at[ys, pl.ds(pl.multiple_of(r * sub, sub), sub), :],
                                  y_hbm.at[pl.ds(pl.multiple_of(dst * sub, sub), sub), :],
                                  ssem.at[ys]).start()
            return 0

        lax.fori_loop(0, rows, body, 0, unroll=8)

    def wait_scatters(blk):
        ys = blk % 2
        pltpu.make_async_copy(ybuf.at[ys], y_hbm.at[pl.ds(0, rows * sub), :], ssem.at[ys]).wait()

    @pl.when(b < nused)
    def _():
        @pl.when(b == 0)
        def _():
            tbl_copy(0).start()
            ybuf[...] = jnp.zeros_like(ybuf)
            spare0 = y_hbm.shape[0] - 2 * rows * sub
            fills = [pltpu.make_async_copy(
                ybuf.at[h], y_hbm.at[pl.ds(spare0 + h * rows * sub, rows * sub), :], ssem.at[h])
                for h in range(2)]
            for f in fills:
                f.start()
            for f in fills:
                f.wait()
            tbl_copy(0).wait()
            start_gathers(0)

            @pl.when(nused > 1)
            def _():
                tbl_copy(1).start()

        @pl.when(b + 1 < nused)
        def _():
            tbl_copy(b + 1).wait()
            start_gathers(b + 1)

        @pl.when(b + 2 < nused)
        def _():
            tbl_copy(b + 2).start()

        changed = jnp.logical_or(b == 0, be_ref[b] != be_ref[jnp.maximum(b - 1, 0)])

        @pl.when(changed)
        def _():
            wgu_b[...] = wgu_ref[0].astype(BF16)
            wd_b[...] = wd_ref[0].astype(BF16)

        wait_gathers(b)
        xs = b % 2
        x = jnp.concatenate(
            [xbuf[xs, pl.ds(s, rows, stride=sub), :].astype(BF16) for s in range(sub)], axis=1)
        gu = _dot(x, wgu_b[...]) + bgu_ref[0]
        gate = jnp.minimum(gu[:, :EXPERT_FF], SWIGLU_LIMIT)
        up = jnp.clip(gu[:, EXPERT_FF:], -SWIGLU_LIMIT, SWIGLU_LIMIT)
        glu = gate * (1.0 / (1.0 + jnp.exp(-SWIGLU_ALPHA * gate)))
        y = _dot(((up + 1.0) * glu).astype(BF16), wd_b[...]) + bd_ref[0]

        @pl.when(b >= 2)
        def _():
            wait_scatters(b - 2)

        for s in range(sub):
            ybuf[xs, pl.ds(s, rows, stride=sub), :] = y[:, s * LANES:(s + 1) * LANES]
        start_scatters(b)

        @pl.when(b == nused - 1)
        def _():
            @pl.when(b >= 1)
            def _():
                wait_scatters(b - 1)

            wait_scatters(b)


def _experts(block_e, nused, h2_rows, tbl, w_gu, b_gu, w_d, b_d, *, n_blocks, y_rows):
    e, d, ff2 = w_gu.shape
    rows = EXPERT_ROWS
    sub = d // LANES
    grid_spec = pltpu.PrefetchScalarGridSpec(
        num_scalar_prefetch=2,
        grid=(n_blocks,),
        in_specs=[pl.BlockSpec(memory_space=pl.ANY),
                  pl.BlockSpec(memory_space=pl.ANY),
                  pl.BlockSpec((1, d, ff2), lambda b, be, nu: (be[b], 0, 0)),
                  pl.BlockSpec((1, 1, ff2), lambda b, be, nu: (be[b], 0, 0)),
                  pl.BlockSpec((1, ff2 // 2, d), lambda b, be, nu: (be[b], 0, 0)),
                  pl.BlockSpec((1, 1, d), lambda b, be, nu: (be[b], 0, 0))],
        out_specs=pl.BlockSpec(memory_space=pl.ANY),
        scratch_shapes=[pltpu.VMEM((2, rows * sub, LANES), F32),
                        pltpu.VMEM((2, rows * sub, LANES), F32),
                        pltpu.SMEM((3 * 2 * rows,), jnp.int32),
                        pltpu.VMEM((d, ff2), BF16),
                        pltpu.VMEM((ff2 // 2, d), BF16),
                        pltpu.SemaphoreType.DMA((2,)),
                        pltpu.SemaphoreType.DMA((2,)),
                        pltpu.SemaphoreType.DMA((3,))],
    )
    return pl.pallas_call(
        _expert_kernel,
        out_shape=jax.ShapeDtypeStruct((y_rows * sub, LANES), F32),
        grid_spec=grid_spec,
        compiler_params=pltpu.CompilerParams(dimension_semantics=("arbitrary",),
                                             vmem_limit_bytes=VMEM_LIMIT),
        name="experts",
    )(block_e, nused, h2_rows, tbl, w_gu, b_gu.reshape(e, 1, ff2), w_d, b_d.reshape(e, 1, d))


def _combine_kernel(x1_ref, route_ref, y0_ref, y1_ref, y2_ref, y3_ref, g_ref, o_ref):
    tm, d = x1_ref.shape
    sub = d // LANES
    route = route_ref[...]
    ws = [route[:, 2 * TOP_K + k:2 * TOP_K + k + 1] for k in range(TOP_K)]
    chunks = []
    ssq = jnp.zeros((tm, 1), F32)
    for s in range(sub):
        acc = x1_ref[:, s * LANES:(s + 1) * LANES]
        for w, y_ref in zip(ws, (y0_ref, y1_ref, y2_ref, y3_ref)):
            acc = acc + w * y_ref[pl.ds(s, tm, stride=sub), :]
        chunks.append(acc)
        ssq = ssq + jnp.sum(acc * acc, axis=-1, keepdims=True)
    inv = lax.rsqrt(ssq / d + NORM_EPS)
    for s in range(sub):
        sl = slice(s * LANES, (s + 1) * LANES)
        o_ref[:, sl] = chunks[s] * inv * g_ref[:, sl]


def _combine(x1, route, y_rows, g_final):
    n, d = x1.shape
    tm = ROW_TILE
    sub = d // LANES
    nt = n // tm
    row = lambda i: (i, 0)
    y_specs = [pl.BlockSpec((tm * sub, LANES), (lambda i, k=k: (k * nt + i, 0))) for k in range(TOP_K)]
    return pl.pallas_call(
        _combine_kernel,
        out_shape=jax.ShapeDtypeStruct((n, d), F32),
        grid=(nt,),
        in_specs=[pl.BlockSpec((tm, d), row), pl.BlockSpec((tm, LANES), row)] + y_specs
                 + [pl.BlockSpec((1, d), lambda i: (0, 0))],
        out_specs=pl.BlockSpec((tm, d), row),
        compiler_params=pltpu.CompilerParams(dimension_semantics=("arbitrary",),
                                             vmem_limit_bytes=VMEM_LIMIT),
        name="combine_norm",
    )(x1, route, y_rows, y_rows, y_rows, y_rows, g_final)


def _pad_heads(w, per_head, width=LANES):
    k = w.shape[0]
    w = w.reshape(k, HEADS, per_head)
    w = jnp.pad(w, ((0, 0), (0, 0), (0, width - per_head)))
    return w.reshape(k, HEADS * width)


def _aug_constants():
    hw = HEADS * LANES
    place = np.zeros((LANES, 2 * hw), np.float32)
    ones = np.zeros((1, 2 * hw), np.float32)
    for hd in range(HEADS):
        for piece in range(3):
            src = FG_LO + piece * HEADS + hd
            place[src, hd * LANES + AUG_LO + piece] = 1.0
            place[src, hw + hd * LANES + AUG_LO + 3 + piece] = -1.0
            ones[0, hd * LANES + AUG_LO + 3 + piece] = 1.0
            ones[0, hw + hd * LANES + AUG_LO + piece] = 1.0
    return jnp.asarray(place, BF16), jnp.asarray(ones, F32)


def _layer(x2, pos2, batch, seq_len, g_attn_norm, w_in, b_fgate, g_q_a, w_q_b, g_kv_a, w_kv_b,
           w_fox_out, w_mla_out, b_merge, w_o, g_ffn_norm, w_router, b_router, w_gu, b_gu,
           w_down, b_down, g_out):
    n, d = x2.shape
    fw = HEADS * HEAD_DIM
    o = 0
    w_qf = w_in[:, o:o + fw]; o += fw
    w_kf = w_in[:, o:o + fw]; o += fw
    w_vf = w_in[:, o:o + fw]; o += fw
    w_f = w_in[:, o:o + HEADS]; o += HEADS
    w_ql = w_in[:, o:o + Q_RANK]; o += Q_RANK
    w_ckv = w_in[:, o:o + KV_RANK]; o += KV_RANK
    w_kpe = w_in[:, o:o + ROPE_DIM]; o += ROPE_DIM
    w_gate = w_in[:, o:]

    wq = _pad_heads(w_qf * (HEAD_DIM ** -0.5), HEAD_DIM).astype(BF16)
    wk = _pad_heads(w_kf, HEAD_DIM).astype(BF16)
    wmisc = jnp.concatenate([jnp.zeros((d, PE_LO), F32), w_kpe, w_f,
                             jnp.zeros((d, LANES - FG_LO - HEADS), F32)], axis=1)
    wlat = jnp.concatenate([w_ql, w_ckv, wmisc], axis=1).astype(BF16)
    bf128 = jnp.zeros((1, LANES), F32).at[0, FG_LO:FG_LO + HEADS].set(b_fgate)
    wqb = _pad_heads(w_q_b, HEAD_DIM + ROPE_DIM).astype(BF16)
    wkv = w_kv_b.reshape(KV_RANK, HEADS, 2 * HEAD_DIM)
    wkbk = _pad_heads(wkv[:, :, :HEAD_DIM].reshape(KV_RANK, fw), HEAD_DIM).astype(BF16)
    wkbv = wkv[:, :, HEAD_DIM:].reshape(KV_RANK, fw).astype(BF16)
    tm = ROW_TILE
    tri = np.tril(np.ones((tm, tm), np.float32))
    ltri = jnp.asarray(tri, BF16)
    lstrict = jnp.asarray(tri - np.eye(tm, dtype=np.float32), BF16)
    place, ones = _aug_constants()
    half = ROPE_DIM // 2
    inv_freq = ROPE_THETA ** (-jnp.arange(half, dtype=F32) / half)
    freq = jnp.zeros((1, LANES), F32).at[0, PE_LO:PE_MID].set(inv_freq).at[0, PE_MID:PE_HI].set(inv_freq)

    qf, kf, vf, qm, km, vm, gates = _inproj(
        x2, pos2, g_attn_norm.reshape(1, d), wq, wk, w_vf.astype(BF16), wlat, w_gate.astype(BF16),
        bf128, g_q_a.reshape(1, -1), wqb, g_kv_a.reshape(1, -1), wkbk, wkbv, ltri, place, ones,
        freq, seq_len=seq_len)

    y_fox = _attention(qf, kf, vf, batch=batch, seq_len=seq_len, chunk_mask=False)
    y_mla = _attention(qm, km, vm, batch=batch, seq_len=seq_len, chunk_mask=True)

    wr = jnp.pad(w_router, ((0, 0), (0, LANES - N_EXPERTS)))
    wrh = wr.astype(BF16)
    wrl = (wr - wrh.astype(F32)).astype(BF16)
    br = jnp.full((1, LANES), NEG, F32).at[0, :N_EXPERTS].set(b_router)
    x1, h2_rows, route, cnt = _outproj(
        y_fox, y_mla, gates, x2, b_merge.reshape(1, -1), w_fox_out.astype(BF16),
        w_mla_out.astype(BF16), w_o.astype(BF16), g_ffn_norm.reshape(1, d), wrh, wrl, br, lstrict)

    rows = EXPERT_ROWS
    a = n * TOP_K
    r_total = a + N_EXPERTS * rows
    n_blocks = r_total // rows
    counts = cnt[0, :N_EXPERTS].astype(jnp.int32)
    padded = (counts + rows - 1) // rows * rows
    pad_end = jnp.cumsum(padded)
    pad_start = pad_end - padded
    top_idx = route[:, :TOP_K].astype(jnp.int32)
    pos = route[:, TOP_K:2 * TOP_K].astype(jnp.int32)
    dest = (pad_start[top_idx] + pos).reshape(a)
    tok = jnp.broadcast_to(jnp.arange(n, dtype=jnp.int32)[:, None], (n, TOP_K)).reshape(a)
    slot = (jnp.arange(n, dtype=jnp.int32)[:, None]
            + jnp.arange(TOP_K, dtype=jnp.int32)[None, :] * n).reshape(a)
    ridx = jnp.arange(r_total, dtype=jnp.int32)
    row_tok = jnp.zeros((r_total,), jnp.int32).at[dest].set(tok)
    row_slot = (a + ridx % (2 * rows)).at[dest].set(slot)
    tbl = jnp.concatenate([row_tok.reshape(n_blocks, rows), row_slot.reshape(n_blocks, rows)],
                          axis=1).reshape(-1)
    nused = (pad_end[-1] // rows).astype(jnp.int32)
    blk = jnp.arange(n_blocks, dtype=jnp.int32)
    block_e = jnp.minimum(jnp.searchsorted(pad_end, blk * rows, side="right"), N_EXPERTS - 1)
    block_e = jnp.where(blk < nused, block_e, block_e[jnp.maximum(nused - 1, 0)]).astype(jnp.int32)

    y_rows = _experts(block_e, nused.reshape(1), h2_rows, tbl, w_gu, b_gu, w_down, b_down,
                      n_blocks=n_blocks, y_rows=a + 2 * rows)
    return _combine(x1, route, y_rows, g_out.reshape(1, d))


def kernel(x, positions, g_attn_norm, w_in, b_fgate, g_q_a, w_q_b, g_kv_a, w_kv_b, w_fox_out, w_mla_out, b_merge, w_o, g_ffn_norm, w_router, b_router, w_gu, b_gu, w_down, b_down, g_final):
    batch, seq_len, d = x.shape
    depth = w_in.shape[0]
    assert depth == 1, "the fused combine + final-norm kernel assumes a single layer"
    assert seq_len % ATT_TILE == 0 and d % LANES == 0
    x2 = x.reshape(batch * seq_len, d)
    pos2 = positions.reshape(batch * seq_len, 1).astype(jnp.int32)
    out = _layer(x2, pos2, batch, seq_len, g_attn_norm[0], w_in[0], b_fgate[0], g_q_a[0], w_q_b[0],
                 g_kv_a[0], w_kv_b[0], w_fox_out[0], w_mla_out[0], b_merge[0], w_o[0],
                 g_ffn_norm[0], w_router[0], b_router[0], w_gu[0], b_gu[0], w_down[0], b_down[0],
                 g_final)
    return out.reshape(batch, seq_len, d)
```

```python
import functools

import jax
import jax.numpy as jnp
import numpy as np
from jax import lax
from jax.experimental import pallas as pl
from jax.experimental.pallas import tpu as pltpu

F32 = jnp.float32
BF16 = jnp.bfloat16
I32 = jnp.int32

LANES = 128
SUBLANES = 8
VMEM_LIMIT = 56 * 1024 * 1024

NORM_EPS = 1e-6
HEADS = 8
HEAD_DIM = 64
ROPE_DIM = 32
Q_RANK = 256
KV_RANK = 128
N_EXPERTS = 32
TOP_K = 4
EXPERT_FF = 1024
SWIGLU_LIMIT = 7.0
SWIGLU_ALPHA = 1.702
ROPE_THETA = 10000.0
CHUNK = 64

NEG = -1e30

ROW_TILE = 256
ATT_TILE = 512
ATT_HEADS = 2
EXPERT_ROWS = ROW_TILE

PE_LO, PE_MID, PE_HI = 64, 80, 96
FG_LO = 96
AUG_LO = 64
ROUTE_DEST = 0
ROUTE_W = 8


def _dot(a, b):
    return jnp.dot(a, b, preferred_element_type=F32)


def _dot_nt(a, b):
    return lax.dot_general(a, b, (((1,), (1,)), ((), ())), preferred_element_type=F32)


def _split3(a):
    hi = a.astype(BF16)
    r1 = a - hi.astype(F32)
    mid = r1.astype(BF16)
    lo = (r1 - mid.astype(F32)).astype(BF16)
    return hi, mid, lo


def _rms(x, g):
    return x * lax.rsqrt(jnp.mean(x * x, axis=-1, keepdims=True) + NORM_EPS) * g


def _inproj_kernel(x_ref, pos_ref, g_ref, wq_ref, wk_ref, wv_ref, wlat_ref, wg_ref, bf_ref,
                   gq_ref, wqb_ref, gkv_ref, wkbk_ref, wkbv_ref, ltri_ref, place_ref, ones_ref,
                   freq_ref,
                   qf_ref, kf_ref, vf_ref, qm_ref, km_ref, vm_ref, gate_ref,
                   carry_ref, *, tiles_per_seq, mla_scale):
    i = pl.program_id(0)

    @pl.when(i % tiles_per_seq == 0)
    def _():
        carry_ref[...] = jnp.zeros_like(carry_ref)

    h = _rms(x_ref[...], g_ref[...]).astype(BF16)
    tm = h.shape[0]
    lane = lax.broadcasted_iota(I32, (tm, LANES), 1)

    gate_ref[...] = _dot(h, wg_ref[...]).astype(BF16)
    vf_ref[0] = _dot_nt(wv_ref[...], h).astype(BF16)

    lat = _dot(h, wlat_ref[...])
    q_lat = lat[:, :Q_RANK]
    c_kv = lat[:, Q_RANK:Q_RANK + KV_RANK]
    misc = lat[:, Q_RANK + KV_RANK:]

    z = misc + bf_ref[...]
    logf = jnp.minimum(z, 0.0) - jnp.log1p(jnp.exp(-jnp.abs(z)))
    fmask = (lane >= FG_LO) & (lane < FG_LO + HEADS)
    logf = jnp.where(fmask, logf, 0.0)
    l_hi, l_mid, l_lo = _split3(logf)
    ltri = ltri_ref[...]
    c = _dot(ltri, l_hi) + _dot(ltri, l_mid) + _dot(ltri, l_lo) + carry_ref[...]
    carry_ref[...] = c[tm - 1:tm, :]
    c_hi, c_mid, c_lo = _split3(c)
    c3 = (c_hi.astype(F32) + pltpu.roll(c_mid.astype(F32), HEADS, 1)
          + pltpu.roll(c_lo.astype(F32), 2 * HEADS, 1)).astype(BF16)
    aug = _dot(c3, place_ref[...]) + ones_ref[...]
    hw = HEADS * LANES
    qf_ref[...] = (_dot(h, wq_ref[...]) + aug[:, :hw]).astype(BF16)
    kf_ref[...] = (_dot(h, wk_ref[...]) + aug[:, hw:]).astype(BF16)

    qn = _rms(q_lat, gq_ref[...]).astype(BF16)
    kvn = _rms(c_kv, gkv_ref[...]).astype(BF16)
    qfull = _dot(qn, wqb_ref[...])
    knope = _dot(kvn, wkbk_ref[...])
    vm_ref[0] = _dot_nt(wkbv_ref[...], kvn).astype(BF16)

    ang = pos_ref[...].astype(F32) * freq_ref[...]
    cosv = jnp.cos(ang)
    sinv = jnp.sin(ang)
    s1 = jnp.where((lane >= PE_LO) & (lane < PE_MID), -sinv, 0.0)
    s2 = jnp.where((lane >= PE_MID) & (lane < PE_HI), sinv, 0.0)

    def rope(v):
        return v * cosv + pltpu.roll(v, LANES - 16, 1) * s1 + pltpu.roll(v, 16, 1) * s2

    kpe = jnp.where((lane >= PE_LO) & (lane < PE_HI), rope(misc), 0.0)
    for hd in range(HEADS):
        sl = slice(hd * LANES, (hd + 1) * LANES)
        qm_ref[:, sl] = (rope(qfull[:, sl]) * mla_scale).astype(BF16)
        km_ref[:, sl] = (knope[:, sl] + kpe).astype(BF16)


def _inproj(x2, pos2, g_attn, wq, wk, wv, wlat, wg, bf128, gq, wqb, gkv, wkbk, wkbv, ltri,
            place, ones, freq, *, seq_len):
    n, d = x2.shape
    tm = ROW_TILE
    hw = HEADS * LANES
    vw = HEADS * HEAD_DIM
    const = lambda i: (0, 0)
    row = lambda i: (i, 0)
    full = lambda a: pl.BlockSpec(a.shape, const)
    rows_out = lambda w: (jax.ShapeDtypeStruct((n, w), BF16), pl.BlockSpec((tm, w), row))
    vt_out = (jax.ShapeDtypeStruct((n // tm, vw, tm), BF16),
              pl.BlockSpec((1, vw, tm), lambda i: (i, 0, 0)))
    outs = [rows_out(hw), rows_out(hw), vt_out, rows_out(hw), rows_out(hw), vt_out,
            rows_out(wg.shape[1])]
    consts = (g_attn, wq, wk, wv, wlat, wg, bf128, gq, wqb, gkv, wkbk, wkbv, ltri, place, ones, freq)
    return pl.pallas_call(
        functools.partial(_inproj_kernel, tiles_per_seq=seq_len // tm,
                          mla_scale=float((HEAD_DIM + ROPE_DIM) ** -0.5)),
        out_shape=[o[0] for o in outs],
        grid=(n // tm,),
        in_specs=[pl.BlockSpec((tm, d), row), pl.BlockSpec((tm, 1), row)] + [full(a) for a in consts],
        out_specs=[o[1] for o in outs],
        scratch_shapes=[pltpu.VMEM((1, LANES), F32)],
        compiler_params=pltpu.CompilerParams(dimension_semantics=("arbitrary",),
                                             vmem_limit_bytes=VMEM_LIMIT),
        name="inproj",
    )(x2, pos2, *consts)


def _attn_kernel(q_ref, k_ref, vt_ref, o_ref, *, chunk_mask):
    i = pl.program_id(2)
    tq = q_ref.shape[0]
    tk = vt_ref.shape[2]
    key = lax.broadcasted_iota(I32, (tk, tq), 0)
    qry = lax.broadcasted_iota(I32, (tk, tq), 1) + i * tq

    def scores(j):
        start = pl.multiple_of(j * tk, tk)
        return tuple(_dot_nt(k_ref[pl.ds(start, tk), hd * LANES:(hd + 1) * LANES],
                             q_ref[:, hd * LANES:(hd + 1) * LANES])
                     for hd in range(ATT_HEADS))

    def update(j, ss, state, masked):
        vt = vt_ref[j]
        if masked:
            keyg = key + j * tk
            allowed = (keyg // CHUNK) <= (qry // CHUNK) if chunk_mask else keyg <= qry
        new = []
        for hd in range(ATT_HEADS):
            m, l, acc = state[hd]
            s = jnp.where(allowed, ss[hd], NEG) if masked else ss[hd]
            m_new = jnp.maximum(m, jnp.max(s, axis=0, keepdims=True))
            alpha = jnp.exp(m - m_new)
            p = jnp.exp(s - m_new)
            l = alpha * l + jnp.sum(p, axis=0, keepdims=True)
            acc = alpha * acc + _dot(vt[hd * HEAD_DIM:(hd + 1) * HEAD_DIM, :], p.astype(BF16))
            new.append((m_new, l, acc))
        return tuple(new)

    def body(j, carry):
        ss, state = carry
        nxt = scores(j + 1)
        return nxt, update(j, ss, state, False)

    init1 = (jnp.full((1, tq), NEG, F32), jnp.zeros((1, tq), F32), jnp.zeros((HEAD_DIM, tq), F32))
    n_full = (i * tq) // tk
    n_diag = max(1, tq // tk)
    ss, state = lax.fori_loop(0, n_full, body, (scores(0), (init1,) * ATT_HEADS))
    for dj in range(n_diag):
        nxt = scores(n_full + dj + 1) if dj + 1 < n_diag else None
        state = update(n_full + dj, ss, state, True)
        ss = nxt
    out_t = jnp.concatenate([acc / l for _, l, acc in state], axis=0)
    o_ref[...] = out_t.T.astype(o_ref.dtype)


def _attention(q, k, vt, *, batch, seq_len, chunk_mask):
    n = q.shape[0]
    t = ATT_TILE
    tk = vt.shape[2]
    nq = seq_len // t
    nkb = seq_len // tk
    hp = ATT_HEADS
    pairs = HEADS // hp
    return pl.pallas_call(
        functools.partial(_attn_kernel, chunk_mask=chunk_mask),
        out_shape=jax.ShapeDtypeStruct((n, HEADS * HEAD_DIM), BF16),
        grid=(batch, pairs, nq),
        in_specs=[pl.BlockSpec((t, hp * LANES), lambda b, p, i: (b * nq + i, p)),
                  pl.BlockSpec((seq_len, hp * LANES), lambda b, p, i: (b, p)),
                  pl.BlockSpec((nkb, hp * HEAD_DIM, tk), lambda b, p, i: (b, p, 0))],
        out_specs=pl.BlockSpec((t, hp * HEAD_DIM), lambda b, p, i: (b * nq + i, p)),
        compiler_params=pltpu.CompilerParams(
            dimension_semantics=("arbitrary", "arbitrary", "arbitrary"),
            vmem_limit_bytes=VMEM_LIMIT),
        name="attn_mla" if chunk_mask else "attn_fox",
    )(q, k, vt)


def _outproj_kernel(yf_ref, ym_ref, gate_ref, x_ref, bm_ref, wfo_ref, wmo_ref, wo_ref, gffn_ref,
                    wrh_ref, wrl_ref, br_ref, lstrict_ref, ustrict_ref,
                    x1_ref, route_ref, dest_ref, meta_ref, xs_hbm,
                    fill_ref, cur_ref, nfree_ref, tbl_ref,
                    hbuf, dbuf, dsm, zbuf, mbuf, msm, ssem, isem, zsem, msem):
    i = pl.program_id(0)
    nt = pl.num_programs(0)
    d = x_ref.shape[1]
    tm = x_ref.shape[0]
    sub = d // LANES
    rows = EXPERT_ROWS
    n_blocks = xs_hbm.shape[0] // (rows * sub)
    slot = i % 2

    def idx_copy(s):
        return pltpu.make_async_copy(dbuf.at[s], dsm.at[s], isem.at[s])

    def start_scatters(s):
        def body(t, _):
            src = hbuf.at[s, pl.ds(pl.multiple_of(t * sub, sub), sub), :]
            for k in range(TOP_K):
                dst = dsm[s, k, t]
                pltpu.make_async_copy(src, xs_hbm.at[pl.ds(pl.multiple_of(dst * sub, sub), sub), :],
                                      ssem.at[s]).start()
            return 0

        lax.fori_loop(0, tm, body, 0, unroll=4)

    def wait_scatters(s):
        for _ in range(TOP_K):
            pltpu.make_async_copy(hbuf.at[s], xs_hbm.at[pl.ds(0, tm * sub), :], ssem.at[s]).wait()

    @pl.when(i == 0)
    def _():
        fill_ref[...] = jnp.full_like(fill_ref, float(rows))
        cur_ref[...] = jnp.zeros_like(cur_ref)
        nfree_ref[...] = jnp.zeros_like(nfree_ref)
        tbl_ref[...] = jnp.full_like(tbl_ref, float(N_EXPERTS))
        zbuf[...] = jnp.zeros_like(zbuf)

    @pl.when(i > 0)
    def _():
        idx_copy(1 - slot).wait()
        start_scatters(1 - slot)

    a = _dot(yf_ref[...], wfo_ref[...])
    b = _dot(ym_ref[...], wmo_ref[...])
    g = 1.0 / (1.0 + jnp.exp(-(gate_ref[...].astype(F32) + bm_ref[...])))
    merged = (g[:, :d] * a + g[:, d:] * b).astype(BF16)
    x1 = x_ref[...] + _dot(merged, wo_ref[...])
    x1_ref[...] = x1
    h2 = _rms(x1, gffn_ref[...])

    hi = h2.astype(BF16)
    lo = (h2 - hi.astype(F32)).astype(BF16)
    wrh = wrh_ref[...]
    logits = _dot(hi, wrh) + _dot(lo, wrh) + _dot(hi, wrl_ref[...]) + br_ref[...]

    lane = lax.broadcasted_iota(I32, (tm, LANES), 1)
    vals = logits
    sels, tops = [], []
    for _ in range(TOP_K):
        mx = jnp.max(vals, axis=-1, keepdims=True)
        idx = jnp.min(jnp.where(vals == mx, lane, LANES), axis=-1, keepdims=True)
        sel = lane == idx
        vals = jnp.where(sel, NEG, vals)
        sels.append(sel)
        tops.append(mx)
    es = [jnp.exp(tv - tops[0]) for tv in tops]
    den = es[0] + es[1] + es[2] + es[3]

    onehot = jnp.zeros((tm, LANES), F32)
    for sel in sels:
        onehot = onehot + sel.astype(F32)
    before = _dot(lstrict_ref[...], onehot.astype(BF16))
    cnt = jnp.sum(onehot, axis=0, keepdims=True)

    fill = fill_ref[...]
    cur = cur_ref[...]
    nfree = nfree_ref[...]
    need = ((fill + cnt) > float(rows)).astype(F32)
    need8 = jnp.broadcast_to(need, (SUBLANES, LANES)).astype(BF16)
    newid = nfree + _dot(need8, ustrict_ref[...])[0:1, :]
    pos = fill + before
    dest = jnp.where(pos < float(rows), cur * rows + pos, newid * rows + pos - float(rows))
    fill_ref[...] = fill + cnt - need * float(rows)
    cur_ref[...] = jnp.where(need > 0, newid, cur)
    nfree_ref[...] = nfree + jnp.sum(need, axis=-1, keepdims=True)
    blk_id = (lax.broadcasted_iota(I32, (SUBLANES, LANES), 0) * LANES
              + lax.broadcasted_iota(I32, (SUBLANES, LANES), 1)).astype(F32)
    tbl = tbl_ref[...]
    for e in range(N_EXPERTS):
        hit = (blk_id == newid[:, e:e + 1]) & (need[:, e:e + 1] > 0)
        tbl = jnp.where(hit, float(e), tbl)
    tbl_ref[...] = tbl

    route = jnp.zeros((tm, LANES), F32)
    for k in range(TOP_K):
        dest_k = jnp.sum(jnp.where(sels[k], dest, 0.0), axis=-1, keepdims=True)
        route = jnp.where(lane == ROUTE_DEST + k, dest_k, route)
        route = jnp.where(lane == ROUTE_W + k, es[k] / den, route)
    route_ref[...] = route
    dest_t = route.T[0:SUBLANES, :].astype(I32)
    dest_ref[0] = dest_t

    @pl.when(i >= 2)
    def _():
        wait_scatters(slot)

    for s in range(sub):
        hbuf[slot, pl.ds(s, tm, stride=sub), :] = h2[:, s * LANES:(s + 1) * LANES]
    dbuf[slot] = dest_t
    idx_copy(slot).start()

    @pl.when(i == nt - 1)
    def _():
        idx_copy(slot).wait()
        start_scatters(slot)

        @pl.when(i >= 1)
        def _():
            wait_scatters(1 - slot)

        wait_scatters(slot)

        meta = jnp.concatenate([tbl_ref[...], jnp.broadcast_to(nfree_ref[...], (SUBLANES, LANES))],
                               axis=0).astype(I32)
        meta_ref[...] = meta
        state = jnp.concatenate([fill_ref[...], cur_ref[...], nfree_ref[...],
                                 jnp.zeros((SUBLANES - 3, LANES), F32)], axis=0).astype(I32)
        mbuf[...] = state
        mcopy = pltpu.make_async_copy(mbuf, msm, msem.at[0])
        mcopy.start()
        mcopy.wait()

        def zero_copy(first_row, n_rows):
            return pltpu.make_async_copy(
                zbuf.at[pl.ds(0, n_rows * sub), :],
                xs_hbm.at[pl.ds(pl.multiple_of(first_row * sub, sub), n_rows * sub), :], zsem.at[0])

        chunks = [rows >> (s + 1) for s in range(rows.bit_length() - 1)]
        plans = []
        for e in range(N_EXPERTS):
            rem = rows - msm[0, e]
            at = msm[1, e] * rows + msm[0, e]
            for c in chunks:
                take = (rem & c) != 0
                plans.append((take, zero_copy(at, c)))
                at = at + jnp.where(take, c, 0)
        for j in range(N_EXPERTS):
            blk = msm[2, 0] + j
            safe = jnp.minimum(blk, n_blocks - 1)
            plans.append((blk < n_blocks, zero_copy(safe * rows, rows)))
        for take, cp in plans:
            pl.when(take)(cp.start)
        for take, cp in plans:
            pl.when(take)(cp.wait)


def _outproj(yf, ym, gates, x2, bm, wfo, wmo, wo, gffn, wrh, wrl, br, lstrict, ustrict, *, n_blocks):
    n, d = x2.shape
    tm = ROW_TILE
    rows = EXPERT_ROWS
    sub = d // LANES
    const = lambda i: (0, 0)
    row = lambda i: (i, 0)
    full = lambda a: pl.BlockSpec(a.shape, const)
    consts = (bm, wfo, wmo, wo, gffn, wrh, wrl, br, lstrict, ustrict)
    return pl.pallas_call(
        _outproj_kernel,
        out_shape=[jax.ShapeDtypeStruct((n, d), F32),
                   jax.ShapeDtypeStruct((n, LANES), F32),
                   jax.ShapeDtypeStruct((n // tm, SUBLANES, tm), I32),
                   jax.ShapeDtypeStruct((2 * SUBLANES, LANES), I32),
                   jax.ShapeDtypeStruct((n_blocks * rows * sub, LANES), F32)],
        grid=(n // tm,),
        in_specs=[pl.BlockSpec((tm, yf.shape[1]), row), pl.BlockSpec((tm, ym.shape[1]), row),
                  pl.BlockSpec((tm, gates.shape[1]), row), pl.BlockSpec((tm, d), row)]
                 + [full(a) for a in consts],
        out_specs=[pl.BlockSpec((tm, d), row), pl.BlockSpec((tm, LANES), row),
                   pl.BlockSpec((1, SUBLANES, tm), lambda i: (i, 0, 0)),
                   pl.BlockSpec((2 * SUBLANES, LANES), const),
                   pl.BlockSpec(memory_space=pl.ANY)],
        scratch_shapes=[pltpu.VMEM((1, LANES), F32), pltpu.VMEM((1, LANES), F32),
                        pltpu.VMEM((1, LANES), F32), pltpu.VMEM((SUBLANES, LANES), F32),
                        pltpu.VMEM((2, tm * sub, LANES), F32),
                        pltpu.VMEM((2, SUBLANES, tm), I32),
                        pltpu.SMEM((2, SUBLANES, tm), I32),
                        pltpu.VMEM((rows * sub, LANES), F32),
                        pltpu.VMEM((SUBLANES, LANES), I32),
                        pltpu.SMEM((SUBLANES, LANES), I32),
                        pltpu.SemaphoreType.DMA((2,)), pltpu.SemaphoreType.DMA((2,)),
                        pltpu.SemaphoreType.DMA((1,)), pltpu.SemaphoreType.DMA((1,))],
        compiler_params=pltpu.CompilerParams(dimension_semantics=("arbitrary",),
                                             vmem_limit_bytes=VMEM_LIMIT),
        name="outproj_router",
    )(yf, ym, gates, x2, *consts)


def _expert_kernel(order_ref, be_ref, nused_ref,
                   xs_ref, wgu_ref, bgu_ref, wd_ref, bd_ref, ys_ref, wgu_b, wd_b):
    b = pl.program_id(0)
    rows = EXPERT_ROWS
    sub = xs_ref.shape[0] // rows

    @pl.when(b < nused_ref[0])
    def _():
        changed = jnp.logical_or(b == 0, be_ref[b] != be_ref[jnp.maximum(b - 1, 0)])

        @pl.when(changed)
        def _():
            wgu_b[...] = wgu_ref[0].astype(BF16)
            wd_b[...] = wd_ref[0].astype(BF16)

        x = jnp.concatenate(
            [xs_ref[pl.ds(s, rows, stride=sub), :].astype(BF16) for s in range(sub)], axis=1)
        gu = _dot(x, wgu_b[...]) + bgu_ref[0]
        gate = jnp.minimum(gu[:, :EXPERT_FF], SWIGLU_LIMIT)
        up = jnp.clip(gu[:, EXPERT_FF:], -SWIGLU_LIMIT, SWIGLU_LIMIT)
        glu = gate * (1.0 / (1.0 + jnp.exp(-SWIGLU_ALPHA * gate)))
        y = _dot(((up + 1.0) * glu).astype(BF16), wd_b[...]) + bd_ref[0]
        for s in range(sub):
            ys_ref[pl.ds(s, rows, stride=sub), :] = y[:, s * LANES:(s + 1) * LANES]

    @pl.when(b >= nused_ref[0])
    def _():
        ys_ref[...] = jnp.zeros_like(ys_ref)


def _experts(order, block_e, nused, xs, w_gu, b_gu, w_d, b_d):
    e, d, ff2 = w_gu.shape
    rows = EXPERT_ROWS
    sub = d // LANES
    n_blocks = xs.shape[0] // (rows * sub)
    wmap = lambda b, od, be, nu: (be[b], 0, 0)
    xmap = lambda b, od, be, nu: (od[b], 0)
    grid_spec = pltpu.PrefetchScalarGridSpec(
        num_scalar_prefetch=3,
        grid=(n_blocks,),
        in_specs=[pl.BlockSpec((rows * sub, LANES), xmap),
                  pl.BlockSpec((1, d, ff2), wmap), pl.BlockSpec((1, 1, ff2), wmap),
                  pl.BlockSpec((1, ff2 // 2, d), wmap), pl.BlockSpec((1, 1, d), wmap)],
        out_specs=pl.BlockSpec((rows * sub, LANES), xmap),
        scratch_shapes=[pltpu.VMEM((d, ff2), BF16), pltpu.VMEM((ff2 // 2, d), BF16)],
    )
    return pl.pallas_call(
        _expert_kernel,
        out_shape=jax.ShapeDtypeStruct(xs.shape, F32),
        grid_spec=grid_spec,
        compiler_params=pltpu.CompilerParams(dimension_semantics=("arbitrary",),
                                             vmem_limit_bytes=VMEM_LIMIT),
        name="experts",
    )(order, block_e, nused, xs, w_gu, b_gu.reshape(e, 1, ff2), w_d, b_d.reshape(e, 1, d))


def _combine_kernel(x1_ref, route_ref, dest_hbm, ys_hbm, g_ref, o_ref, ybuf, dsm, gsem, isem):
    i = pl.program_id(0)
    nt = pl.num_programs(0)
    tm, d = x1_ref.shape
    sub = d // LANES
    slot = i % 2

    def idx_copy(tile):
        s = tile % 2
        return pltpu.make_async_copy(dest_hbm.at[tile], dsm.at[s], isem.at[s])

    def start_gathers(tile):
        s = tile % 2

        def body(t, _):
            for k in range(TOP_K):
                src = dsm[s, k, t]
                pltpu.make_async_copy(
                    ys_hbm.at[pl.ds(pl.multiple_of(src * sub, sub), sub), :],
                    ybuf.at[s, pl.ds(pl.multiple_of((k * tm + t) * sub, sub), sub), :],
                    gsem.at[s]).start()
            return 0

        lax.fori_loop(0, tm, body, 0, unroll=4)

    def wait_gathers(s):
        pltpu.make_async_copy(ys_hbm.at[pl.ds(0, TOP_K * tm * sub), :], ybuf.at[s], gsem.at[s]).wait()

    @pl.when(i == 0)
    def _():
        idx_copy(0).start()
        idx_copy(0).wait()
        start_gathers(0)

        @pl.when(nt > 1)
        def _():
            idx_copy(1).start()

    @pl.when(i + 1 < nt)
    def _():
        idx_copy(i + 1).wait()
        start_gathers(i + 1)

    @pl.when(i + 2 < nt)
    def _():
        idx_copy(i + 2).start()

    wait_gathers(slot)
    route = route_ref[...]
    ws = [route[:, ROUTE_W + k:ROUTE_W + k + 1] for k in range(TOP_K)]
    chunks = []
    ssq = jnp.zeros((tm, 1), F32)
    for s in range(sub):
        acc = x1_ref[:, s * LANES:(s + 1) * LANES]
        for k in range(TOP_K):
            acc = acc + ws[k] * ybuf[slot, pl.ds(k * tm * sub + s, tm, stride=sub), :]
        chunks.append(acc)
        ssq = ssq + jnp.sum(acc * acc, axis=-1, keepdims=True)
    inv = lax.rsqrt(ssq / d + NORM_EPS)
    for s in range(sub):
        sl = slice(s * LANES, (s + 1) * LANES)
        o_ref[:, sl] = chunks[s] * inv * g_ref[:, sl]


def _combine(x1, route, dest_t, ys, g_final):
    n, d = x1.shape
    tm = ROW_TILE
    sub = d // LANES
    row = lambda i: (i, 0)
    return pl.pallas_call(
        _combine_kernel,
        out_shape=jax.ShapeDtypeStruct((n, d), F32),
        grid=(n // tm,),
        in_specs=[pl.BlockSpec((tm, d), row), pl.BlockSpec((tm, LANES), row),
                  pl.BlockSpec(memory_space=pl.ANY), pl.BlockSpec(memory_space=pl.ANY),
                  pl.BlockSpec((1, d), lambda i: (0, 0))],
        out_specs=pl.BlockSpec((tm, d), row),
        scratch_shapes=[pltpu.VMEM((2, TOP_K * tm * sub, LANES), F32),
                        pltpu.SMEM((2, SUBLANES, tm), I32),
                        pltpu.SemaphoreType.DMA((2,)), pltpu.SemaphoreType.DMA((2,))],
        compiler_params=pltpu.CompilerParams(dimension_semantics=("arbitrary",),
                                             vmem_limit_bytes=VMEM_LIMIT),
        name="combine_norm",
    )(x1, route, dest_t, ys, g_final)


def _pad_heads(w, per_head, width=LANES):
    k = w.shape[0]
    w = w.reshape(k, HEADS, per_head)
    w = jnp.pad(w, ((0, 0), (0, 0), (0, width - per_head)))
    return w.reshape(k, HEADS * width)


def _aug_constants():
    hw = HEADS * LANES
    place = np.zeros((LANES, 2 * hw), np.float32)
    ones = np.zeros((1, 2 * hw), np.float32)
    for hd in range(HEADS):
        for piece in range(3):
            src = FG_LO + piece * HEADS + hd
            place[src, hd * LANES + AUG_LO + piece] = 1.0
            place[src, hw + hd * LANES + AUG_LO + 3 + piece] = -1.0
            ones[0, hd * LANES + AUG_LO + 3 + piece] = 1.0
            ones[0, hw + hd * LANES + AUG_LO + piece] = 1.0
    return jnp.asarray(place, BF16), jnp.asarray(ones, F32)


def _layer(x2, pos2, batch, seq_len, g_attn_norm, w_in, b_fgate, g_q_a, w_q_b, g_kv_a, w_kv_b,
           w_fox_out, w_mla_out, b_merge, w_o, g_ffn_norm, w_router, b_router, w_gu, b_gu,
           w_down, b_down, g_out):
    n, d = x2.shape
    fw = HEADS * HEAD_DIM
    o = 0
    w_qf = w_in[:, o:o + fw]; o += fw
    w_kf = w_in[:, o:o + fw]; o += fw
    w_vf = w_in[:, o:o + fw]; o += fw
    w_f = w_in[:, o:o + HEADS]; o += HEADS
    w_ql = w_in[:, o:o + Q_RANK]; o += Q_RANK
    w_ckv = w_in[:, o:o + KV_RANK]; o += KV_RANK
    w_kpe = w_in[:, o:o + ROPE_DIM]; o += ROPE_DIM
    w_gate = w_in[:, o:]

    wq = _pad_heads(w_qf * (HEAD_DIM ** -0.5), HEAD_DIM).astype(BF16)
    wk = _pad_heads(w_kf, HEAD_DIM).astype(BF16)
    wmisc = jnp.concatenate([jnp.zeros((d, PE_LO), F32), w_kpe, w_f,
                             jnp.zeros((d, LANES - FG_LO - HEADS), F32)], axis=1)
    wlat = jnp.concatenate([w_ql, w_ckv, wmisc], axis=1).astype(BF16)
    bf128 = jnp.zeros((1, LANES), F32).at[0, FG_LO:FG_LO + HEADS].set(b_fgate)
    wqb = _pad_heads(w_q_b, HEAD_DIM + ROPE_DIM).astype(BF16)
    wkv = w_kv_b.reshape(KV_RANK, HEADS, 2 * HEAD_DIM)
    wkbk = _pad_heads(wkv[:, :, :HEAD_DIM].reshape(KV_RANK, fw), HEAD_DIM).astype(BF16)
    wkbv = wkv[:, :, HEAD_DIM:].reshape(KV_RANK, fw).T.astype(BF16)
    tm = ROW_TILE
    tri = np.tril(np.ones((tm, tm), np.float32))
    ltri = jnp.asarray(tri, BF16)
    lstrict = jnp.asarray(tri - np.eye(tm, dtype=np.float32), BF16)
    ustrict = jnp.asarray(np.triu(np.ones((LANES, LANES), np.float32), 1), BF16)
    place, ones = _aug_constants()
    half = ROPE_DIM // 2
    inv_freq = ROPE_THETA ** (-jnp.arange(half, dtype=F32) / half)
    freq = jnp.zeros((1, LANES), F32).at[0, PE_LO:PE_MID].set(inv_freq).at[0, PE_MID:PE_HI].set(inv_freq)

    qf, kf, vf, qm, km, vm, gates = _inproj(
        x2, pos2, g_attn_norm.reshape(1, d), wq, wk, w_vf.T.astype(BF16), wlat, w_gate.astype(BF16),
        bf128, g_q_a.reshape(1, -1), wqb, g_kv_a.reshape(1, -1), wkbk, wkbv, ltri, place, ones,
        freq, seq_len=seq_len)

    y_fox = _attention(qf, kf, vf, batch=batch, seq_len=seq_len, chunk_mask=False)
    y_mla = _attention(qm, km, vm, batch=batch, seq_len=seq_len, chunk_mask=True)

    wr = jnp.pad(w_router, ((0, 0), (0, LANES - N_EXPERTS)))
    wrh = wr.astype(BF16)
    wrl = (wr - wrh.astype(F32)).astype(BF16)
    br = jnp.full((1, LANES), NEG, F32).at[0, :N_EXPERTS].set(b_router)
    n_blocks = n * TOP_K // EXPERT_ROWS + N_EXPERTS
    x1, route, dest_t, meta, xs = _outproj(
        y_fox, y_mla, gates, x2, b_merge.reshape(1, -1), w_fox_out.astype(BF16),
        w_mla_out.astype(BF16), w_o.astype(BF16), g_ffn_norm.reshape(1, d), wrh, wrl, br, lstrict,
        ustrict, n_blocks=n_blocks)

    block_e = meta[:SUBLANES].reshape(-1)[:n_blocks]
    order = jnp.argsort(block_e, stable=True).astype(I32)
    nused = meta[SUBLANES, 0:1]
    be_sorted = jnp.minimum(block_e[order], N_EXPERTS - 1).astype(I32)
    be_sorted = jnp.where(jnp.arange(n_blocks) < nused[0], be_sorted,
                          be_sorted[jnp.maximum(nused[0] - 1, 0)])
    ys = _experts(order, be_sorted, nused, xs, w_gu, b_gu, w_down, b_down)
    return _combine(x1, route, dest_t, ys, g_out.reshape(1, d))


def kernel(x, positions, g_attn_norm, w_in, b_fgate, g_q_a, w_q_b, g_kv_a, w_kv_b, w_fox_out, w_mla_out, b_merge, w_o, g_ffn_norm, w_router, b_router, w_gu, b_gu, w_down, b_down, g_final):
    batch, seq_len, d = x.shape
    depth = w_in.shape[0]
    assert depth == 1, "the fused combine + final-norm kernel assumes a single layer"
    assert seq_len % ATT_TILE == 0 and d % LANES == 0
    assert (batch * seq_len * TOP_K) % EXPERT_ROWS == 0
    assert batch * seq_len * TOP_K // EXPERT_ROWS + N_EXPERTS <= SUBLANES * LANES
    x2 = x.reshape(batch * seq_len, d)
    pos2 = positions.reshape(batch * seq_len, 1).astype(I32)
    out = _layer(x2, pos2, batch, seq_len, g_attn_norm[0], w_in[0], b_fgate[0], g_q_a[0], w_q_b[0],
                 g_kv_a[0], w_kv_b[0], w_fox_out[0], w_mla_out[0], b_merge[0], w_o[0],
                 g_ffn_norm[0], w_router[0], b_router[0], w_gu[0], b_gu[0], w_down[0], b_down[0],
                 g_final)
    return out.reshape(batch, seq_len, d)
```

```python
import functools

import jax
import jax.numpy as jnp
import numpy as np
from jax import lax
from jax.experimental import pallas as pl
from jax.experimental.pallas import tpu as pltpu

F32 = jnp.float32
BF16 = jnp.bfloat16
I32 = jnp.int32

LANES = 128
SUBLANES = 8
VMEM_LIMIT = 56 * 1024 * 1024

NORM_EPS = 1e-6
HEADS = 8
HEAD_DIM = 64
ROPE_DIM = 32
Q_RANK = 256
KV_RANK = 128
N_EXPERTS = 32
TOP_K = 4
EXPERT_FF = 1024
SWIGLU_LIMIT = 7.0
SWIGLU_ALPHA = 1.702
ROPE_THETA = 10000.0
CHUNK = 64

NEG = -1e30

ROW_TILE = 256
ATT_TILE = 512
ATT_HEADS = 2
EXPERT_ROWS = ROW_TILE

PE_LO, PE_MID, PE_HI = 64, 80, 96
FG_LO = 96
AUG_LO = 64
ROUTE_DEST = 0
ROUTE_W = 8


def _dot(a, b):
    return jnp.dot(a, b, preferred_element_type=F32)


def _dot_nt(a, b):
    return lax.dot_general(a, b, (((1,), (1,)), ((), ())), preferred_element_type=F32)


def _split3(a):
    hi = a.astype(BF16)
    r1 = a - hi.astype(F32)
    mid = r1.astype(BF16)
    lo = (r1 - mid.astype(F32)).astype(BF16)
    return hi, mid, lo


def _rms(x, g):
    return x * lax.rsqrt(jnp.mean(x * x, axis=-1, keepdims=True) + NORM_EPS) * g


def _inproj_kernel(x_ref, pos_ref, g_ref, wq_ref, wk_ref, wv_ref, wlat_ref, wg_ref, bf_ref,
                   gq_ref, wqb_ref, gkv_ref, wkbk_ref, wkbv_ref, ltri_ref, place_ref, ones_ref,
                   freq_ref,
                   qf_ref, kf_ref, vf_ref, qm_ref, km_ref, vm_ref, gate_ref,
                   carry_ref, *, tiles_per_seq, mla_scale):
    i = pl.program_id(0)

    @pl.when(i % tiles_per_seq == 0)
    def _():
        carry_ref[...] = jnp.zeros_like(carry_ref)

    h = _rms(x_ref[...], g_ref[...]).astype(BF16)
    tm = h.shape[0]
    lane = lax.broadcasted_iota(I32, (tm, LANES), 1)

    gate_ref[...] = _dot(h, wg_ref[...]).astype(BF16)
    vf_ref[0] = _dot_nt(wv_ref[...], h).astype(BF16)

    lat = _dot(h, wlat_ref[...])
    q_lat = lat[:, :Q_RANK]
    c_kv = lat[:, Q_RANK:Q_RANK + KV_RANK]
    misc = lat[:, Q_RANK + KV_RANK:]

    z = misc + bf_ref[...]
    logf = jnp.minimum(z, 0.0) - jnp.log1p(jnp.exp(-jnp.abs(z)))
    fmask = (lane >= FG_LO) & (lane < FG_LO + HEADS)
    logf = jnp.where(fmask, logf, 0.0)
    l_hi, l_mid, l_lo = _split3(logf)
    ltri = ltri_ref[...]
    c = _dot(ltri, l_hi) + _dot(ltri, l_mid) + _dot(ltri, l_lo) + carry_ref[...]
    carry_ref[...] = c[tm - 1:tm, :]
    c_hi, c_mid, c_lo = _split3(c)
    c3 = (c_hi.astype(F32) + pltpu.roll(c_mid.astype(F32), HEADS, 1)
          + pltpu.roll(c_lo.astype(F32), 2 * HEADS, 1)).astype(BF16)
    aug = _dot(c3, place_ref[...]) + ones_ref[...]
    hw = HEADS * LANES
    qf_ref[...] = (_dot(h, wq_ref[...]) + aug[:, :hw]).astype(BF16)
    kf_ref[...] = (_dot(h, wk_ref[...]) + aug[:, hw:]).astype(BF16)

    qn = _rms(q_lat, gq_ref[...]).astype(BF16)
    kvn = _rms(c_kv, gkv_ref[...]).astype(BF16)
    qfull = _dot(qn, wqb_ref[...])
    knope = _dot(kvn, wkbk_ref[...])
    vm_ref[0] = _dot_nt(wkbv_ref[...], kvn).astype(BF16)

    ang = pos_ref[...].astype(F32) * freq_ref[...]
    cosv = jnp.cos(ang)
    sinv = jnp.sin(ang)
    s1 = jnp.where((lane >= PE_LO) & (lane < PE_MID), -sinv, 0.0)
    s2 = jnp.where((lane >= PE_MID) & (lane < PE_HI), sinv, 0.0)

    def rope(v):
        return v * cosv + pltpu.roll(v, LANES - 16, 1) * s1 + pltpu.roll(v, 16, 1) * s2

    kpe = jnp.where((lane >= PE_LO) & (lane < PE_HI), rope(misc), 0.0)
    for hd in range(HEADS):
        sl = slice(hd * LANES, (hd + 1) * LANES)
        qm_ref[:, sl] = (rope(qfull[:, sl]) * mla_scale).astype(BF16)
        km_ref[:, sl] = (knope[:, sl] + kpe).astype(BF16)


def _inproj(x2, pos2, g_attn, wq, wk, wv, wlat, wg, bf128, gq, wqb, gkv, wkbk, wkbv, ltri,
            place, ones, freq, *, seq_len):
    n, d = x2.shape
    tm = ROW_TILE
    hw = HEADS * LANES
    vw = HEADS * HEAD_DIM
    const = lambda i: (0, 0)
    row = lambda i: (i, 0)
    full = lambda a: pl.BlockSpec(a.shape, const)
    rows_out = lambda w: (jax.ShapeDtypeStruct((n, w), BF16), pl.BlockSpec((tm, w), row))
    vt_out = (jax.ShapeDtypeStruct((n // tm, vw, tm), BF16),
              pl.BlockSpec((1, vw, tm), lambda i: (i, 0, 0)))
    outs = [rows_out(hw), rows_out(hw), vt_out, rows_out(hw), rows_out(hw), vt_out,
            rows_out(wg.shape[1])]
    consts = (g_attn, wq, wk, wv, wlat, wg, bf128, gq, wqb, gkv, wkbk, wkbv, ltri, place, ones, freq)
    return pl.pallas_call(
        functools.partial(_inproj_kernel, tiles_per_seq=seq_len // tm,
                          mla_scale=float((HEAD_DIM + ROPE_DIM) ** -0.5)),
        out_shape=[o[0] for o in outs],
        grid=(n // tm,),
        in_specs=[pl.BlockSpec((tm, d), row), pl.BlockSpec((tm, 1), row)] + [full(a) for a in consts],
        out_specs=[o[1] for o in outs],
        scratch_shapes=[pltpu.VMEM((1, LANES), F32)],
        compiler_params=pltpu.CompilerParams(dimension_semantics=("arbitrary",),
                                             vmem_limit_bytes=VMEM_LIMIT),
        name="inproj",
    )(x2, pos2, *consts)


def _attn_kernel(q_ref, k_ref, vt_ref, o_ref, *, chunk_mask):
    i = pl.program_id(2)
    tq = q_ref.shape[0]
    tk = vt_ref.shape[2]
    key = lax.broadcasted_iota(I32, (tk, tq), 0)
    qry = lax.broadcasted_iota(I32, (tk, tq), 1) + i * tq

    def scores(j, masked):
        start = pl.multiple_of(j * tk, tk)
        ss = [_dot_nt(k_ref[pl.ds(start, tk), hd * LANES:(hd + 1) * LANES],
                      q_ref[:, hd * LANES:(hd + 1) * LANES])
              for hd in range(ATT_HEADS)]
        if masked:
            keyg = key + j * tk
            allowed = (keyg // CHUNK) <= (qry // CHUNK) if chunk_mask else keyg <= qry
            ss = [jnp.where(allowed, s, NEG) for s in ss]
        return ss

    def update(j, ss, state):
        vt = vt_ref[j]
        new = []
        for hd, s in enumerate(ss):
            m, l, acc = state[hd]
            m_new = jnp.maximum(m, jnp.max(s, axis=0, keepdims=True))
            alpha = jnp.exp(m - m_new)
            p = jnp.exp(s - m_new)
            l = alpha * l + jnp.sum(p, axis=0, keepdims=True)
            acc = alpha * acc + _dot(vt[hd * HEAD_DIM:(hd + 1) * HEAD_DIM, :], p.astype(BF16))
            new.append((m_new, l, acc))
        return tuple(new)

    n_full = (i * tq) // tk
    group = max(1, tq // tk)

    def run_group(first, state, masked):
        ss = scores(first, masked)
        for u in range(group):
            nxt = scores(first + u + 1, masked) if u + 1 < group else None
            state = update(first + u, ss, state)
            ss = nxt
        return state

    init1 = (jnp.full((1, tq), NEG, F32), jnp.zeros((1, tq), F32), jnp.zeros((HEAD_DIM, tq), F32))
    state = lax.fori_loop(0, n_full // group, lambda jj, c: run_group(jj * group, c, False),
                          (init1,) * ATT_HEADS)
    state = run_group(n_full, state, True)
    out_t = jnp.concatenate([acc / l for _, l, acc in state], axis=0)
    o_ref[...] = out_t.T.astype(o_ref.dtype)


def _attention(q, k, vt, *, batch, seq_len, chunk_mask):
    n = q.shape[0]
    t = ATT_TILE
    tk = vt.shape[2]
    nq = seq_len // t
    nkb = seq_len // tk
    hp = ATT_HEADS
    pairs = HEADS // hp
    return pl.pallas_call(
        functools.partial(_attn_kernel, chunk_mask=chunk_mask),
        out_shape=jax.ShapeDtypeStruct((n, HEADS * HEAD_DIM), BF16),
        grid=(batch, pairs, nq),
        in_specs=[pl.BlockSpec((t, hp * LANES), lambda b, p, i: (b * nq + i, p)),
                  pl.BlockSpec((seq_len, hp * LANES), lambda b, p, i: (b, p)),
                  pl.BlockSpec((nkb, hp * HEAD_DIM, tk), lambda b, p, i: (b, p, 0))],
        out_specs=pl.BlockSpec((t, hp * HEAD_DIM), lambda b, p, i: (b * nq + i, p)),
        compiler_params=pltpu.CompilerParams(
            dimension_semantics=("arbitrary", "arbitrary", "arbitrary"),
            vmem_limit_bytes=VMEM_LIMIT),
        name="attn_mla" if chunk_mask else "attn_fox",
    )(q, k, vt)


def _outproj_kernel(yf_ref, ym_ref, gate_ref, x_ref, bm_ref, wfo_ref, wmo_ref, wo_ref, gffn_ref,
                    wrh_ref, wrl_ref, br_ref, lstrict_ref, ustrict_ref,
                    x1_ref, route_ref, dest_ref, meta_ref, xs_hbm,
                    fill_ref, cur_ref, nfree_ref, tbl_ref,
                    hbuf, dbuf, dsm, zbuf, mbuf, msm, ssem, isem, zsem, msem):
    i = pl.program_id(0)
    nt = pl.num_programs(0)
    d = x_ref.shape[1]
    tm = x_ref.shape[0]
    sub = d // LANES
    rows = EXPERT_ROWS
    n_blocks = xs_hbm.shape[0] // (rows * sub)
    slot = i % 2

    def idx_copy(s):
        return pltpu.make_async_copy(dbuf.at[s], dsm.at[s], isem.at[s])

    def start_scatters(s):
        def body(t, _):
            src = hbuf.at[s, pl.ds(pl.multiple_of(t * sub, sub), sub), :]
            for k in range(TOP_K):
                dst = dsm[s, k, t]
                pltpu.make_async_copy(src, xs_hbm.at[pl.ds(pl.multiple_of(dst * sub, sub), sub), :],
                                      ssem.at[s]).start()
            return 0

        lax.fori_loop(0, tm, body, 0, unroll=4)

    def wait_scatters(s):
        for _ in range(TOP_K):
            pltpu.make_async_copy(hbuf.at[s], xs_hbm.at[pl.ds(0, tm * sub), :], ssem.at[s]).wait()

    @pl.when(i == 0)
    def _():
        fill_ref[...] = jnp.full_like(fill_ref, float(rows))
        cur_ref[...] = jnp.zeros_like(cur_ref)
        nfree_ref[...] = jnp.zeros_like(nfree_ref)
        tbl_ref[...] = jnp.full_like(tbl_ref, float(N_EXPERTS))
        zbuf[...] = jnp.zeros_like(zbuf)

    @pl.when(i > 0)
    def _():
        idx_copy(1 - slot).wait()
        start_scatters(1 - slot)

    a = _dot(yf_ref[...], wfo_ref[...])
    b = _dot(ym_ref[...], wmo_ref[...])
    g = 1.0 / (1.0 + jnp.exp(-(gate_ref[...].astype(F32) + bm_ref[...])))
    merged = (g[:, :d] * a + g[:, d:] * b).astype(BF16)
    x1 = x_ref[...] + _dot(merged, wo_ref[...])
    x1_ref[...] = x1
    h2 = _rms(x1, gffn_ref[...])

    hi = h2.astype(BF16)
    lo = (h2 - hi.astype(F32)).astype(BF16)
    wrh = wrh_ref[...]
    logits = _dot(hi, wrh) + _dot(lo, wrh) + _dot(hi, wrl_ref[...]) + br_ref[...]

    lane = lax.broadcasted_iota(I32, (tm, LANES), 1)
    vals = logits
    sels, tops = [], []
    for _ in range(TOP_K):
        mx = jnp.max(vals, axis=-1, keepdims=True)
        idx = jnp.min(jnp.where(vals == mx, lane, LANES), axis=-1, keepdims=True)
        sel = lane == idx
        vals = jnp.where(sel, NEG, vals)
        sels.append(sel)
        tops.append(mx)
    es = [jnp.exp(tv - tops[0]) for tv in tops]
    den = es[0] + es[1] + es[2] + es[3]

    onehot = jnp.zeros((tm, LANES), F32)
    for sel in sels:
        onehot = onehot + sel.astype(F32)
    before = _dot(lstrict_ref[...], onehot.astype(BF16))
    cnt = jnp.sum(onehot, axis=0, keepdims=True)

    fill = fill_ref[...]
    cur = cur_ref[...]
    nfree = nfree_ref[...]
    need = ((fill + cnt) > float(rows)).astype(F32)
    need8 = jnp.broadcast_to(need, (SUBLANES, LANES)).astype(BF16)
    newid = nfree + _dot(need8, ustrict_ref[...])[0:1, :]
    pos = fill + before
    dest = jnp.where(pos < float(rows), cur * rows + pos, newid * rows + pos - float(rows))
    fill_ref[...] = fill + cnt - need * float(rows)
    cur_ref[...] = jnp.where(need > 0, newid, cur)
    nfree_ref[...] = nfree + jnp.sum(need, axis=-1, keepdims=True)
    blk_id = (lax.broadcasted_iota(I32, (SUBLANES, LANES), 0) * LANES
              + lax.broadcasted_iota(I32, (SUBLANES, LANES), 1)).astype(F32)
    tbl = tbl_ref[...]
    for e in range(N_EXPERTS):
        hit = (blk_id == newid[:, e:e + 1]) & (need[:, e:e + 1] > 0)
        tbl = jnp.where(hit, float(e), tbl)
    tbl_ref[...] = tbl

    route = jnp.zeros((tm, LANES), F32)
    for k in range(TOP_K):
        dest_k = jnp.sum(jnp.where(sels[k], dest, 0.0), axis=-1, keepdims=True)
        route = jnp.where(lane == ROUTE_DEST + k, dest_k, route)
        route = jnp.where(lane == ROUTE_W + k, es[k] / den, route)
    route_ref[...] = route
    dest_t = route.T[0:SUBLANES, :].astype(I32)
    dest_ref[0] = dest_t

    @pl.when(i >= 2)
    def _():
        wait_scatters(slot)

    for s in range(sub):
        hbuf[slot, pl.ds(s, tm, stride=sub), :] = h2[:, s * LANES:(s + 1) * LANES]
    dbuf[slot] = dest_t
    idx_copy(slot).start()

    @pl.when(i == nt - 1)
    def _():
        idx_copy(slot).wait()
        start_scatters(slot)

        @pl.when(i >= 1)
        def _():
            wait_scatters(1 - slot)

        wait_scatters(slot)

        meta = jnp.concatenate([tbl_ref[...], jnp.broadcast_to(nfree_ref[...], (SUBLANES, LANES))],
                               axis=0).astype(I32)
        meta_ref[...] = meta
        state = jnp.concatenate([fill_ref[...], cur_ref[...], nfree_ref[...],
                                 jnp.zeros((SUBLANES - 3, LANES), F32)], axis=0).astype(I32)
        mbuf[...] = state
        mcopy = pltpu.make_async_copy(mbuf, msm, msem.at[0])
        mcopy.start()
        mcopy.wait()

        def zero_copy(first_row, n_rows):
            return pltpu.make_async_copy(
                zbuf.at[pl.ds(0, n_rows * sub), :],
                xs_hbm.at[pl.ds(pl.multiple_of(first_row * sub, sub), n_rows * sub), :], zsem.at[0])

        chunks = [rows >> (s + 1) for s in range(rows.bit_length() - 1)]
        plans = []
        for e in range(N_EXPERTS):
            rem = rows - msm[0, e]
            at = msm[1, e] * rows + msm[0, e]
            for c in chunks:
                take = (rem & c) != 0
                plans.append((take, zero_copy(at, c)))
                at = at + jnp.where(take, c, 0)
        for j in range(N_EXPERTS):
            blk = msm[2, 0] + j
            safe = jnp.minimum(blk, n_blocks - 1)
            plans.append((blk < n_blocks, zero_copy(safe * rows, rows)))
        for take, cp in plans:
            pl.when(take)(cp.start)
        for take, cp in plans:
            pl.when(take)(cp.wait)


def _outproj(yf, ym, gates, x2, bm, wfo, wmo, wo, gffn, wrh, wrl, br, lstrict, ustrict, *, n_blocks):
    n, d = x2.shape
    tm = ROW_TILE
    rows = EXPERT_ROWS
    sub = d // LANES
    const = lambda i: (0, 0)
    row = lambda i: (i, 0)
    full = lambda a: pl.BlockSpec(a.shape, const)
    consts = (bm, wfo, wmo, wo, gffn, wrh, wrl, br, lstrict, ustrict)
    return pl.pallas_call(
        _outproj_kernel,
        out_shape=[jax.ShapeDtypeStruct((n, d), F32),
                   jax.ShapeDtypeStruct((n, LANES), F32),
                   jax.ShapeDtypeStruct((n // tm, SUBLANES, tm), I32),
                   jax.ShapeDtypeStruct((2 * SUBLANES, LANES), I32),
                   jax.ShapeDtypeStruct((n_blocks * rows * sub, LANES), F32)],
        grid=(n // tm,),
        in_specs=[pl.BlockSpec((tm, yf.shape[1]), row), pl.BlockSpec((tm, ym.shape[1]), row),
                  pl.BlockSpec((tm, gates.shape[1]), row), pl.BlockSpec((tm, d), row)]
                 + [full(a) for a in consts],
        out_specs=[pl.BlockSpec((tm, d), row), pl.BlockSpec((tm, LANES), row),
                   pl.BlockSpec((1, SUBLANES, tm), lambda i: (i, 0, 0)),
                   pl.BlockSpec((2 * SUBLANES, LANES), const),
                   pl.BlockSpec(memory_space=pl.ANY)],
        scratch_shapes=[pltpu.VMEM((1, LANES), F32), pltpu.VMEM((1, LANES), F32),
                        pltpu.VMEM((1, LANES), F32), pltpu.VMEM((SUBLANES, LANES), F32),
                        pltpu.VMEM((2, tm * sub, LANES), F32),
                        pltpu.VMEM((2, SUBLANES, tm), I32),
                        pltpu.SMEM((2, SUBLANES, tm), I32),
                        pltpu.VMEM((rows * sub, LANES), F32),
                        pltpu.VMEM((SUBLANES, LANES), I32),
                        pltpu.SMEM((SUBLANES, LANES), I32),
                        pltpu.SemaphoreType.DMA((2,)), pltpu.SemaphoreType.DMA((2,)),
                        pltpu.SemaphoreType.DMA((1,)), pltpu.SemaphoreType.DMA((1,))],
        compiler_params=pltpu.CompilerParams(dimension_semantics=("arbitrary",),
                                             vmem_limit_bytes=VMEM_LIMIT),
        name="outproj_router",
    )(yf, ym, gates, x2, *consts)


def _expert_kernel(order_ref, be_ref, nused_ref,
                   xs_ref, wgu_ref, bgu_ref, wd_ref, bd_ref, ys_ref, wgu_b, wd_b):
    b = pl.program_id(0)
    rows = EXPERT_ROWS
    sub = xs_ref.shape[0] // rows

    @pl.when(b < nused_ref[0])
    def _():
        changed = jnp.logical_or(b == 0, be_ref[b] != be_ref[jnp.maximum(b - 1, 0)])

        @pl.when(changed)
        def _():
            wgu_b[...] = wgu_ref[0].astype(BF16)
            wd_b[...] = wd_ref[0].astype(BF16)

        x = jnp.concatenate(
            [xs_ref[pl.ds(s, rows, stride=sub), :].astype(BF16) for s in range(sub)], axis=1)
        gu = _dot(x, wgu_b[...]) + bgu_ref[0]
        gate = jnp.minimum(gu[:, :EXPERT_FF], SWIGLU_LIMIT)
        up = jnp.clip(gu[:, EXPERT_FF:], -SWIGLU_LIMIT, SWIGLU_LIMIT)
        glu = gate * (1.0 / (1.0 + jnp.exp(-SWIGLU_ALPHA * gate)))
        y = _dot(((up + 1.0) * glu).astype(BF16), wd_b[...]) + bd_ref[0]
        for s in range(sub):
            ys_ref[pl.ds(s, rows, stride=sub), :] = y[:, s * LANES:(s + 1) * LANES]

    @pl.when(b >= nused_ref[0])
    def _():
        ys_ref[...] = jnp.zeros_like(ys_ref)


def _experts(order, block_e, nused, xs, w_gu, b_gu, w_d, b_d):
    e, d, ff2 = w_gu.shape
    rows = EXPERT_ROWS
    sub = d // LANES
    n_blocks = xs.shape[0] // (rows * sub)
    wmap = lambda b, od, be, nu: (be[b], 0, 0)
    xmap = lambda b, od, be, nu: (od[b], 0)
    grid_spec = pltpu.PrefetchScalarGridSpec(
        num_scalar_prefetch=3,
        grid=(n_blocks,),
        in_specs=[pl.BlockSpec((rows * sub, LANES), xmap),
                  pl.BlockSpec((1, d, ff2), wmap), pl.BlockSpec((1, 1, ff2), wmap),
                  pl.BlockSpec((1, ff2 // 2, d), wmap), pl.BlockSpec((1, 1, d), wmap)],
        out_specs=pl.BlockSpec((rows * sub, LANES), xmap),
        scratch_shapes=[pltpu.VMEM((d, ff2), BF16), pltpu.VMEM((ff2 // 2, d), BF16)],
    )
    return pl.pallas_call(
        _expert_kernel,
        out_shape=jax.ShapeDtypeStruct(xs.shape, F32),
        grid_spec=grid_spec,
        compiler_params=pltpu.CompilerParams(dimension_semantics=("arbitrary",),
                                             vmem_limit_bytes=VMEM_LIMIT),
        name="experts",
    )(order, block_e, nused, xs, w_gu, b_gu.reshape(e, 1, ff2), w_d, b_d.reshape(e, 1, d))


def _combine_kernel(x1_ref, route_ref, dest_hbm, ys_hbm, g_ref, o_ref, ybuf, dsm, gsem, isem):
    i = pl.program_id(0)
    nt = pl.num_programs(0)
    tm, d = x1_ref.shape
    sub = d // LANES
    slot = i % 2

    def idx_copy(tile):
        s = tile % 2
        return pltpu.make_async_copy(dest_hbm.at[tile], dsm.at[s], isem.at[s])

    def start_gathers(tile):
        s = tile % 2

        def body(t, _):
            for k in range(TOP_K):
                src = dsm[s, k, t]
                pltpu.make_async_copy(
                    ys_hbm.at[pl.ds(pl.multiple_of(src * sub, sub), sub), :],
                    ybuf.at[s, pl.ds(pl.multiple_of((k * tm + t) * sub, sub), sub), :],
                    gsem.at[s]).start()
            return 0

        lax.fori_loop(0, tm, body, 0, unroll=4)

    def wait_gathers(s):
        pltpu.make_async_copy(ys_hbm.at[pl.ds(0, TOP_K * tm * sub), :], ybuf.at[s], gsem.at[s]).wait()

    @pl.when(i == 0)
    def _():
        idx_copy(0).start()
        idx_copy(0).wait()
        start_gathers(0)

        @pl.when(nt > 1)
        def _():
            idx_copy(1).start()

    @pl.when(i + 1 < nt)
    def _():
        idx_copy(i + 1).wait()
        start_gathers(i + 1)

    @pl.when(i + 2 < nt)
    def _():
        idx_copy(i + 2).start()

    wait_gathers(slot)
    route = route_ref[...]
    ws = [route[:, ROUTE_W + k:ROUTE_W + k + 1] for k in range(TOP_K)]
    chunks = []
    ssq = jnp.zeros((tm, 1), F32)
    for s in range(sub):
        acc = x1_ref[:, s * LANES:(s + 1) * LANES]
        for k in range(TOP_K):
            acc = acc + ws[k] * ybuf[slot, pl.ds(k * tm * sub + s, tm, stride=sub), :]
        chunks.append(acc)
        ssq = ssq + jnp.sum(acc * acc, axis=-1, keepdims=True)
    inv = lax.rsqrt(ssq / d + NORM_EPS)
    for s in range(sub):
        sl = slice(s * LANES, (s + 1) * LANES)
        o_ref[:, sl] = chunks[s] * inv * g_ref[:, sl]


def _combine(x1, route, dest_t, ys, g_final):
    n, d = x1.shape
    tm = ROW_TILE
    sub = d // LANES
    row = lambda i: (i, 0)
    return pl.pallas_call(
        _combine_kernel,
        out_shape=jax.ShapeDtypeStruct((n, d), F32),
        grid=(n // tm,),
        in_specs=[pl.BlockSpec((tm, d), row), pl.BlockSpec((tm, LANES), row),
                  pl.BlockSpec(memory_space=pl.ANY), pl.BlockSpec(memory_space=pl.ANY),
                  pl.BlockSpec((1, d), lambda i: (0, 0))],
        out_specs=pl.BlockSpec((tm, d), row),
        scratch_shapes=[pltpu.VMEM((2, TOP_K * tm * sub, LANES), F32),
                        pltpu.SMEM((2, SUBLANES, tm), I32),
                        pltpu.SemaphoreType.DMA((2,)), pltpu.SemaphoreType.DMA((2,))],
        compiler_params=pltpu.CompilerParams(dimension_semantics=("arbitrary",),
                                             vmem_limit_bytes=VMEM_LIMIT),
        name="combine_norm",
    )(x1, route, dest_t, ys, g_final)


def _pad_heads(w, per_head, width=LANES):
    k = w.shape[0]
    w = w.reshape(k, HEADS, per_head)
    w = jnp.pad(w, ((0, 0), (0, 0), (0, width - per_head)))
    return w.reshape(k, HEADS * width)


def _aug_constants():
    hw = HEADS * LANES
    place = np.zeros((LANES, 2 * hw), np.float32)
    ones = np.zeros((1, 2 * hw), np.float32)
    for hd in range(HEADS):
        for piece in range(3):
            src = FG_LO + piece * HEADS + hd
            place[src, hd * LANES + AUG_LO + piece] = 1.0
            place[src, hw + hd * LANES + AUG_LO + 3 + piece] = -1.0
            ones[0, hd * LANES + AUG_LO + 3 + piece] = 1.0
            ones[0, hw + hd * LANES + AUG_LO + piece] = 1.0
    return jnp.asarray(place, BF16), jnp.asarray(ones, F32)


def _layer(x2, pos2, batch, seq_len, g_attn_norm, w_in, b_fgate, g_q_a, w_q_b, g_kv_a, w_kv_b,
           w_fox_out, w_mla_out, b_merge, w_o, g_ffn_norm, w_router, b_router, w_gu, b_gu,
           w_down, b_down, g_out):
    n, d = x2.shape
    fw = HEADS * HEAD_DIM
    o = 0
    w_qf = w_in[:, o:o + fw]; o += fw
    w_kf = w_in[:, o:o + fw]; o += fw
    w_vf = w_in[:, o:o + fw]; o += fw
    w_f = w_in[:, o:o + HEADS]; o += HEADS
    w_ql = w_in[:, o:o + Q_RANK]; o += Q_RANK
    w_ckv = w_in[:, o:o + KV_RANK]; o += KV_RANK
    w_kpe = w_in[:, o:o + ROPE_DIM]; o += ROPE_DIM
    w_gate = w_in[:, o:]

    wq = _pad_heads(w_qf * (HEAD_DIM ** -0.5), HEAD_DIM).astype(BF16)
    wk = _pad_heads(w_kf, HEAD_DIM).astype(BF16)
    wmisc = jnp.concatenate([jnp.zeros((d, PE_LO), F32), w_kpe, w_f,
                             jnp.zeros((d, LANES - FG_LO - HEADS), F32)], axis=1)
    wlat = jnp.concatenate([w_ql, w_ckv, wmisc], axis=1).astype(BF16)
    bf128 = jnp.zeros((1, LANES), F32).at[0, FG_LO:FG_LO + HEADS].set(b_fgate)
    wqb = _pad_heads(w_q_b, HEAD_DIM + ROPE_DIM).astype(BF16)
    wkv = w_kv_b.reshape(KV_RANK, HEADS, 2 * HEAD_DIM)
    wkbk = _pad_heads(wkv[:, :, :HEAD_DIM].reshape(KV_RANK, fw), HEAD_DIM).astype(BF16)
    wkbv = wkv[:, :, HEAD_DIM:].reshape(KV_RANK, fw).T.astype(BF16)
    tm = ROW_TILE
    tri = np.tril(np.ones((tm, tm), np.float32))
    ltri = jnp.asarray(tri, BF16)
    lstrict = jnp.asarray(tri - np.eye(tm, dtype=np.float32), BF16)
    ustrict = jnp.asarray(np.triu(np.ones((LANES, LANES), np.float32), 1), BF16)
    place, ones = _aug_constants()
    half = ROPE_DIM // 2
    inv_freq = ROPE_THETA ** (-jnp.arange(half, dtype=F32) / half)
    freq = jnp.zeros((1, LANES), F32).at[0, PE_LO:PE_MID].set(inv_freq).at[0, PE_MID:PE_HI].set(inv_freq)

    qf, kf, vf, qm, km, vm, gates = _inproj(
        x2, pos2, g_attn_norm.reshape(1, d), wq, wk, w_vf.T.astype(BF16), wlat, w_gate.astype(BF16),
        bf128, g_q_a.reshape(1, -1), wqb, g_kv_a.reshape(1, -1), wkbk, wkbv, ltri, place, ones,
        freq, seq_len=seq_len)

    y_fox = _attention(qf, kf, vf, batch=batch, seq_len=seq_len, chunk_mask=False)
    y_mla = _attention(qm, km, vm, batch=batch, seq_len=seq_len, chunk_mask=True)

    wr = jnp.pad(w_router, ((0, 0), (0, LANES - N_EXPERTS)))
    wrh = wr.astype(BF16)
    wrl = (wr - wrh.astype(F32)).astype(BF16)
    br = jnp.full((1, LANES), NEG, F32).at[0, :N_EXPERTS].set(b_router)
    n_blocks = n * TOP_K // EXPERT_ROWS + N_EXPERTS
    x1, route, dest_t, meta, xs = _outproj(
        y_fox, y_mla, gates, x2, b_merge.reshape(1, -1), w_fox_out.astype(BF16),
        w_mla_out.astype(BF16), w_o.astype(BF16), g_ffn_norm.reshape(1, d), wrh, wrl, br, lstrict,
        ustrict, n_blocks=n_blocks)

    block_e = meta[:SUBLANES].reshape(-1)[:n_blocks]
    order = jnp.argsort(block_e, stable=True).astype(I32)
    nused = meta[SUBLANES, 0:1]
    be_sorted = jnp.minimum(block_e[order], N_EXPERTS - 1).astype(I32)
    be_sorted = jnp.where(jnp.arange(n_blocks) < nused[0], be_sorted,
                          be_sorted[jnp.maximum(nused[0] - 1, 0)])
    ys = _experts(order, be_sorted, nused, xs, w_gu, b_gu, w_down, b_down)
    return _combine(x1, route, dest_t, ys, g_out.reshape(1, d))


def kernel(x, positions, g_attn_norm, w_in, b_fgate, g_q_a, w_q_b, g_kv_a, w_kv_b, w_fox_out, w_mla_out, b_merge, w_o, g_ffn_norm, w_router, b_router, w_gu, b_gu, w_down, b_down, g_final):
    batch, seq_len, d = x.shape
    depth = w_in.shape[0]
    assert depth == 1, "the fused combine + final-norm kernel assumes a single layer"
    assert seq_len % ATT_TILE == 0 and d % LANES == 0
    assert (batch * seq_len * TOP_K) % EXPERT_ROWS == 0
    assert batch * seq_len * TOP_K // EXPERT_ROWS + N_EXPERTS <= SUBLANES * LANES
    x2 = x.reshape(batch * seq_len, d)
    pos2 = positions.reshape(batch * seq_len, 1).astype(I32)
    out = _layer(x2, pos2, batch, seq_len, g_attn_norm[0], w_in[0], b_fgate[0], g_q_a[0], w_q_b[0],
                 g_kv_a[0], w_kv_b[0], w_fox_out[0], w_mla_out[0], b_merge[0], w_o[0],
                 g_ffn_norm[0], w_router[0], b_router[0], w_gu[0], b_gu[0], w_down[0], b_down[0],
                 g_final)
    return out.reshape(batch, seq_len, d)
```

```python
import functools

import jax
import jax.numpy as jnp
import numpy as np
from jax import lax
from jax.experimental import pallas as pl
from jax.experimental.pallas import tpu as pltpu

F32 = jnp.float32
BF16 = jnp.bfloat16
I32 = jnp.int32

LANES = 128
SUBLANES = 8
VMEM_LIMIT = 56 * 1024 * 1024

NORM_EPS = 1e-6
HEADS = 8
HEAD_DIM = 64
ROPE_DIM = 32
Q_RANK = 256
KV_RANK = 128
N_EXPERTS = 32
TOP_K = 4
EXPERT_FF = 1024
SWIGLU_LIMIT = 7.0
SWIGLU_ALPHA = 1.702
ROPE_THETA = 10000.0
CHUNK = 64

NEG = -1e30
LOG2E = 1.4426950408889634

ROW_TILE = 256
ATT_TILE = 512
ATT_HEADS = 2
EXPERT_ROWS = ROW_TILE

PE_LO, PE_MID, PE_HI = 64, 80, 96
FG_LO = 96
AUG_LO = 64
ROUTE_DEST = 0
ROUTE_W = 8


def _dot(a, b):
    return jnp.dot(a, b, preferred_element_type=F32)


def _dot_nt(a, b):
    return lax.dot_general(a, b, (((1,), (1,)), ((), ())), preferred_element_type=F32)


def _split3(a):
    hi = a.astype(BF16)
    r1 = a - hi.astype(F32)
    mid = r1.astype(BF16)
    lo = (r1 - mid.astype(F32)).astype(BF16)
    return hi, mid, lo


def _rms(x, g):
    return x * lax.rsqrt(jnp.mean(x * x, axis=-1, keepdims=True) + NORM_EPS) * g


def _inproj_kernel(x_ref, pos_ref, g_ref, wq_ref, wk_ref, wv_ref, wlat_ref, wg_ref, bf_ref,
                   gq_ref, wqb_ref, gkv_ref, wkbk_ref, wkbv_ref, ltri_ref, place_ref, ones_ref,
                   freq_ref,
                   qf_ref, kf_ref, vf_ref, qm_ref, km_ref, vm_ref, gate_ref,
                   carry_ref, *, tiles_per_seq, mla_scale):
    i = pl.program_id(0)

    @pl.when(i % tiles_per_seq == 0)
    def _():
        carry_ref[...] = jnp.zeros_like(carry_ref)

    h = _rms(x_ref[...], g_ref[...]).astype(BF16)
    tm = h.shape[0]
    lane = lax.broadcasted_iota(I32, (tm, LANES), 1)

    gate_ref[...] = _dot(h, wg_ref[...]).astype(BF16)
    vf_ref[0] = _dot_nt(wv_ref[...], h).astype(BF16)

    lat = _dot(h, wlat_ref[...])
    q_lat = lat[:, :Q_RANK]
    c_kv = lat[:, Q_RANK:Q_RANK + KV_RANK]
    misc = lat[:, Q_RANK + KV_RANK:]

    z = misc + bf_ref[...]
    logf = jnp.minimum(z, 0.0) - jnp.log1p(jnp.exp(-jnp.abs(z)))
    fmask = (lane >= FG_LO) & (lane < FG_LO + HEADS)
    logf = jnp.where(fmask, logf, 0.0)
    l_hi, l_mid, l_lo = _split3(logf)
    ltri = ltri_ref[...]
    c = _dot(ltri, l_hi) + _dot(ltri, l_mid) + _dot(ltri, l_lo) + carry_ref[...]
    carry_ref[...] = c[tm - 1:tm, :]
    c_hi, c_mid, c_lo = _split3(c * LOG2E)
    c3 = (c_hi.astype(F32) + pltpu.roll(c_mid.astype(F32), HEADS, 1)
          + pltpu.roll(c_lo.astype(F32), 2 * HEADS, 1)).astype(BF16)
    aug = _dot(c3, place_ref[...]) + ones_ref[...]
    hw = HEADS * LANES
    qf_ref[...] = (_dot(h, wq_ref[...]) * LOG2E + aug[:, :hw]).astype(BF16)
    kf_ref[...] = (_dot(h, wk_ref[...]) + aug[:, hw:]).astype(BF16)

    qn = _rms(q_lat, gq_ref[...]).astype(BF16)
    kvn = _rms(c_kv, gkv_ref[...]).astype(BF16)
    qfull = _dot(qn, wqb_ref[...])
    knope = _dot(kvn, wkbk_ref[...])
    vm_ref[0] = _dot_nt(wkbv_ref[...], kvn).astype(BF16)

    ang = pos_ref[...].astype(F32) * freq_ref[...]
    cosv = jnp.cos(ang)
    sinv = jnp.sin(ang)
    s1 = jnp.where((lane >= PE_LO) & (lane < PE_MID), -sinv, 0.0)
    s2 = jnp.where((lane >= PE_MID) & (lane < PE_HI), sinv, 0.0)

    def rope(v):
        return v * cosv + pltpu.roll(v, LANES - 16, 1) * s1 + pltpu.roll(v, 16, 1) * s2

    kpe = jnp.where((lane >= PE_LO) & (lane < PE_HI), rope(misc), 0.0)
    for hd in range(HEADS):
        sl = slice(hd * LANES, (hd + 1) * LANES)
        qm_ref[:, sl] = (rope(qfull[:, sl]) * mla_scale).astype(BF16)
        km_ref[:, sl] = (knope[:, sl] + kpe).astype(BF16)


def _inproj(x2, pos2, g_attn, wq, wk, wv, wlat, wg, bf128, gq, wqb, gkv, wkbk, wkbv, ltri,
            place, ones, freq, *, seq_len):
    n, d = x2.shape
    tm = ROW_TILE
    hw = HEADS * LANES
    vw = HEADS * HEAD_DIM
    const = lambda i: (0, 0)
    row = lambda i: (i, 0)
    full = lambda a: pl.BlockSpec(a.shape, const)
    rows_out = lambda w: (jax.ShapeDtypeStruct((n, w), BF16), pl.BlockSpec((tm, w), row))
    vt_out = (jax.ShapeDtypeStruct((n // tm, vw, tm), BF16),
              pl.BlockSpec((1, vw, tm), lambda i: (i, 0, 0)))
    outs = [rows_out(hw), rows_out(hw), vt_out, rows_out(hw), rows_out(hw), vt_out,
            rows_out(wg.shape[1])]
    consts = (g_attn, wq, wk, wv, wlat, wg, bf128, gq, wqb, gkv, wkbk, wkbv, ltri, place, ones, freq)
    return pl.pallas_call(
        functools.partial(_inproj_kernel, tiles_per_seq=seq_len // tm,
                          mla_scale=float((HEAD_DIM + ROPE_DIM) ** -0.5) * LOG2E),
        out_shape=[o[0] for o in outs],
        grid=(n // tm,),
        in_specs=[pl.BlockSpec((tm, d), row), pl.BlockSpec((tm, 1), row)] + [full(a) for a in consts],
        out_specs=[o[1] for o in outs],
        scratch_shapes=[pltpu.VMEM((1, LANES), F32)],
        compiler_params=pltpu.CompilerParams(dimension_semantics=("arbitrary",),
                                             vmem_limit_bytes=VMEM_LIMIT),
        name="inproj",
    )(x2, pos2, *consts)


def _attn_kernel(q_ref, k_ref, vt_ref, o_ref, *, chunk_mask):
    i = pl.program_id(2)
    tq = q_ref.shape[0]
    tk = vt_ref.shape[2]
    key = lax.broadcasted_iota(I32, (tk, tq), 0)
    qry = lax.broadcasted_iota(I32, (tk, tq), 1) + i * tq

    def scores(j, masked):
        start = pl.multiple_of(j * tk, tk)
        ss = [_dot_nt(k_ref[pl.ds(start, tk), hd * LANES:(hd + 1) * LANES],
                      q_ref[:, hd * LANES:(hd + 1) * LANES])
              for hd in range(ATT_HEADS)]
        if masked:
            keyg = key + j * tk
            allowed = (keyg // CHUNK) <= (qry // CHUNK) if chunk_mask else keyg <= qry
            ss = [jnp.where(allowed, s, NEG) for s in ss]
        return ss

    ones = jnp.ones((2 * SUBLANES, tk), BF16)

    def update(j, ss, state):
        vt = vt_ref[j]
        new = []
        for hd, s in enumerate(ss):
            m, acc = state[hd]
            m_new = jnp.maximum(m, jnp.max(s, axis=0, keepdims=True))
            alpha = jnp.exp2(m - m_new)
            p = jnp.exp2((s - m_new).astype(BF16))
            va = jnp.concatenate([vt[hd * HEAD_DIM:(hd + 1) * HEAD_DIM, :], ones], axis=0)
            acc = alpha * acc + _dot(va, p)
            new.append((m_new, acc))
        return tuple(new)

    n_full = (i * tq) // tk
    group = max(1, tq // tk)

    def run_group(first, state, masked):
        ss = scores(first, masked)
        for u in range(group):
            nxt = scores(first + u + 1, masked) if u + 1 < group else None
            state = update(first + u, ss, state)
            ss = nxt
        return state

    init1 = (jnp.full((1, tq), NEG, F32), jnp.zeros((HEAD_DIM + 2 * SUBLANES, tq), F32))
    state = lax.fori_loop(0, n_full // group, lambda jj, c: run_group(jj * group, c, False),
                          (init1,) * ATT_HEADS)
    state = run_group(n_full, state, True)
    out_t = jnp.concatenate([acc[:HEAD_DIM] / acc[HEAD_DIM:HEAD_DIM + 1] for _, acc in state],
                            axis=0)
    o_ref[...] = out_t.T.astype(o_ref.dtype)


def _attention(q, k, vt, *, batch, seq_len, chunk_mask):
    n = q.shape[0]
    t = ATT_TILE
    tk = vt.shape[2]
    nq = seq_len // t
    nkb = seq_len // tk
    hp = ATT_HEADS
    pairs = HEADS // hp
    return pl.pallas_call(
        functools.partial(_attn_kernel, chunk_mask=chunk_mask),
        out_shape=jax.ShapeDtypeStruct((n, HEADS * HEAD_DIM), BF16),
        grid=(batch, pairs, nq),
        in_specs=[pl.BlockSpec((t, hp * LANES), lambda b, p, i: (b * nq + i, p)),
                  pl.BlockSpec((seq_len, hp * LANES), lambda b, p, i: (b, p)),
                  pl.BlockSpec((nkb, hp * HEAD_DIM, tk), lambda b, p, i: (b, p, 0))],
        out_specs=pl.BlockSpec((t, hp * HEAD_DIM), lambda b, p, i: (b * nq + i, p)),
        compiler_params=pltpu.CompilerParams(
            dimension_semantics=("arbitrary", "arbitrary", "arbitrary"),
            vmem_limit_bytes=VMEM_LIMIT),
        name="attn_mla" if chunk_mask else "attn_fox",
    )(q, k, vt)


def _outproj_kernel(yf_ref, ym_ref, gate_ref, x_ref, bm_ref, wfo_ref, wmo_ref, wo_ref, gffn_ref,
                    wrh_ref, wrl_ref, br_ref, lstrict_ref, ustrict_ref,
                    x1_ref, route_ref, dest_ref, meta_ref, xs_hbm,
                    fill_ref, cur_ref, nfree_ref, tbl_ref,
                    hbuf, dbuf, dsm, zbuf, mbuf, msm, ssem, isem, zsem, msem):
    i = pl.program_id(0)
    nt = pl.num_programs(0)
    d = x_ref.shape[1]
    tm = x_ref.shape[0]
    sub = d // LANES
    rows = EXPERT_ROWS
    n_blocks = xs_hbm.shape[0] // (rows * sub)
    slot = i % 2

    def idx_copy(s):
        return pltpu.make_async_copy(dbuf.at[s], dsm.at[s], isem.at[s])

    def start_scatters(s):
        for static_s in range(2):
            @pl.when(s == static_s)
            def _(static_s=static_s):
                def body(t, _):
                    src = hbuf.at[static_s, pl.ds(pl.multiple_of(t * sub, sub), sub), :]
                    for k in range(TOP_K):
                        dst = dsm[static_s, k, t]
                        pltpu.make_async_copy(
                            src, xs_hbm.at[pl.ds(pl.multiple_of(dst, sub), sub), :],
                            ssem.at[static_s]).start()
                    return 0

                lax.fori_loop(0, tm, body, 0, unroll=4)

    def wait_scatters(s):
        for _ in range(TOP_K):
            pltpu.make_async_copy(hbuf.at[s], xs_hbm.at[pl.ds(0, tm * sub), :], ssem.at[s]).wait()

    @pl.when(i == 0)
    def _():
        fill_ref[...] = jnp.full_like(fill_ref, float(rows))
        cur_ref[...] = jnp.zeros_like(cur_ref)
        nfree_ref[...] = jnp.zeros_like(nfree_ref)
        tbl_ref[...] = jnp.full_like(tbl_ref, float(N_EXPERTS))
        zbuf[...] = jnp.zeros_like(zbuf)

    @pl.when(i > 0)
    def _():
        idx_copy(1 - slot).wait()
        start_scatters(1 - slot)

    a = _dot(yf_ref[...], wfo_ref[...])
    b = _dot(ym_ref[...], wmo_ref[...])
    g = 1.0 / (1.0 + jnp.exp(-(gate_ref[...].astype(F32) + bm_ref[...])))
    merged = (g[:, :d] * a + g[:, d:] * b).astype(BF16)
    x1 = x_ref[...] + _dot(merged, wo_ref[...])
    x1_ref[...] = x1
    h2 = _rms(x1, gffn_ref[...])

    hi = h2.astype(BF16)
    lo = (h2 - hi.astype(F32)).astype(BF16)
    wrh = wrh_ref[...]
    logits = _dot(hi, wrh) + _dot(lo, wrh) + _dot(hi, wrl_ref[...]) + br_ref[...]

    lane = lax.broadcasted_iota(I32, (tm, LANES), 1)
    vals = logits
    sels, tops = [], []
    for _ in range(TOP_K):
        mx = jnp.max(vals, axis=-1, keepdims=True)
        idx = jnp.min(jnp.where(vals == mx, lane, LANES), axis=-1, keepdims=True)
        sel = lane == idx
        vals = jnp.where(sel, NEG, vals)
        sels.append(sel)
        tops.append(mx)
    es = [jnp.exp(tv - tops[0]) for tv in tops]
    den = es[0] + es[1] + es[2] + es[3]

    onehot = jnp.zeros((tm, LANES), F32)
    for sel in sels:
        onehot = onehot + sel.astype(F32)
    before = _dot(lstrict_ref[...], onehot.astype(BF16))
    cnt = jnp.sum(onehot, axis=0, keepdims=True)

    fill = fill_ref[...]
    cur = cur_ref[...]
    nfree = nfree_ref[...]
    need = ((fill + cnt) > float(rows)).astype(F32)
    need8 = jnp.broadcast_to(need, (SUBLANES, LANES)).astype(BF16)
    newid = nfree + _dot(need8, ustrict_ref[...])[0:1, :]
    pos = fill + before
    dest = jnp.where(pos < float(rows), cur * rows + pos, newid * rows + pos - float(rows))
    fill_ref[...] = fill + cnt - need * float(rows)
    cur_ref[...] = jnp.where(need > 0, newid, cur)
    nfree_ref[...] = nfree + jnp.sum(need, axis=-1, keepdims=True)
    blk_id = (lax.broadcasted_iota(I32, (SUBLANES, LANES), 0) * LANES
              + lax.broadcasted_iota(I32, (SUBLANES, LANES), 1)).astype(F32)
    tbl = tbl_ref[...]
    for e in range(N_EXPERTS):
        hit = (blk_id == newid[:, e:e + 1]) & (need[:, e:e + 1] > 0)
        tbl = jnp.where(hit, float(e), tbl)
    tbl_ref[...] = tbl

    route = jnp.zeros((tm, LANES), F32)
    for k in range(TOP_K):
        dest_k = jnp.sum(jnp.where(sels[k], dest, 0.0), axis=-1, keepdims=True)
        route = jnp.where(lane == ROUTE_DEST + k, dest_k, route)
        route = jnp.where(lane == ROUTE_W + k, es[k] / den, route)
    route_ref[...] = route
    dest_t = (route.T[0:SUBLANES, :] * float(sub)).astype(I32)
    dest_ref[0] = dest_t

    @pl.when(i >= 2)
    def _():
        wait_scatters(slot)

    for s in range(sub):
        hbuf[slot, pl.ds(s, tm, stride=sub), :] = h2[:, s * LANES:(s + 1) * LANES]
    dbuf[slot] = dest_t
    idx_copy(slot).start()

    @pl.when(i == nt - 1)
    def _():
        idx_copy(slot).wait()
        start_scatters(slot)

        @pl.when(i >= 1)
        def _():
            wait_scatters(1 - slot)

        wait_scatters(slot)

        meta = jnp.concatenate([tbl_ref[...], jnp.broadcast_to(nfree_ref[...], (SUBLANES, LANES))],
                               axis=0).astype(I32)
        meta_ref[...] = meta
        state = jnp.concatenate([fill_ref[...], cur_ref[...], nfree_ref[...],
                                 jnp.zeros((SUBLANES - 3, LANES), F32)], axis=0).astype(I32)
        mbuf[...] = state
        mcopy = pltpu.make_async_copy(mbuf, msm, msem.at[0])
        mcopy.start()
        mcopy.wait()

        def zero_copy(first_row, n_rows):
            return pltpu.make_async_copy(
                zbuf.at[pl.ds(0, n_rows * sub), :],
                xs_hbm.at[pl.ds(pl.multiple_of(first_row * sub, sub), n_rows * sub), :], zsem.at[0])

        chunks = [rows >> (s + 1) for s in range(rows.bit_length() - 1)]
        plans = []
        for e in range(N_EXPERTS):
            rem = rows - msm[0, e]
            at = msm[1, e] * rows + msm[0, e]
            for c in chunks:
                take = (rem & c) != 0
                plans.append((take, zero_copy(at, c)))
                at = at + jnp.where(take, c, 0)
        for j in range(N_EXPERTS):
            blk = msm[2, 0] + j
            safe = jnp.minimum(blk, n_blocks - 1)
            plans.append((blk < n_blocks, zero_copy(safe * rows, rows)))
        for take, cp in plans:
            pl.when(take)(cp.start)
        for take, cp in plans:
            pl.when(take)(cp.wait)


def _outproj(yf, ym, gates, x2, bm, wfo, wmo, wo, gffn, wrh, wrl, br, lstrict, ustrict, *, n_blocks):
    n, d = x2.shape
    tm = ROW_TILE
    rows = EXPERT_ROWS
    sub = d // LANES
    const = lambda i: (0, 0)
    row = lambda i: (i, 0)
    full = lambda a: pl.BlockSpec(a.shape, const)
    consts = (bm, wfo, wmo, wo, gffn, wrh, wrl, br, lstrict, ustrict)
    return pl.pallas_call(
        _outproj_kernel,
        out_shape=[jax.ShapeDtypeStruct((n, d), F32),
                   jax.ShapeDtypeStruct((n, LANES), F32),
                   jax.ShapeDtypeStruct((n // tm, SUBLANES, tm), I32),
                   jax.ShapeDtypeStruct((2 * SUBLANES, LANES), I32),
                   jax.ShapeDtypeStruct((n_blocks * rows * sub, LANES), F32)],
        grid=(n // tm,),
        in_specs=[pl.BlockSpec((tm, yf.shape[1]), row), pl.BlockSpec((tm, ym.shape[1]), row),
                  pl.BlockSpec((tm, gates.shape[1]), row), pl.BlockSpec((tm, d), row)]
                 + [full(a) for a in consts],
        out_specs=[pl.BlockSpec((tm, d), row), pl.BlockSpec((tm, LANES), row),
                   pl.BlockSpec((1, SUBLANES, tm), lambda i: (i, 0, 0)),
                   pl.BlockSpec((2 * SUBLANES, LANES), const),
                   pl.BlockSpec(memory_space=pl.ANY)],
        scratch_shapes=[pltpu.VMEM((1, LANES), F32), pltpu.VMEM((1, LANES), F32),
                        pltpu.VMEM((1, LANES), F32), pltpu.VMEM((SUBLANES, LANES), F32),
                        pltpu.VMEM((2, tm * sub, LANES), F32),
                        pltpu.VMEM((2, SUBLANES, tm), I32),
                        pltpu.SMEM((2, SUBLANES, tm), I32),
                        pltpu.VMEM((rows * sub, LANES), F32),
                        pltpu.VMEM((SUBLANES, LANES), I32),
                        pltpu.SMEM((SUBLANES, LANES), I32),
                        pltpu.SemaphoreType.DMA((2,)), pltpu.SemaphoreType.DMA((2,)),
                        pltpu.SemaphoreType.DMA((1,)), pltpu.SemaphoreType.DMA((1,))],
        compiler_params=pltpu.CompilerParams(dimension_semantics=("arbitrary",),
                                             vmem_limit_bytes=VMEM_LIMIT),
        name="outproj_router",
    )(yf, ym, gates, x2, *consts)


def _expert_kernel(order_ref, be_ref, nused_ref,
                   xs_ref, wgu_ref, bgu_ref, wd_ref, bd_ref, ys_ref, wgu_b, wd_b):
    b = pl.program_id(0)
    rows = EXPERT_ROWS
    sub = xs_ref.shape[0] // rows

    @pl.when(b < nused_ref[0])
    def _():
        changed = jnp.logical_or(b == 0, be_ref[b] != be_ref[jnp.maximum(b - 1, 0)])

        @pl.when(changed)
        def _():
            wgu_b[...] = wgu_ref[0].astype(BF16)
            wd_b[...] = wd_ref[0].astype(BF16)

        x = jnp.concatenate(
            [xs_ref[pl.ds(s, rows, stride=sub), :].astype(BF16) for s in range(sub)], axis=1)
        gu = _dot(x, wgu_b[...]) + bgu_ref[0]
        gate = jnp.minimum(gu[:, :EXPERT_FF], SWIGLU_LIMIT)
        up = jnp.clip(gu[:, EXPERT_FF:], -SWIGLU_LIMIT, SWIGLU_LIMIT)
        glu = gate * (1.0 / (1.0 + jnp.exp(-SWIGLU_ALPHA * gate)))
        y = _dot(((up + 1.0) * glu).astype(BF16), wd_b[...]) + bd_ref[0]
        for s in range(sub):
            ys_ref[pl.ds(s, rows, stride=sub), :] = y[:, s * LANES:(s + 1) * LANES]

    @pl.when(b >= nused_ref[0])
    def _():
        ys_ref[...] = jnp.zeros_like(ys_ref)


def _experts(order, block_e, nused, xs, w_gu, b_gu, w_d, b_d):
    e, d, ff2 = w_gu.shape
    rows = EXPERT_ROWS
    sub = d // LANES
    n_blocks = xs.shape[0] // (rows * sub)
    wmap = lambda b, od, be, nu: (be[b], 0, 0)
    xmap = lambda b, od, be, nu: (od[b], 0)
    grid_spec = pltpu.PrefetchScalarGridSpec(
        num_scalar_prefetch=3,
        grid=(n_blocks,),
        in_specs=[pl.BlockSpec((rows * sub, LANES), xmap),
                  pl.BlockSpec((1, d, ff2), wmap), pl.BlockSpec((1, 1, ff2), wmap),
                  pl.BlockSpec((1, ff2 // 2, d), wmap), pl.BlockSpec((1, 1, d), wmap)],
        out_specs=pl.BlockSpec((rows * sub, LANES), xmap),
        scratch_shapes=[pltpu.VMEM((d, ff2), BF16), pltpu.VMEM((ff2 // 2, d), BF16)],
    )
    return pl.pallas_call(
        _expert_kernel,
        out_shape=jax.ShapeDtypeStruct(xs.shape, F32),
        grid_spec=grid_spec,
        compiler_params=pltpu.CompilerParams(dimension_semantics=("arbitrary",),
                                             vmem_limit_bytes=VMEM_LIMIT),
        name="experts",
    )(order, block_e, nused, xs, w_gu, b_gu.reshape(e, 1, ff2), w_d, b_d.reshape(e, 1, d))


def _combine_kernel(x1_ref, route_ref, dest_hbm, ys_hbm, g_ref, o_ref, ybuf, dsm, gsem, isem):
    i = pl.program_id(0)
    nt = pl.num_programs(0)
    tm, d = x1_ref.shape
    sub = d // LANES
    slot = i % 2

    def idx_copy(tile):
        s = tile % 2
        return pltpu.make_async_copy(dest_hbm.at[tile], dsm.at[s], isem.at[s])

    def start_gathers(tile):
        for static_s in range(2):
            @pl.when(tile % 2 == static_s)
            def _(static_s=static_s):
                def body(t, _):
                    for k in range(TOP_K):
                        src = dsm[static_s, k, t]
                        pltpu.make_async_copy(
                            ys_hbm.at[pl.ds(pl.multiple_of(src, sub), sub), :],
                            ybuf.at[static_s, pl.ds(pl.multiple_of((k * tm + t) * sub, sub), sub), :],
                            gsem.at[static_s]).start()
                    return 0

                lax.fori_loop(0, tm, body, 0, unroll=4)

    def wait_gathers(s):
        pltpu.make_async_copy(ys_hbm.at[pl.ds(0, TOP_K * tm * sub), :], ybuf.at[s], gsem.at[s]).wait()

    @pl.when(i == 0)
    def _():
        idx_copy(0).start()
        idx_copy(0).wait()
        start_gathers(0)

        @pl.when(nt > 1)
        def _():
            idx_copy(1).start()

    @pl.when(i + 1 < nt)
    def _():
        idx_copy(i + 1).wait()
        start_gathers(i + 1)

    @pl.when(i + 2 < nt)
    def _():
        idx_copy(i + 2).start()

    wait_gathers(slot)
    route = route_ref[...]
    ws = [route[:, ROUTE_W + k:ROUTE_W + k + 1] for k in range(TOP_K)]
    chunks = []
    ssq = jnp.zeros((tm, 1), F32)
    for s in range(sub):
        acc = x1_ref[:, s * LANES:(s + 1) * LANES]
        for k in range(TOP_K):
            acc = acc + ws[k] * ybuf[slot, pl.ds(k * tm * sub + s, tm, stride=sub), :]
        chunks.append(acc)
        ssq = ssq + jnp.sum(acc * acc, axis=-1, keepdims=True)
    inv = lax.rsqrt(ssq / d + NORM_EPS)
    for s in range(sub):
        sl = slice(s * LANES, (s + 1) * LANES)
        o_ref[:, sl] = chunks[s] * inv * g_ref[:, sl]


def _combine(x1, route, dest_t, ys, g_final):
    n, d = x1.shape
    tm = ROW_TILE
    sub = d // LANES
    row = lambda i: (i, 0)
    return pl.pallas_call(
        _combine_kernel,
        out_shape=jax.ShapeDtypeStruct((n, d), F32),
        grid=(n // tm,),
        in_specs=[pl.BlockSpec((tm, d), row), pl.BlockSpec((tm, LANES), row),
                  pl.BlockSpec(memory_space=pl.ANY), pl.BlockSpec(memory_space=pl.ANY),
                  pl.BlockSpec((1, d), lambda i: (0, 0))],
        out_specs=pl.BlockSpec((tm, d), row),
        scratch_shapes=[pltpu.VMEM((2, TOP_K * tm * sub, LANES), F32),
                        pltpu.SMEM((2, SUBLANES, tm), I32),
                        pltpu.SemaphoreType.DMA((2,)), pltpu.SemaphoreType.DMA((2,))],
        compiler_params=pltpu.CompilerParams(dimension_semantics=("arbitrary",),
                                             vmem_limit_bytes=VMEM_LIMIT),
        name="combine_norm",
    )(x1, route, dest_t, ys, g_final)


def _pad_heads(w, per_head, width=LANES):
    k = w.shape[0]
    w = w.reshape(k, HEADS, per_head)
    w = jnp.pad(w, ((0, 0), (0, 0), (0, width - per_head)))
    return w.reshape(k, HEADS * width)


def _aug_constants():
    hw = HEADS * LANES
    place = np.zeros((LANES, 2 * hw), np.float32)
    ones = np.zeros((1, 2 * hw), np.float32)
    for hd in range(HEADS):
        for piece in range(3):
            src = FG_LO + piece * HEADS + hd
            place[src, hd * LANES + AUG_LO + piece] = 1.0
            place[src, hw + hd * LANES + AUG_LO + 3 + piece] = -1.0
            ones[0, hd * LANES + AUG_LO + 3 + piece] = 1.0
            ones[0, hw + hd * LANES + AUG_LO + piece] = 1.0
    return jnp.asarray(place, BF16), jnp.asarray(ones, F32)


def _layer(x2, pos2, batch, seq_len, g_attn_norm, w_in, b_fgate, g_q_a, w_q_b, g_kv_a, w_kv_b,
           w_fox_out, w_mla_out, b_merge, w_o, g_ffn_norm, w_router, b_router, w_gu, b_gu,
           w_down, b_down, g_out):
    n, d = x2.shape
    fw = HEADS * HEAD_DIM
    o = 0
    w_qf = w_in[:, o:o + fw]; o += fw
    w_kf = w_in[:, o:o + fw]; o += fw
    w_vf = w_in[:, o:o + fw]; o += fw
    w_f = w_in[:, o:o + HEADS]; o += HEADS
    w_ql = w_in[:, o:o + Q_RANK]; o += Q_RANK
    w_ckv = w_in[:, o:o + KV_RANK]; o += KV_RANK
    w_kpe = w_in[:, o:o + ROPE_DIM]; o += ROPE_DIM
    w_gate = w_in[:, o:]

    wq = _pad_heads(w_qf * (HEAD_DIM ** -0.5), HEAD_DIM).astype(BF16)
    wk = _pad_heads(w_kf, HEAD_DIM).astype(BF16)
    wmisc = jnp.concatenate([jnp.zeros((d, PE_LO), F32), w_kpe, w_f,
                             jnp.zeros((d, LANES - FG_LO - HEADS), F32)], axis=1)
    wlat = jnp.concatenate([w_ql, w_ckv, wmisc], axis=1).astype(BF16)
    bf128 = jnp.zeros((1, LANES), F32).at[0, FG_LO:FG_LO + HEADS].set(b_fgate)
    wqb = _pad_heads(w_q_b, HEAD_DIM + ROPE_DIM).astype(BF16)
    wkv = w_kv_b.reshape(KV_RANK, HEADS, 2 * HEAD_DIM)
    wkbk = _pad_heads(wkv[:, :, :HEAD_DIM].reshape(KV_RANK, fw), HEAD_DIM).astype(BF16)
    wkbv = wkv[:, :, HEAD_DIM:].reshape(KV_RANK, fw).T.astype(BF16)
    tm = ROW_TILE
    tri = np.tril(np.ones((tm, tm), np.float32))
    ltri = jnp.asarray(tri, BF16)
    lstrict = jnp.asarray(tri - np.eye(tm, dtype=np.float32), BF16)
    ustrict = jnp.asarray(np.triu(np.ones((LANES, LANES), np.float32), 1), BF16)
    place, ones = _aug_constants()
    half = ROPE_DIM // 2
    inv_freq = ROPE_THETA ** (-jnp.arange(half, dtype=F32) / half)
    freq = jnp.zeros((1, LANES), F32).at[0, PE_LO:PE_MID].set(inv_freq).at[0, PE_MID:PE_HI].set(inv_freq)

    qf, kf, vf, qm, km, vm, gates = _inproj(
        x2, pos2, g_attn_norm.reshape(1, d), wq, wk, w_vf.T.astype(BF16), wlat, w_gate.astype(BF16),
        bf128, g_q_a.reshape(1, -1), wqb, g_kv_a.reshape(1, -1), wkbk, wkbv, ltri, place, ones,
        freq, seq_len=seq_len)

    y_fox = _attention(qf, kf, vf, batch=batch, seq_len=seq_len, chunk_mask=False)
    y_mla = _attention(qm, km, vm, batch=batch, seq_len=seq_len, chunk_mask=True)

    wr = jnp.pad(w_router, ((0, 0), (0, LANES - N_EXPERTS)))
    wrh = wr.astype(BF16)
    wrl = (wr - wrh.astype(F32)).astype(BF16)
    br = jnp.full((1, LANES), NEG, F32).at[0, :N_EXPERTS].set(b_router)
    n_blocks = n * TOP_K // EXPERT_ROWS + N_EXPERTS
    x1, route, dest_t, meta, xs = _outproj(
        y_fox, y_mla, gates, x2, b_merge.reshape(1, -1), w_fox_out.astype(BF16),
        w_mla_out.astype(BF16), w_o.astype(BF16), g_ffn_norm.reshape(1, d), wrh, wrl, br, lstrict,
        ustrict, n_blocks=n_blocks)

    block_e = meta[:SUBLANES].reshape(-1)[:n_blocks]
    order = jnp.argsort(block_e, stable=True).astype(I32)
    nused = meta[SUBLANES, 0:1]
    be_sorted = jnp.minimum(block_e[order], N_EXPERTS - 1).astype(I32)
    be_sorted = jnp.where(jnp.arange(n_blocks) < nused[0], be_sorted,
                          be_sorted[jnp.maximum(nused[0] - 1, 0)])
    ys = _experts(order, be_sorted, nused, xs, w_gu, b_gu, w_down, b_down)
    return _combine(x1, route, dest_t, ys, g_out.reshape(1, d))


def kernel(x, positions, g_attn_norm, w_in, b_fgate, g_q_a, w_q_b, g_kv_a, w_kv_b, w_fox_out, w_mla_out, b_merge, w_o, g_ffn_norm, w_router, b_router, w_gu, b_gu, w_down, b_down, g_final):
    batch, seq_len, d = x.shape
    depth = w_in.shape[0]
    assert depth == 1, "the fused combine + final-norm kernel assumes a single layer"
    assert seq_len % ATT_TILE == 0 and d % LANES == 0
    assert (batch * seq_len * TOP_K) % EXPERT_ROWS == 0
    assert batch * seq_len * TOP_K // EXPERT_ROWS + N_EXPERTS <= SUBLANES * LANES
    x2 = x.reshape(batch * seq_len, d)
    pos2 = positions.reshape(batch * seq_len, 1).astype(I32)
    out = _layer(x2, pos2, batch, seq_len, g_attn_norm[0], w_in[0], b_fgate[0], g_q_a[0], w_q_b[0],
                 g_kv_a[0], w_kv_b[0], w_fox_out[0], w_mla_out[0], b_merge[0], w_o[0],
                 g_ffn_norm[0], w_router[0], b_router[0], w_gu[0], b_gu[0], w_down[0], b_down[0],
                 g_final)
    return out.reshape(batch, seq_len, d)
```

```python
import functools

import jax
import jax.numpy as jnp
import numpy as np
from jax import lax
from jax.experimental import pallas as pl
from jax.experimental.pallas import tpu as pltpu

F32 = jnp.float32
BF16 = jnp.bfloat16
I32 = jnp.int32

LANES = 128
SUBLANES = 8
VMEM_LIMIT = 56 * 1024 * 1024

NORM_EPS = 1e-6
HEADS = 8
HEAD_DIM = 64
ROPE_DIM = 32
Q_RANK = 256
KV_RANK = 128
N_EXPERTS = 32
TOP_K = 4
EXPERT_FF = 1024
SWIGLU_LIMIT = 7.0
SWIGLU_ALPHA = 1.702
ROPE_THETA = 10000.0
CHUNK = 64

NEG = -1e30
LOG2E = 1.4426950408889634

ROW_TILE = 256
MOE_TILE = 512
ATT_TILE = 512
ATT_HEADS = 2
EXPERT_ROWS = MOE_TILE

PE_LO, PE_MID, PE_HI = 64, 80, 96
FG_LO = 96
AUG_LO = 64
ROUTE_DEST = 0
ROUTE_W = 8


def _dot(a, b):
    return jnp.dot(a, b, preferred_element_type=F32)


def _dot_nt(a, b):
    return lax.dot_general(a, b, (((1,), (1,)), ((), ())), preferred_element_type=F32)


def _split3(a):
    hi = a.astype(BF16)
    r1 = a - hi.astype(F32)
    mid = r1.astype(BF16)
    lo = (r1 - mid.astype(F32)).astype(BF16)
    return hi, mid, lo


def _rms(x, g):
    return x * lax.rsqrt(jnp.mean(x * x, axis=-1, keepdims=True) + NORM_EPS) * g


def _inproj_kernel(x_ref, pos_ref, g_ref, wq_ref, wk_ref, wv_ref, wlat_ref, wg_ref, bf_ref,
                   gq_ref, wqb_ref, gkv_ref, wkbk_ref, wkbv_ref, ltri_ref, place_ref, ones_ref,
                   freq_ref,
                   qf_ref, kf_ref, vf_ref, qm_ref, km_ref, vm_ref, gate_ref,
                   carry_ref, *, tiles_per_seq, mla_scale):
    i = pl.program_id(0)

    @pl.when(i % tiles_per_seq == 0)
    def _():
        carry_ref[...] = jnp.zeros_like(carry_ref)

    h = _rms(x_ref[...], g_ref[...]).astype(BF16)
    tm = h.shape[0]
    lane = lax.broadcasted_iota(I32, (tm, LANES), 1)

    gate_ref[...] = _dot(h, wg_ref[...]).astype(BF16)
    vf_ref[0] = _dot_nt(wv_ref[...], h).astype(BF16)

    lat = _dot(h, wlat_ref[...])
    q_lat = lat[:, :Q_RANK]
    c_kv = lat[:, Q_RANK:Q_RANK + KV_RANK]
    misc = lat[:, Q_RANK + KV_RANK:]

    z = misc + bf_ref[...]
    logf = jnp.minimum(z, 0.0) - jnp.log1p(jnp.exp(-jnp.abs(z)))
    fmask = (lane >= FG_LO) & (lane < FG_LO + HEADS)
    logf = jnp.where(fmask, logf, 0.0)
    l_hi, l_mid, l_lo = _split3(logf)
    ltri = ltri_ref[...]
    c = _dot(ltri, l_hi) + _dot(ltri, l_mid) + _dot(ltri, l_lo) + carry_ref[...]
    carry_ref[...] = c[tm - 1:tm, :]
    c_hi, c_mid, c_lo = _split3(c * LOG2E)
    c3 = (c_hi.astype(F32) + pltpu.roll(c_mid.astype(F32), HEADS, 1)
          + pltpu.roll(c_lo.astype(F32), 2 * HEADS, 1)).astype(BF16)
    aug = _dot(c3, place_ref[...]) + ones_ref[...]
    hw = HEADS * LANES
    qf_ref[...] = (_dot(h, wq_ref[...]) * LOG2E + aug[:, :hw]).astype(BF16)
    kf_ref[...] = (_dot(h, wk_ref[...]) + aug[:, hw:]).astype(BF16)

    qn = _rms(q_lat, gq_ref[...]).astype(BF16)
    kvn = _rms(c_kv, gkv_ref[...]).astype(BF16)
    qfull = _dot(qn, wqb_ref[...])
    knope = _dot(kvn, wkbk_ref[...])
    vm_ref[0] = _dot_nt(wkbv_ref[...], kvn).astype(BF16)

    ang = pos_ref[...].astype(F32) * freq_ref[...]
    cosv = jnp.cos(ang)
    sinv = jnp.sin(ang)
    s1 = jnp.where((lane >= PE_LO) & (lane < PE_MID), -sinv, 0.0)
    s2 = jnp.where((lane >= PE_MID) & (lane < PE_HI), sinv, 0.0)

    def rope(v):
        return v * cosv + pltpu.roll(v, LANES - 16, 1) * s1 + pltpu.roll(v, 16, 1) * s2

    kpe = jnp.where((lane >= PE_LO) & (lane < PE_HI), rope(misc), 0.0)
    for hd in range(HEADS):
        sl = slice(hd * LANES, (hd + 1) * LANES)
        qm_ref[:, sl] = (rope(qfull[:, sl]) * mla_scale).astype(BF16)
        km_ref[:, sl] = (knope[:, sl] + kpe).astype(BF16)


def _inproj(x2, pos2, g_attn, wq, wk, wv, wlat, wg, bf128, gq, wqb, gkv, wkbk, wkbv, ltri,
            place, ones, freq, *, seq_len):
    n, d = x2.shape
    tm = ROW_TILE
    hw = HEADS * LANES
    vw = HEADS * HEAD_DIM
    const = lambda i: (0, 0)
    row = lambda i: (i, 0)
    full = lambda a: pl.BlockSpec(a.shape, const)
    rows_out = lambda w: (jax.ShapeDtypeStruct((n, w), BF16), pl.BlockSpec((tm, w), row))
    vt_out = (jax.ShapeDtypeStruct((n // tm, vw, tm), BF16),
              pl.BlockSpec((1, vw, tm), lambda i: (i, 0, 0)))
    outs = [rows_out(hw), rows_out(hw), vt_out, rows_out(hw), rows_out(hw), vt_out,
            rows_out(wg.shape[1])]
    consts = (g_attn, wq, wk, wv, wlat, wg, bf128, gq, wqb, gkv, wkbk, wkbv, ltri, place, ones, freq)
    return pl.pallas_call(
        functools.partial(_inproj_kernel, tiles_per_seq=seq_len // tm,
                          mla_scale=float((HEAD_DIM + ROPE_DIM) ** -0.5) * LOG2E),
        out_shape=[o[0] for o in outs],
        grid=(n // tm,),
        in_specs=[pl.BlockSpec((tm, d), row), pl.BlockSpec((tm, 1), row)] + [full(a) for a in consts],
        out_specs=[o[1] for o in outs],
        scratch_shapes=[pltpu.VMEM((1, LANES), F32)],
        compiler_params=pltpu.CompilerParams(dimension_semantics=("arbitrary",),
                                             vmem_limit_bytes=VMEM_LIMIT),
        name="inproj",
    )(x2, pos2, *consts)


def _attn_kernel(q_ref, k_ref, vt_ref, o_ref, *, chunk_mask):
    i = pl.program_id(2)
    tq = q_ref.shape[0]
    tk = vt_ref.shape[2]

    def scores(j, masked, lo=0):
        start = pl.multiple_of(j * tk, tk)
        ss = [_dot_nt(k_ref[pl.ds(start, tk), hd * LANES:(hd + 1) * LANES],
                      q_ref[lo:, hd * LANES:(hd + 1) * LANES])
              for hd in range(ATT_HEADS)]
        if masked:
            keyg = lax.broadcasted_iota(I32, (tk, tq - lo), 0) + j * tk
            qlo = lax.broadcasted_iota(I32, (tk, tq - lo), 1) + (i * tq + lo)
            allowed = (keyg // CHUNK) <= (qlo // CHUNK) if chunk_mask else keyg <= qlo
            ss = [jnp.where(allowed, s, NEG) for s in ss]
        return ss

    ones = jnp.ones((2 * SUBLANES, tk), BF16)

    def update(j, ss, state, lo=0):
        vt = vt_ref[j]
        new = []
        for hd, s in enumerate(ss):
            m_all, acc_all = state[hd]
            m, acc = m_all[:, lo:], acc_all[:, lo:]
            m_new = jnp.maximum(m, jnp.max(s, axis=0, keepdims=True))
            alpha = jnp.exp2(m - m_new)
            p = jnp.exp2((s - m_new[0:1]).astype(BF16))
            va = jnp.concatenate([vt[hd * HEAD_DIM:(hd + 1) * HEAD_DIM, :], ones], axis=0)
            acc = alpha[0:1] * acc + _dot(va, p)
            if lo:
                m_new = jnp.concatenate([m_all[:, :lo], m_new], axis=1)
                acc = jnp.concatenate([acc_all[:, :lo], acc], axis=1)
            new.append((m_new, acc))
        return tuple(new)

    n_full = (i * tq) // tk
    group = max(1, tq // tk)

    def run_group(first, state, masked):
        los = [u * tk if masked else 0 for u in range(group)]
        ss = scores(first, masked, los[0])
        for u in range(group):
            nxt = scores(first + u + 1, masked, los[u + 1]) if u + 1 < group else None
            state = update(first + u, ss, state, los[u])
            ss = nxt
        return state

    init1 = (jnp.full((SUBLANES, tq), NEG, F32), jnp.zeros((HEAD_DIM + 2 * SUBLANES, tq), F32))
    state = lax.fori_loop(0, n_full // group, lambda jj, c: run_group(jj * group, c, False),
                          (init1,) * ATT_HEADS)
    state = run_group(n_full, state, True)
    out_t = jnp.concatenate([acc[:HEAD_DIM] / acc[HEAD_DIM:HEAD_DIM + 1] for _, acc in state],
                            axis=0)
    o_ref[...] = out_t.T.astype(o_ref.dtype)


def _attention(q, k, vt, *, batch, seq_len, chunk_mask):
    n = q.shape[0]
    t = ATT_TILE
    tk = vt.shape[2]
    nq = seq_len // t
    nkb = seq_len // tk
    hp = ATT_HEADS
    pairs = HEADS // hp
    return pl.pallas_call(
        functools.partial(_attn_kernel, chunk_mask=chunk_mask),
        out_shape=jax.ShapeDtypeStruct((n, HEADS * HEAD_DIM), BF16),
        grid=(batch, pairs, nq),
        in_specs=[pl.BlockSpec((t, hp * LANES), lambda b, p, i: (b * nq + i, p)),
                  pl.BlockSpec((seq_len, hp * LANES), lambda b, p, i: (b, p)),
                  pl.BlockSpec((nkb, hp * HEAD_DIM, tk), lambda b, p, i: (b, p, 0))],
        out_specs=pl.BlockSpec((t, hp * HEAD_DIM), lambda b, p, i: (b * nq + i, p)),
        compiler_params=pltpu.CompilerParams(
            dimension_semantics=("arbitrary", "arbitrary", "arbitrary"),
            vmem_limit_bytes=VMEM_LIMIT),
        name="attn_mla" if chunk_mask else "attn_fox",
    )(q, k, vt)


def _outproj_kernel(yf_ref, ym_ref, gate_ref, x_ref, bm_ref, wfo_ref, wmo_ref, wo_ref, gffn_ref,
                    wrh_ref, wrl_ref, br_ref, lstrict_ref, ustrict_ref,
                    x1_ref, route_ref, dest_ref, meta_ref, xs_hbm,
                    fill_ref, cur_ref, nfree_ref, tbl_ref,
                    hbuf, dbuf, dsm, zbuf, mbuf, msm, ssem, isem, zsem, msem):
    i = pl.program_id(0)
    nt = pl.num_programs(0)
    d = x_ref.shape[1]
    tm = x_ref.shape[0]
    sub = d // LANES
    rows = EXPERT_ROWS
    n_blocks = xs_hbm.shape[0] // (rows * sub)
    slot = i % 2

    def idx_copy(s):
        return pltpu.make_async_copy(dbuf.at[s], dsm.at[s], isem.at[s])

    def start_scatters(s):
        for static_s in range(2):
            @pl.when(s == static_s)
            def _(static_s=static_s):
                def body(t, _):
                    src = hbuf.at[static_s, pl.ds(pl.multiple_of(t * sub, sub), sub), :]
                    for k in range(TOP_K):
                        dst = dsm[static_s, k, t]
                        pltpu.make_async_copy(
                            src, xs_hbm.at[pl.ds(pl.multiple_of(dst, sub), sub), :],
                            ssem.at[static_s]).start(priority=k % 2)
                    return 0

                lax.fori_loop(0, tm, body, 0, unroll=4)

    def wait_scatters(s):
        for _ in range(TOP_K):
            pltpu.make_async_copy(hbuf.at[s], xs_hbm.at[pl.ds(0, tm * sub), :], ssem.at[s]).wait()

    @pl.when(i == 0)
    def _():
        fill_ref[...] = jnp.full_like(fill_ref, float(rows))
        cur_ref[...] = jnp.zeros_like(cur_ref)
        nfree_ref[...] = jnp.zeros_like(nfree_ref)
        tbl_ref[...] = jnp.full_like(tbl_ref, float(N_EXPERTS))
        zbuf[...] = jnp.zeros_like(zbuf)

    @pl.when(i > 0)
    def _():
        idx_copy(1 - slot).wait()
        start_scatters(1 - slot)

    a = _dot(yf_ref[...], wfo_ref[...])
    b = _dot(ym_ref[...], wmo_ref[...])
    g = 1.0 / (1.0 + jnp.exp(-(gate_ref[...].astype(F32) + bm_ref[...])))
    merged = (g[:, :d] * a + g[:, d:] * b).astype(BF16)
    x1 = x_ref[...] + _dot(merged, wo_ref[...])
    x1_ref[...] = x1
    h2 = _rms(x1, gffn_ref[...])

    hi = h2.astype(BF16)
    lo = (h2 - hi.astype(F32)).astype(BF16)
    wrh = wrh_ref[...]
    logits = _dot(hi, wrh) + _dot(lo, wrh) + _dot(hi, wrl_ref[...]) + br_ref[...]

    lane = lax.broadcasted_iota(I32, (tm, LANES), 1)
    vals = logits
    sels, tops = [], []
    for _ in range(TOP_K):
        mx = jnp.max(vals, axis=-1, keepdims=True)
        idx = jnp.min(jnp.where(vals == mx, lane, LANES), axis=-1, keepdims=True)
        sel = lane == idx
        vals = jnp.where(sel, NEG, vals)
        sels.append(sel)
        tops.append(mx)
    es = [jnp.exp(tv - tops[0]) for tv in tops]
    den = es[0] + es[1] + es[2] + es[3]

    onehot = jnp.zeros((tm, LANES), F32)
    for sel in sels:
        onehot = onehot + sel.astype(F32)
    before = _dot(lstrict_ref[...], onehot.astype(BF16))
    cnt = jnp.sum(onehot, axis=0, keepdims=True)

    fill = fill_ref[...]
    cur = cur_ref[...]
    nfree = nfree_ref[...]
    need = ((fill + cnt) > float(rows)).astype(F32)
    need8 = jnp.broadcast_to(need, (SUBLANES, LANES)).astype(BF16)
    newid = nfree + _dot(need8, ustrict_ref[...])[0:1, :]
    pos = fill + before
    dest = jnp.where(pos < float(rows), cur * rows + pos, newid * rows + pos - float(rows))
    fill_ref[...] = fill + cnt - need * float(rows)
    cur_ref[...] = jnp.where(need > 0, newid, cur)
    nfree_ref[...] = nfree + jnp.sum(need, axis=-1, keepdims=True)
    blk_id = (lax.broadcasted_iota(I32, (SUBLANES, LANES), 0) * LANES
              + lax.broadcasted_iota(I32, (SUBLANES, LANES), 1)).astype(F32)
    tbl = tbl_ref[...]
    for e in range(N_EXPERTS):
        hit = (blk_id == newid[:, e:e + 1]) & (need[:, e:e + 1] > 0)
        tbl = jnp.where(hit, float(e), tbl)
    tbl_ref[...] = tbl

    route = jnp.zeros((tm, LANES), F32)
    for k in range(TOP_K):
        dest_k = jnp.sum(jnp.where(sels[k], dest, 0.0), axis=-1, keepdims=True)
        route = jnp.where(lane == ROUTE_DEST + k, dest_k, route)
        route = jnp.where(lane == ROUTE_W + k, es[k] / den, route)
    route_ref[...] = route
    dest_t = (route.T[0:SUBLANES, :] * float(sub)).astype(I32)
    dest_ref[0] = dest_t

    @pl.when(i >= 2)
    def _():
        wait_scatters(slot)

    for s in range(sub):
        hbuf[slot, pl.ds(s, tm, stride=sub), :] = h2[:, s * LANES:(s + 1) * LANES]
    dbuf[slot] = dest_t
    idx_copy(slot).start()

    @pl.when(i == nt - 1)
    def _():
        idx_copy(slot).wait()
        start_scatters(slot)

        @pl.when(i >= 1)
        def _():
            wait_scatters(1 - slot)

        wait_scatters(slot)

        meta = jnp.concatenate([tbl_ref[...], jnp.broadcast_to(nfree_ref[...], (SUBLANES, LANES))],
                               axis=0).astype(I32)
        meta_ref[...] = meta
        state = jnp.concatenate([fill_ref[...], cur_ref[...], nfree_ref[...],
                                 jnp.zeros((SUBLANES - 3, LANES), F32)], axis=0).astype(I32)
        mbuf[...] = state
        mcopy = pltpu.make_async_copy(mbuf, msm, msem.at[0])
        mcopy.start()
        mcopy.wait()

        def zero_copy(first_row, n_rows):
            return pltpu.make_async_copy(
                zbuf.at[pl.ds(0, n_rows * sub), :],
                xs_hbm.at[pl.ds(pl.multiple_of(first_row * sub, sub), n_rows * sub), :], zsem.at[0])

        chunks = [rows >> (s + 1) for s in range(rows.bit_length() - 1)]
        plans = []
        for e in range(N_EXPERTS):
            rem = rows - msm[0, e]
            at = msm[1, e] * rows + msm[0, e]
            for c in chunks:
                take = (rem & c) != 0
                plans.append((take, zero_copy(at, c)))
                at = at + jnp.where(take, c, 0)
        for j in range(N_EXPERTS):
            blk = msm[2, 0] + j
            safe = jnp.minimum(blk, n_blocks - 1)
            plans.append((blk < n_blocks, zero_copy(safe * rows, rows)))
        for take, cp in plans:
            pl.when(take)(cp.start)
        for take, cp in plans:
            pl.when(take)(cp.wait)


def _outproj(yf, ym, gates, x2, bm, wfo, wmo, wo, gffn, wrh, wrl, br, lstrict, ustrict, *, n_blocks):
    n, d = x2.shape
    tm = MOE_TILE
    rows = EXPERT_ROWS
    sub = d // LANES
    const = lambda i: (0, 0)
    row = lambda i: (i, 0)
    full = lambda a: pl.BlockSpec(a.shape, const)
    consts = (bm, wfo, wmo, wo, gffn, wrh, wrl, br, lstrict, ustrict)
    return pl.pallas_call(
        _outproj_kernel,
        out_shape=[jax.ShapeDtypeStruct((n, d), F32),
                   jax.ShapeDtypeStruct((n, LANES), F32),
                   jax.ShapeDtypeStruct((n // tm, SUBLANES, tm), I32),
                   jax.ShapeDtypeStruct((2 * SUBLANES, LANES), I32),
                   jax.ShapeDtypeStruct((n_blocks * rows * sub, LANES), F32)],
        grid=(n // tm,),
        in_specs=[pl.BlockSpec((tm, yf.shape[1]), row), pl.BlockSpec((tm, ym.shape[1]), row),
                  pl.BlockSpec((tm, gates.shape[1]), row), pl.BlockSpec((tm, d), row)]
                 + [full(a) for a in consts],
        out_specs=[pl.BlockSpec((tm, d), row), pl.BlockSpec((tm, LANES), row),
                   pl.BlockSpec((1, SUBLANES, tm), lambda i: (i, 0, 0)),
                   pl.BlockSpec((2 * SUBLANES, LANES), const),
                   pl.BlockSpec(memory_space=pl.ANY)],
        scratch_shapes=[pltpu.VMEM((1, LANES), F32), pltpu.VMEM((1, LANES), F32),
                        pltpu.VMEM((1, LANES), F32), pltpu.VMEM((SUBLANES, LANES), F32),
                        pltpu.VMEM((2, tm * sub, LANES), F32),
                        pltpu.VMEM((2, SUBLANES, tm), I32),
                        pltpu.SMEM((2, SUBLANES, tm), I32),
                        pltpu.VMEM((rows * sub, LANES), F32),
                        pltpu.VMEM((SUBLANES, LANES), I32),
                        pltpu.SMEM((SUBLANES, LANES), I32),
                        pltpu.SemaphoreType.DMA((2,)), pltpu.SemaphoreType.DMA((2,)),
                        pltpu.SemaphoreType.DMA((1,)), pltpu.SemaphoreType.DMA((1,))],
        compiler_params=pltpu.CompilerParams(dimension_semantics=("arbitrary",),
                                             vmem_limit_bytes=VMEM_LIMIT),
        name="outproj_router",
    )(yf, ym, gates, x2, *consts)


def _expert_kernel(order_ref, be_ref, nused_ref,
                   xs_ref, wgu_ref, bgu_ref, wd_ref, bd_ref, ys_ref, wgu_b, wd_b):
    b = pl.program_id(0)
    rows = EXPERT_ROWS
    sub = xs_ref.shape[0] // rows

    @pl.when(b < nused_ref[0])
    def _():
        changed = jnp.logical_or(b == 0, be_ref[b] != be_ref[jnp.maximum(b - 1, 0)])

        @pl.when(changed)
        def _():
            wgu_b[...] = wgu_ref[0].astype(BF16)
            wd_b[...] = wd_ref[0].astype(BF16)

        x = jnp.concatenate(
            [xs_ref[pl.ds(s, rows, stride=sub), :].astype(BF16) for s in range(sub)], axis=1)
        gu = _dot(x, wgu_b[...]) + bgu_ref[0]
        gate = jnp.minimum(gu[:, :EXPERT_FF], SWIGLU_LIMIT)
        up = jnp.clip(gu[:, EXPERT_FF:], -SWIGLU_LIMIT, SWIGLU_LIMIT)
        glu = gate * (1.0 / (1.0 + jnp.exp(-SWIGLU_ALPHA * gate)))
        y = _dot(((up + 1.0) * glu).astype(BF16), wd_b[...]) + bd_ref[0]
        for s in range(sub):
            ys_ref[pl.ds(s, rows, stride=sub), :] = y[:, s * LANES:(s + 1) * LANES]

    @pl.when(b >= nused_ref[0])
    def _():
        ys_ref[...] = jnp.zeros_like(ys_ref)


def _experts(order, block_e, nused, xs, w_gu, b_gu, w_d, b_d):
    e, d, ff2 = w_gu.shape
    rows = EXPERT_ROWS
    sub = d // LANES
    n_blocks = xs.shape[0] // (rows * sub)
    wmap = lambda b, od, be, nu: (be[b], 0, 0)
    xmap = lambda b, od, be, nu: (od[b], 0)
    grid_spec = pltpu.PrefetchScalarGridSpec(
        num_scalar_prefetch=3,
        grid=(n_blocks,),
        in_specs=[pl.BlockSpec((rows * sub, LANES), xmap),
                  pl.BlockSpec((1, d, ff2), wmap), pl.BlockSpec((1, 1, ff2), wmap),
                  pl.BlockSpec((1, ff2 // 2, d), wmap), pl.BlockSpec((1, 1, d), wmap)],
        out_specs=pl.BlockSpec((rows * sub, LANES), xmap),
        scratch_shapes=[pltpu.VMEM((d, ff2), BF16), pltpu.VMEM((ff2 // 2, d), BF16)],
    )
    return pl.pallas_call(
        _expert_kernel,
        out_shape=jax.ShapeDtypeStruct(xs.shape, F32),
        grid_spec=grid_spec,
        compiler_params=pltpu.CompilerParams(dimension_semantics=("arbitrary",),
                                             vmem_limit_bytes=VMEM_LIMIT),
        name="experts",
    )(order, block_e, nused, xs, w_gu, b_gu.reshape(e, 1, ff2), w_d, b_d.reshape(e, 1, d))


def _combine_kernel(x1_ref, route_ref, dest_hbm, ys_hbm, g_ref, o_ref, ybuf, dsm, gsem, isem):
    i = pl.program_id(0)
    nt = pl.num_programs(0)
    tm, d = x1_ref.shape
    sub = d // LANES
    slot = i % 2

    def idx_copy(tile):
        s = tile % 2
        return pltpu.make_async_copy(dest_hbm.at[tile], dsm.at[s], isem.at[s])

    def start_gathers(tile):
        for static_s in range(2):
            @pl.when(tile % 2 == static_s)
            def _(static_s=static_s):
                def body(t, _):
                    for k in range(TOP_K):
                        src = dsm[static_s, k, t]
                        pltpu.make_async_copy(
                            ys_hbm.at[pl.ds(pl.multiple_of(src, sub), sub), :],
                            ybuf.at[static_s, pl.ds(pl.multiple_of((k * tm + t) * sub, sub), sub), :],
                            gsem.at[static_s]).start(priority=k % 2)
                    return 0

                lax.fori_loop(0, tm, body, 0, unroll=4)

    def wait_gathers(s):
        pltpu.make_async_copy(ys_hbm.at[pl.ds(0, TOP_K * tm * sub), :], ybuf.at[s], gsem.at[s]).wait()

    @pl.when(i == 0)
    def _():
        idx_copy(0).start()
        idx_copy(0).wait()
        start_gathers(0)

        @pl.when(nt > 1)
        def _():
            idx_copy(1).start()

    @pl.when(i + 1 < nt)
    def _():
        idx_copy(i + 1).wait()
        start_gathers(i + 1)

    @pl.when(i + 2 < nt)
    def _():
        idx_copy(i + 2).start()

    wait_gathers(slot)
    route = route_ref[...]
    ws = [route[:, ROUTE_W + k:ROUTE_W + k + 1] for k in range(TOP_K)]
    chunks = []
    ssq = jnp.zeros((tm, 1), F32)
    for s in range(sub):
        acc = x1_ref[:, s * LANES:(s + 1) * LANES]
        for k in range(TOP_K):
            acc = acc + ws[k] * ybuf[slot, pl.ds(k * tm * sub + s, tm, stride=sub), :]
        chunks.append(acc)
        ssq = ssq + jnp.sum(acc * acc, axis=-1, keepdims=True)
    inv = lax.rsqrt(ssq / d + NORM_EPS)
    for s in range(sub):
        sl = slice(s * LANES, (s + 1) * LANES)
        o_ref[:, sl] = chunks[s] * inv * g_ref[:, sl]


def _combine(x1, route, dest_t, ys, g_final):
    n, d = x1.shape
    tm = MOE_TILE
    sub = d // LANES
    row = lambda i: (i, 0)
    return pl.pallas_call(
        _combine_kernel,
        out_shape=jax.ShapeDtypeStruct((n, d), F32),
        grid=(n // tm,),
        in_specs=[pl.BlockSpec((tm, d), row), pl.BlockSpec((tm, LANES), row),
                  pl.BlockSpec(memory_space=pl.ANY), pl.BlockSpec(memory_space=pl.ANY),
                  pl.BlockSpec((1, d), lambda i: (0, 0))],
        out_specs=pl.BlockSpec((tm, d), row),
        scratch_shapes=[pltpu.VMEM((2, TOP_K * tm * sub, LANES), F32),
                        pltpu.SMEM((2, SUBLANES, tm), I32),
                        pltpu.SemaphoreType.DMA((2,)), pltpu.SemaphoreType.DMA((2,))],
        compiler_params=pltpu.CompilerParams(dimension_semantics=("arbitrary",),
                                             vmem_limit_bytes=VMEM_LIMIT),
        name="combine_norm",
    )(x1, route, dest_t, ys, g_final)


def _pad_heads(w, per_head, width=LANES):
    k = w.shape[0]
    w = w.reshape(k, HEADS, per_head)
    w = jnp.pad(w, ((0, 0), (0, 0), (0, width - per_head)))
    return w.reshape(k, HEADS * width)


def _aug_constants():
    hw = HEADS * LANES
    place = np.zeros((LANES, 2 * hw), np.float32)
    ones = np.zeros((1, 2 * hw), np.float32)
    for hd in range(HEADS):
        for piece in range(3):
            src = FG_LO + piece * HEADS + hd
            place[src, hd * LANES + AUG_LO + piece] = 1.0
            place[src, hw + hd * LANES + AUG_LO + 3 + piece] = -1.0
            ones[0, hd * LANES + AUG_LO + 3 + piece] = 1.0
            ones[0, hw + hd * LANES + AUG_LO + piece] = 1.0
    return jnp.asarray(place, BF16), jnp.asarray(ones, F32)


def _layer(x2, pos2, batch, seq_len, g_attn_norm, w_in, b_fgate, g_q_a, w_q_b, g_kv_a, w_kv_b,
           w_fox_out, w_mla_out, b_merge, w_o, g_ffn_norm, w_router, b_router, w_gu, b_gu,
           w_down, b_down, g_out):
    n, d = x2.shape
    fw = HEADS * HEAD_DIM
    o = 0
    w_qf = w_in[:, o:o + fw]; o += fw
    w_kf = w_in[:, o:o + fw]; o += fw
    w_vf = w_in[:, o:o + fw]; o += fw
    w_f = w_in[:, o:o + HEADS]; o += HEADS
    w_ql = w_in[:, o:o + Q_RANK]; o += Q_RANK
    w_ckv = w_in[:, o:o + KV_RANK]; o += KV_RANK
    w_kpe = w_in[:, o:o + ROPE_DIM]; o += ROPE_DIM
    w_gate = w_in[:, o:]

    wq = _pad_heads(w_qf * (HEAD_DIM ** -0.5), HEAD_DIM).astype(BF16)
    wk = _pad_heads(w_kf, HEAD_DIM).astype(BF16)
    wmisc = jnp.concatenate([jnp.zeros((d, PE_LO), F32), w_kpe, w_f,
                             jnp.zeros((d, LANES - FG_LO - HEADS), F32)], axis=1)
    wlat = jnp.concatenate([w_ql, w_ckv, wmisc], axis=1).astype(BF16)
    bf128 = jnp.zeros((1, LANES), F32).at[0, FG_LO:FG_LO + HEADS].set(b_fgate)
    wqb = _pad_heads(w_q_b, HEAD_DIM + ROPE_DIM).astype(BF16)
    wkv = w_kv_b.reshape(KV_RANK, HEADS, 2 * HEAD_DIM)
    wkbk = _pad_heads(wkv[:, :, :HEAD_DIM].reshape(KV_RANK, fw), HEAD_DIM).astype(BF16)
    wkbv = wkv[:, :, HEAD_DIM:].reshape(KV_RANK, fw).T.astype(BF16)
    ltri = jnp.asarray(np.tril(np.ones((ROW_TILE, ROW_TILE), np.float32)), BF16)
    lstrict = jnp.asarray(np.tril(np.ones((MOE_TILE, MOE_TILE), np.float32), -1), BF16)
    ustrict = jnp.asarray(np.triu(np.ones((LANES, LANES), np.float32), 1), BF16)
    place, ones = _aug_constants()
    half = ROPE_DIM // 2
    inv_freq = ROPE_THETA ** (-jnp.arange(half, dtype=F32) / half)
    freq = jnp.zeros((1, LANES), F32).at[0, PE_LO:PE_MID].set(inv_freq).at[0, PE_MID:PE_HI].set(inv_freq)

    qf, kf, vf, qm, km, vm, gates = _inproj(
        x2, pos2, g_attn_norm.reshape(1, d), wq, wk, w_vf.T.astype(BF16), wlat, w_gate.astype(BF16),
        bf128, g_q_a.reshape(1, -1), wqb, g_kv_a.reshape(1, -1), wkbk, wkbv, ltri, place, ones,
        freq, seq_len=seq_len)

    y_fox = _attention(qf, kf, vf, batch=batch, seq_len=seq_len, chunk_mask=False)
    y_mla = _attention(qm, km, vm, batch=batch, seq_len=seq_len, chunk_mask=True)

    wr = jnp.pad(w_router, ((0, 0), (0, LANES - N_EXPERTS)))
    wrh = wr.astype(BF16)
    wrl = (wr - wrh.astype(F32)).astype(BF16)
    br = jnp.full((1, LANES), NEG, F32).at[0, :N_EXPERTS].set(b_router)
    n_blocks = n * TOP_K // EXPERT_ROWS + N_EXPERTS
    x1, route, dest_t, meta, xs = _outproj(
        y_fox, y_mla, gates, x2, b_merge.reshape(1, -1), w_fox_out.astype(BF16),
        w_mla_out.astype(BF16), w_o.astype(BF16), g_ffn_norm.reshape(1, d), wrh, wrl, br, lstrict,
        ustrict, n_blocks=n_blocks)

    block_e = meta[:SUBLANES].reshape(-1)[:n_blocks]
    order = jnp.argsort(block_e, stable=True).astype(I32)
    nused = meta[SUBLANES, 0:1]
    be_sorted = jnp.minimum(block_e[order], N_EXPERTS - 1).astype(I32)
    be_sorted = jnp.where(jnp.arange(n_blocks) < nused[0], be_sorted,
                          be_sorted[jnp.maximum(nused[0] - 1, 0)])
    ys = _experts(order, be_sorted, nused, xs, w_gu, b_gu, w_down, b_down)
    return _combine(x1, route, dest_t, ys, g_out.reshape(1, d))


def kernel(x, positions, g_attn_norm, w_in, b_fgate, g_q_a, w_q_b, g_kv_a, w_kv_b, w_fox_out, w_mla_out, b_merge, w_o, g_ffn_norm, w_router, b_router, w_gu, b_gu, w_down, b_down, g_final):
    batch, seq_len, d = x.shape
    depth = w_in.shape[0]
    assert depth == 1, "the fused combine + final-norm kernel assumes a single layer"
    assert seq_len % ATT_TILE == 0 and d % LANES == 0
    assert (batch * seq_len * TOP_K) % EXPERT_ROWS == 0
    assert batch * seq_len * TOP_K // EXPERT_ROWS + N_EXPERTS <= SUBLANES * LANES
    x2 = x.reshape(batch * seq_len, d)
    pos2 = positions.reshape(batch * seq_len, 1).astype(I32)
    out = _layer(x2, pos2, batch, seq_len, g_attn_norm[0], w_in[0], b_fgate[0], g_q_a[0], w_q_b[0],
                 g_kv_a[0], w_kv_b[0], w_fox_out[0], w_mla_out[0], b_merge[0], w_o[0],
                 g_ffn_norm[0], w_router[0], b_router[0], w_gu[0], b_gu[0], w_down[0], b_down[0],
                 g_final)
    return out.reshape(batch, seq_len, d)
```

```python
import functools

import jax
import jax.numpy as jnp
import numpy as np
from jax import lax
from jax.experimental import pallas as pl
from jax.experimental.pallas import tpu as pltpu

F32 = jnp.float32
BF16 = jnp.bfloat16
I32 = jnp.int32

LANES = 128
SUBLANES = 8
VMEM_LIMIT = 56 * 1024 * 1024

NORM_EPS = 1e-6
HEADS = 8
HEAD_DIM = 64
ROPE_DIM = 32
Q_RANK = 256
KV_RANK = 128
N_EXPERTS = 32
TOP_K = 4
EXPERT_FF = 1024
SWIGLU_LIMIT = 7.0
SWIGLU_ALPHA = 1.702
ROPE_THETA = 10000.0
CHUNK = 64

NEG = -1e30
LOG2E = 1.4426950408889634

ROW_TILE = 256
MOE_TILE = 512
ATT_TILE = 512
ATT_HEADS = 2
EXPERT_ROWS = MOE_TILE

PE_LO, PE_MID, PE_HI = 64, 80, 96
FG_LO = 96
AUG_LO = 64
ROUTE_DEST = 0
ROUTE_W = 8


def _dot(a, b):
    return jnp.dot(a, b, preferred_element_type=F32)


def _dot_nt(a, b):
    return lax.dot_general(a, b, (((1,), (1,)), ((), ())), preferred_element_type=F32)


def _split3(a):
    hi = a.astype(BF16)
    r1 = a - hi.astype(F32)
    mid = r1.astype(BF16)
    lo = (r1 - mid.astype(F32)).astype(BF16)
    return hi, mid, lo


def _rms(x, g):
    return x * lax.rsqrt(jnp.mean(x * x, axis=-1, keepdims=True) + NORM_EPS) * g


def _inproj_kernel(x_ref, pos_ref, g_ref, wq_ref, wk_ref, wv_ref, wlat_ref, wg_ref, bf_ref,
                   gq_ref, wqb_ref, gkv_ref, wkbk_ref, wkbv_ref, ltri_ref, place_ref, ones_ref,
                   freq_ref,
                   qf_ref, kf_ref, vf_ref, qm_ref, km_ref, vm_ref, gate_ref,
                   carry_ref, *, tiles_per_seq, mla_scale):
    i = pl.program_id(0)

    @pl.when(i % tiles_per_seq == 0)
    def _():
        carry_ref[...] = jnp.zeros_like(carry_ref)

    h = _rms(x_ref[...], g_ref[...]).astype(BF16)
    tm = h.shape[0]
    lane = lax.broadcasted_iota(I32, (tm, LANES), 1)

    lat = _dot(h, wlat_ref[...])
    q_lat = lat[:, :Q_RANK]
    c_kv = lat[:, Q_RANK:Q_RANK + KV_RANK]
    misc = lat[:, Q_RANK + KV_RANK:]

    qn = _rms(q_lat, gq_ref[...]).astype(BF16)
    kvn = _rms(c_kv, gkv_ref[...]).astype(BF16)
    qfull = _dot(qn, wqb_ref[...])
    knope = _dot(kvn, wkbk_ref[...])
    vm_ref[0] = _dot_nt(wkbv_ref[...], kvn).astype(BF16)

    z = misc + bf_ref[...]
    logf = jnp.minimum(z, 0.0) - jnp.log1p(jnp.exp(-jnp.abs(z)))
    fmask = (lane >= FG_LO) & (lane < FG_LO + HEADS)
    logf = jnp.where(fmask, logf, 0.0)
    l_hi, l_mid, l_lo = _split3(logf)
    ltri = ltri_ref[...]
    c = _dot(ltri, l_hi) + _dot(ltri, l_mid) + _dot(ltri, l_lo) + carry_ref[...]
    carry_ref[...] = c[tm - 1:tm, :]
    c_hi, c_mid, c_lo = _split3(c * LOG2E)
    c3 = (c_hi.astype(F32) + pltpu.roll(c_mid.astype(F32), HEADS, 1)
          + pltpu.roll(c_lo.astype(F32), 2 * HEADS, 1)).astype(BF16)
    augc = _dot(c3, place_ref[...]) + ones_ref[...]
    low = lane < HEAD_DIM
    amask = (lane >= AUG_LO) & (lane < AUG_LO + 6)

    def aug_block(half, hd):
        src = augc[:, half * LANES:(half + 1) * LANES]
        return jnp.where(amask, pltpu.roll(src, (AUG_LO - 16 * hd) % LANES, 1), 0.0)

    def head_block(compact, hd):
        pair = compact[:, (hd // 2) * LANES:(hd // 2 + 1) * LANES]
        if hd % 2:
            pair = pltpu.roll(pair, HEAD_DIM, 1)
        return jnp.where(low, pair, 0.0)

    qc = _dot(h, wq_ref[...]) * LOG2E
    kc = _dot(h, wk_ref[...])
    vf_ref[0] = _dot_nt(wv_ref[...], h).astype(BF16)
    gate_ref[...] = _dot(h, wg_ref[...]).astype(BF16)

    for hd in range(HEADS):
        sl = slice(hd * LANES, (hd + 1) * LANES)
        qf_ref[:, sl] = (head_block(qc, hd) + aug_block(0, hd)).astype(BF16)
        kf_ref[:, sl] = (head_block(kc, hd) + aug_block(1, hd)).astype(BF16)

    ang = pos_ref[...].astype(F32) * freq_ref[...]
    cosv = jnp.cos(ang)
    sinv = jnp.sin(ang)
    s1 = jnp.where((lane >= PE_LO) & (lane < PE_MID), -sinv, 0.0)
    s2 = jnp.where((lane >= PE_MID) & (lane < PE_HI), sinv, 0.0)

    def rope(v):
        return v * cosv + pltpu.roll(v, LANES - 16, 1) * s1 + pltpu.roll(v, 16, 1) * s2

    kpe = jnp.where((lane >= PE_LO) & (lane < PE_HI), rope(misc), 0.0)
    for hd in range(HEADS):
        sl = slice(hd * LANES, (hd + 1) * LANES)
        qm_ref[:, sl] = (rope(qfull[:, sl]) * mla_scale).astype(BF16)
        km_ref[:, sl] = (head_block(knope, hd) + kpe).astype(BF16)


def _inproj(x2, pos2, g_attn, wq, wk, wv, wlat, wg, bf128, gq, wqb, gkv, wkbk, wkbv, ltri,
            place, ones, freq, *, seq_len):
    n, d = x2.shape
    tm = ROW_TILE
    hw = HEADS * LANES
    vw = HEADS * HEAD_DIM
    const = lambda i: (0, 0)
    row = lambda i: (i, 0)
    full = lambda a: pl.BlockSpec(a.shape, const)
    rows_out = lambda w: (jax.ShapeDtypeStruct((n, w), BF16), pl.BlockSpec((tm, w), row))
    vt_out = (jax.ShapeDtypeStruct((n // tm, vw, tm), BF16),
              pl.BlockSpec((1, vw, tm), lambda i: (i, 0, 0)))
    outs = [rows_out(hw), rows_out(hw), vt_out, rows_out(hw), rows_out(hw), vt_out,
            rows_out(wg.shape[1])]
    consts = (g_attn, wq, wk, wv, wlat, wg, bf128, gq, wqb, gkv, wkbk, wkbv, ltri, place, ones, freq)
    return pl.pallas_call(
        functools.partial(_inproj_kernel, tiles_per_seq=seq_len // tm,
                          mla_scale=float((HEAD_DIM + ROPE_DIM) ** -0.5) * LOG2E),
        out_shape=[o[0] for o in outs],
        grid=(n // tm,),
        in_specs=[pl.BlockSpec((tm, d), row), pl.BlockSpec((tm, 1), row)] + [full(a) for a in consts],
        out_specs=[o[1] for o in outs],
        scratch_shapes=[pltpu.VMEM((1, LANES), F32)],
        compiler_params=pltpu.CompilerParams(dimension_semantics=("arbitrary",),
                                             vmem_limit_bytes=VMEM_LIMIT),
        name="inproj",
    )(x2, pos2, *consts)


def _attn_kernel(q_ref, k_ref, vt_ref, o_ref, *, chunk_mask):
    i = pl.program_id(2)
    tq = q_ref.shape[0]
    tk = vt_ref.shape[2]

    def scores(j, masked, lo=0):
        start = pl.multiple_of(j * tk, tk)
        ss = [_dot_nt(k_ref[pl.ds(start, tk), hd * LANES:(hd + 1) * LANES],
                      q_ref[lo:, hd * LANES:(hd + 1) * LANES])
              for hd in range(ATT_HEADS)]
        if masked:
            keyg = lax.broadcasted_iota(I32, (tk, tq - lo), 0) + j * tk
            qlo = lax.broadcasted_iota(I32, (tk, tq - lo), 1) + (i * tq + lo)
            allowed = (keyg // CHUNK) <= (qlo // CHUNK) if chunk_mask else keyg <= qlo
            ss = [jnp.where(allowed, s, NEG) for s in ss]
        return ss

    ones = jnp.ones((2 * SUBLANES, tk), BF16)

    def update(j, ss, state, lo=0):
        vt = vt_ref[j]
        new = []
        for hd, s in enumerate(ss):
            m_all, acc_all = state[hd]
            m, acc = m_all[:, lo:], acc_all[:, lo:]
            m_new = jnp.maximum(m, jnp.max(s, axis=0, keepdims=True))
            alpha = jnp.exp2(m - m_new)
            p = jnp.exp2((s - m_new[0:1]).astype(BF16))
            va = jnp.concatenate([vt[hd * HEAD_DIM:(hd + 1) * HEAD_DIM, :], ones], axis=0)
            acc = alpha[0:1] * acc + _dot(va, p)
            if lo:
                m_new = jnp.concatenate([m_all[:, :lo], m_new], axis=1)
                acc = jnp.concatenate([acc_all[:, :lo], acc], axis=1)
            new.append((m_new, acc))
        return tuple(new)

    n_full = (i * tq) // tk
    group = max(1, tq // tk)

    def run_group(first, state, masked):
        los = [u * tk if masked else 0 for u in range(group)]
        ss = scores(first, masked, los[0])
        for u in range(group):
            nxt = scores(first + u + 1, masked, los[u + 1]) if u + 1 < group else None
            state = update(first + u, ss, state, los[u])
            ss = nxt
        return state

    init1 = (jnp.full((SUBLANES, tq), NEG, F32), jnp.zeros((HEAD_DIM + 2 * SUBLANES, tq), F32))
    state = lax.fori_loop(0, n_full // group, lambda jj, c: run_group(jj * group, c, False),
                          (init1,) * ATT_HEADS)
    state = run_group(n_full, state, True)
    out_t = jnp.concatenate([acc[:HEAD_DIM] / acc[HEAD_DIM:HEAD_DIM + 1] for _, acc in state],
                            axis=0)
    o_ref[...] = out_t.T.astype(o_ref.dtype)


def _attention(q, k, vt, *, batch, seq_len, chunk_mask):
    n = q.shape[0]
    t = ATT_TILE
    tk = vt.shape[2]
    nq = seq_len // t
    nkb = seq_len // tk
    hp = ATT_HEADS
    pairs = HEADS // hp
    return pl.pallas_call(
        functools.partial(_attn_kernel, chunk_mask=chunk_mask),
        out_shape=jax.ShapeDtypeStruct((n, HEADS * HEAD_DIM), BF16),
        grid=(batch, pairs, nq),
        in_specs=[pl.BlockSpec((t, hp * LANES), lambda b, p, i: (b * nq + i, p)),
                  pl.BlockSpec((seq_len, hp * LANES), lambda b, p, i: (b, p)),
                  pl.BlockSpec((nkb, hp * HEAD_DIM, tk), lambda b, p, i: (b, p, 0))],
        out_specs=pl.BlockSpec((t, hp * HEAD_DIM), lambda b, p, i: (b * nq + i, p)),
        compiler_params=pltpu.CompilerParams(
            dimension_semantics=("arbitrary", "arbitrary", "arbitrary"),
            vmem_limit_bytes=VMEM_LIMIT),
        name="attn_mla" if chunk_mask else "attn_fox",
    )(q, k, vt)


def _outproj_kernel(yf_ref, ym_ref, gate_ref, x_ref, bm_ref, wfo_ref, wmo_ref, wo_ref, gffn_ref,
                    wrh_ref, wrl_ref, br_ref, lstrict_ref, ustrict_ref,
                    x1_ref, route_ref, dest_ref, meta_ref, xs_hbm,
                    fill_ref, cur_ref, nfree_ref, tbl_ref,
                    hbuf, dbuf, dsm, zbuf, mbuf, msm, ssem, isem, zsem, msem):
    i = pl.program_id(0)
    nt = pl.num_programs(0)
    d = x_ref.shape[1]
    tm = x_ref.shape[0]
    sub = d // LANES
    rows = EXPERT_ROWS
    n_blocks = xs_hbm.shape[0] // (rows * sub)
    slot = i % 2

    def idx_copy(s):
        return pltpu.make_async_copy(dbuf.at[s], dsm.at[s], isem.at[s])

    def start_scatters(s):
        for static_s in range(2):
            @pl.when(s == static_s)
            def _(static_s=static_s):
                def body(t, _):
                    src = hbuf.at[static_s, pl.ds(pl.multiple_of(t * sub, sub), sub), :]
                    for k in range(TOP_K):
                        dst = dsm[static_s, k, t]
                        pltpu.make_async_copy(
                            src, xs_hbm.at[pl.ds(pl.multiple_of(dst, sub), sub), :],
                            ssem.at[static_s]).start(priority=k % 2)
                    return 0

                lax.fori_loop(0, tm, body, 0, unroll=4)

    def wait_scatters(s):
        for _ in range(TOP_K):
            pltpu.make_async_copy(hbuf.at[s], xs_hbm.at[pl.ds(0, tm * sub), :], ssem.at[s]).wait()

    @pl.when(i == 0)
    def _():
        fill_ref[...] = jnp.full_like(fill_ref, float(rows))
        cur_ref[...] = jnp.zeros_like(cur_ref)
        nfree_ref[...] = jnp.zeros_like(nfree_ref)
        tbl_ref[...] = jnp.full_like(tbl_ref, float(N_EXPERTS))
        zbuf[...] = jnp.zeros_like(zbuf)

    @pl.when(i > 0)
    def _():
        idx_copy(1 - slot).wait()
        start_scatters(1 - slot)

    a = _dot(yf_ref[...], wfo_ref[...])
    b = _dot(ym_ref[...], wmo_ref[...])
    g = 1.0 / (1.0 + jnp.exp(-(gate_ref[...].astype(F32) + bm_ref[...])))
    merged = (g[:, :d] * a + g[:, d:] * b).astype(BF16)
    x1 = x_ref[...] + _dot(merged, wo_ref[...])
    x1_ref[...] = x1
    h2 = _rms(x1, gffn_ref[...])

    hi = h2.astype(BF16)
    lo = (h2 - hi.astype(F32)).astype(BF16)
    wrh = wrh_ref[...]
    logits = _dot(hi, wrh) + _dot(lo, wrh) + _dot(hi, wrl_ref[...]) + br_ref[...]

    lane = lax.broadcasted_iota(I32, (tm, LANES), 1)
    vals = logits
    sels, tops = [], []
    for _ in range(TOP_K):
        mx = jnp.max(vals, axis=-1, keepdims=True)
        idx = jnp.min(jnp.where(vals == mx, lane, LANES), axis=-1, keepdims=True)
        sel = lane == idx
        vals = jnp.where(sel, NEG, vals)
        sels.append(sel)
        tops.append(mx)
    es = [jnp.exp(tv - tops[0]) for tv in tops]
    den = es[0] + es[1] + es[2] + es[3]

    onehot = jnp.zeros((tm, LANES), F32)
    for sel in sels:
        onehot = onehot + sel.astype(F32)
    before = _dot(lstrict_ref[...], onehot.astype(BF16))
    cnt = jnp.sum(onehot, axis=0, keepdims=True)

    fill = fill_ref[...]
    cur = cur_ref[...]
    nfree = nfree_ref[...]
    need = ((fill + cnt) > float(rows)).astype(F32)
    need8 = jnp.broadcast_to(need, (SUBLANES, LANES)).astype(BF16)
    newid = nfree + _dot(need8, ustrict_ref[...])[0:1, :]
    pos = fill + before
    dest = jnp.where(pos < float(rows), cur * rows + pos, newid * rows + pos - float(rows))
    fill_ref[...] = fill + cnt - need * float(rows)
    cur_ref[...] = jnp.where(need > 0, newid, cur)
    nfree_ref[...] = nfree + jnp.sum(need, axis=-1, keepdims=True)
    blk_id = (lax.broadcasted_iota(I32, (SUBLANES, LANES), 0) * LANES
              + lax.broadcasted_iota(I32, (SUBLANES, LANES), 1)).astype(F32)
    tbl = tbl_ref[...]
    for e in range(N_EXPERTS):
        hit = (blk_id == newid[:, e:e + 1]) & (need[:, e:e + 1] > 0)
        tbl = jnp.where(hit, float(e), tbl)
    tbl_ref[...] = tbl

    route = jnp.zeros((tm, LANES), F32)
    for k in range(TOP_K):
        dest_k = jnp.sum(jnp.where(sels[k], dest, 0.0), axis=-1, keepdims=True)
        route = jnp.where(lane == ROUTE_DEST + k, dest_k, route)
        route = jnp.where(lane == ROUTE_W + k, es[k] / den, route)
    route_ref[...] = route
    dest_t = (route.T[0:SUBLANES, :] * float(sub)).astype(I32)
    dest_ref[0] = dest_t

    @pl.when(i >= 2)
    def _():
        wait_scatters(slot)

    for s in range(sub):
        hbuf[slot, pl.ds(s, tm, stride=sub), :] = h2[:, s * LANES:(s + 1) * LANES]
    dbuf[slot] = dest_t
    idx_copy(slot).start()

    @pl.when(i == nt - 1)
    def _():
        idx_copy(slot).wait()
        start_scatters(slot)

        @pl.when(i >= 1)
        def _():
            wait_scatters(1 - slot)

        wait_scatters(slot)

        meta = jnp.concatenate([tbl_ref[...], jnp.broadcast_to(nfree_ref[...], (SUBLANES, LANES))],
                               axis=0).astype(I32)
        meta_ref[...] = meta
        state = jnp.concatenate([fill_ref[...], cur_ref[...], nfree_ref[...],
                                 jnp.zeros((SUBLANES - 3, LANES), F32)], axis=0).astype(I32)
        mbuf[...] = state
        mcopy = pltpu.make_async_copy(mbuf, msm, msem.at[0])
        mcopy.start()
        mcopy.wait()

        def zero_copy(first_row, n_rows):
            return pltpu.make_async_copy(
                zbuf.at[pl.ds(0, n_rows * sub), :],
                xs_hbm.at[pl.ds(pl.multiple_of(first_row * sub, sub), n_rows * sub), :], zsem.at[0])

        chunks = [rows >> (s + 1) for s in range(rows.bit_length() - 1)]
        plans = []
        for e in range(N_EXPERTS):
            rem = rows - msm[0, e]
            at = msm[1, e] * rows + msm[0, e]
            for c in chunks:
                take = (rem & c) != 0
                plans.append((take, zero_copy(at, c)))
                at = at + jnp.where(take, c, 0)
        for j in range(N_EXPERTS):
            blk = msm[2, 0] + j
            safe = jnp.minimum(blk, n_blocks - 1)
            plans.append((blk < n_blocks, zero_copy(safe * rows, rows)))
        for take, cp in plans:
            pl.when(take)(cp.start)
        for take, cp in plans:
            pl.when(take)(cp.wait)


def _outproj(yf, ym, gates, x2, bm, wfo, wmo, wo, gffn, wrh, wrl, br, lstrict, ustrict, *, n_blocks):
    n, d = x2.shape
    tm = MOE_TILE
    rows = EXPERT_ROWS
    sub = d // LANES
    const = lambda i: (0, 0)
    row = lambda i: (i, 0)
    full = lambda a: pl.BlockSpec(a.shape, const)
    consts = (bm, wfo, wmo, wo, gffn, wrh, wrl, br, lstrict, ustrict)
    return pl.pallas_call(
        _outproj_kernel,
        out_shape=[jax.ShapeDtypeStruct((n, d), F32),
                   jax.ShapeDtypeStruct((n, LANES), F32),
                   jax.ShapeDtypeStruct((n // tm, SUBLANES, tm), I32),
                   jax.ShapeDtypeStruct((2 * SUBLANES, LANES), I32),
                   jax.ShapeDtypeStruct((n_blocks * rows * sub, LANES), F32)],
        grid=(n // tm,),
        in_specs=[pl.BlockSpec((tm, yf.shape[1]), row), pl.BlockSpec((tm, ym.shape[1]), row),
                  pl.BlockSpec((tm, gates.shape[1]), row), pl.BlockSpec((tm, d), row)]
                 + [full(a) for a in consts],
        out_specs=[pl.BlockSpec((tm, d), row), pl.BlockSpec((tm, LANES), row),
                   pl.BlockSpec((1, SUBLANES, tm), lambda i: (i, 0, 0)),
                   pl.BlockSpec((2 * SUBLANES, LANES), const),
                   pl.BlockSpec(memory_space=pl.ANY)],
        scratch_shapes=[pltpu.VMEM((1, LANES), F32), pltpu.VMEM((1, LANES), F32),
                        pltpu.VMEM((1, LANES), F32), pltpu.VMEM((SUBLANES, LANES), F32),
                        pltpu.VMEM((2, tm * sub, LANES), F32),
                        pltpu.VMEM((2, SUBLANES, tm), I32),
                        pltpu.SMEM((2, SUBLANES, tm), I32),
                        pltpu.VMEM((rows * sub, LANES), F32),
                        pltpu.VMEM((SUBLANES, LANES), I32),
                        pltpu.SMEM((SUBLANES, LANES), I32),
                        pltpu.SemaphoreType.DMA((2,)), pltpu.SemaphoreType.DMA((2,)),
                        pltpu.SemaphoreType.DMA((1,)), pltpu.SemaphoreType.DMA((1,))],
        compiler_params=pltpu.CompilerParams(dimension_semantics=("arbitrary",),
                                             vmem_limit_bytes=VMEM_LIMIT),
        name="outproj_router",
    )(yf, ym, gates, x2, *consts)


def _expert_kernel(order_ref, be_ref, nused_ref,
                   xs_ref, wgu_ref, bgu_ref, wd_ref, bd_ref, ys_ref, wgu_b, wd_b):
    b = pl.program_id(0)
    rows = EXPERT_ROWS
    sub = xs_ref.shape[0] // rows

    @pl.when(b < nused_ref[0])
    def _():
        changed = jnp.logical_or(b == 0, be_ref[b] != be_ref[jnp.maximum(b - 1, 0)])

        @pl.when(changed)
        def _():
            wgu_b[...] = wgu_ref[0].astype(BF16)
            wd_b[...] = wd_ref[0].astype(BF16)

        x = jnp.concatenate(
            [xs_ref[pl.ds(s, rows, stride=sub), :].astype(BF16) for s in range(sub)], axis=1)
        gu = _dot(x, wgu_b[...]) + bgu_ref[0]
        gate = jnp.minimum(gu[:, :EXPERT_FF], SWIGLU_LIMIT)
        up = jnp.clip(gu[:, EXPERT_FF:], -SWIGLU_LIMIT, SWIGLU_LIMIT)
        glu = gate * (1.0 / (1.0 + jnp.exp(-SWIGLU_ALPHA * gate)))
        y = _dot(((up + 1.0) * glu).astype(BF16), wd_b[...]) + bd_ref[0]
        for s in range(sub):
            ys_ref[pl.ds(s, rows, stride=sub), :] = y[:, s * LANES:(s + 1) * LANES]

    @pl.when(b >= nused_ref[0])
    def _():
        ys_ref[...] = jnp.zeros_like(ys_ref)


def _experts(order, block_e, nused, xs, w_gu, b_gu, w_d, b_d):
    e, d, ff2 = w_gu.shape
    rows = EXPERT_ROWS
    sub = d // LANES
    n_blocks = xs.shape[0] // (rows * sub)
    wmap = lambda b, od, be, nu: (be[b], 0, 0)
    xmap = lambda b, od, be, nu: (od[b], 0)
    grid_spec = pltpu.PrefetchScalarGridSpec(
        num_scalar_prefetch=3,
        grid=(n_blocks,),
        in_specs=[pl.BlockSpec((rows * sub, LANES), xmap),
                  pl.BlockSpec((1, d, ff2), wmap), pl.BlockSpec((1, 1, ff2), wmap),
                  pl.BlockSpec((1, ff2 // 2, d), wmap), pl.BlockSpec((1, 1, d), wmap)],
        out_specs=pl.BlockSpec((rows * sub, LANES), xmap),
        scratch_shapes=[pltpu.VMEM((d, ff2), BF16), pltpu.VMEM((ff2 // 2, d), BF16)],
    )
    return pl.pallas_call(
        _expert_kernel,
        out_shape=jax.ShapeDtypeStruct(xs.shape, F32),
        grid_spec=grid_spec,
        compiler_params=pltpu.CompilerParams(dimension_semantics=("arbitrary",),
                                             vmem_limit_bytes=VMEM_LIMIT),
        name="experts",
    )(order, block_e, nused, xs, w_gu, b_gu.reshape(e, 1, ff2), w_d, b_d.reshape(e, 1, d))


def _combine_kernel(x1_ref, route_ref, dest_hbm, ys_hbm, g_ref, o_ref, ybuf, dsm, gsem, isem):
    i = pl.program_id(0)
    nt = pl.num_programs(0)
    tm, d = x1_ref.shape
    sub = d // LANES
    slot = i % 2

    def idx_copy(tile):
        s = tile % 2
        return pltpu.make_async_copy(dest_hbm.at[tile], dsm.at[s], isem.at[s])

    def start_gathers(tile):
        for static_s in range(2):
            @pl.when(tile % 2 == static_s)
            def _(static_s=static_s):
                def body(t, _):
                    for k in range(TOP_K):
                        src = dsm[static_s, k, t]
                        pltpu.make_async_copy(
                            ys_hbm.at[pl.ds(pl.multiple_of(src, sub), sub), :],
                            ybuf.at[static_s, pl.ds(pl.multiple_of((t * TOP_K + k) * sub, sub), sub), :],
                            gsem.at[static_s]).start(priority=k % 2)
                    return 0

                lax.fori_loop(0, tm, body, 0, unroll=4)

    def wait_gathers(s):
        pltpu.make_async_copy(ys_hbm.at[pl.ds(0, TOP_K * tm * sub), :], ybuf.at[s], gsem.at[s]).wait()

    @pl.when(i == 0)
    def _():
        idx_copy(0).start()
        idx_copy(0).wait()
        start_gathers(0)

        @pl.when(nt > 1)
        def _():
            idx_copy(1).start()

    @pl.when(i + 1 < nt)
    def _():
        idx_copy(i + 1).wait()
        start_gathers(i + 1)

    @pl.when(i + 2 < nt)
    def _():
        idx_copy(i + 2).start()

    wait_gathers(slot)
    route = route_ref[...]
    ws = [route[:, ROUTE_W + k:ROUTE_W + k + 1] for k in range(TOP_K)]
    chunks = []
    ssq = jnp.zeros((tm, 1), F32)
    for s in range(sub):
        acc = x1_ref[:, s * LANES:(s + 1) * LANES]
        for k in range(TOP_K):
            acc = acc + ws[k] * ybuf[slot, pl.ds(k * sub + s, tm, stride=TOP_K * sub), :]
        chunks.append(acc)
        ssq = ssq + jnp.sum(acc * acc, axis=-1, keepdims=True)
    inv = lax.rsqrt(ssq / d + NORM_EPS)
    for s in range(sub):
        sl = slice(s * LANES, (s + 1) * LANES)
        o_ref[:, sl] = chunks[s] * inv * g_ref[:, sl]


def _combine(x1, route, dest_t, ys, g_final):
    n, d = x1.shape
    tm = MOE_TILE
    sub = d // LANES
    row = lambda i: (i, 0)
    return pl.pallas_call(
        _combine_kernel,
        out_shape=jax.ShapeDtypeStruct((n, d), F32),
        grid=(n // tm,),
        in_specs=[pl.BlockSpec((tm, d), row), pl.BlockSpec((tm, LANES), row),
                  pl.BlockSpec(memory_space=pl.ANY), pl.BlockSpec(memory_space=pl.ANY),
                  pl.BlockSpec((1, d), lambda i: (0, 0))],
        out_specs=pl.BlockSpec((tm, d), row),
        scratch_shapes=[pltpu.VMEM((2, TOP_K * tm * sub, LANES), F32),
                        pltpu.SMEM((2, SUBLANES, tm), I32),
                        pltpu.SemaphoreType.DMA((2,)), pltpu.SemaphoreType.DMA((2,))],
        compiler_params=pltpu.CompilerParams(dimension_semantics=("arbitrary",),
                                             vmem_limit_bytes=VMEM_LIMIT),
        name="combine_norm",
    )(x1, route, dest_t, ys, g_final)


def _pad_heads(w, per_head, width=LANES):
    k = w.shape[0]
    w = w.reshape(k, HEADS, per_head)
    w = jnp.pad(w, ((0, 0), (0, 0), (0, width - per_head)))
    return w.reshape(k, HEADS * width)


def _aug_constants():
    place = np.zeros((LANES, 2 * LANES), np.float32)
    ones = np.zeros((1, 2 * LANES), np.float32)
    for hd in range(HEADS):
        for piece in range(3):
            src = FG_LO + piece * HEADS + hd
            place[src, 16 * hd + piece] = 1.0
            place[src, LANES + 16 * hd + 3 + piece] = -1.0
            ones[0, 16 * hd + 3 + piece] = 1.0
            ones[0, LANES + 16 * hd + piece] = 1.0
    return jnp.asarray(place, BF16), jnp.asarray(ones, F32)


def _layer(x2, pos2, batch, seq_len, g_attn_norm, w_in, b_fgate, g_q_a, w_q_b, g_kv_a, w_kv_b,
           w_fox_out, w_mla_out, b_merge, w_o, g_ffn_norm, w_router, b_router, w_gu, b_gu,
           w_down, b_down, g_out):
    n, d = x2.shape
    fw = HEADS * HEAD_DIM
    o = 0
    w_qf = w_in[:, o:o + fw]; o += fw
    w_kf = w_in[:, o:o + fw]; o += fw
    w_vf = w_in[:, o:o + fw]; o += fw
    w_f = w_in[:, o:o + HEADS]; o += HEADS
    w_ql = w_in[:, o:o + Q_RANK]; o += Q_RANK
    w_ckv = w_in[:, o:o + KV_RANK]; o += KV_RANK
    w_kpe = w_in[:, o:o + ROPE_DIM]; o += ROPE_DIM
    w_gate = w_in[:, o:]

    wq = (w_qf * (HEAD_DIM ** -0.5)).astype(BF16)
    wk = w_kf.astype(BF16)
    wmisc = jnp.concatenate([jnp.zeros((d, PE_LO), F32), w_kpe, w_f,
                             jnp.zeros((d, LANES - FG_LO - HEADS), F32)], axis=1)
    wlat = jnp.concatenate([w_ql, w_ckv, wmisc], axis=1).astype(BF16)
    bf128 = jnp.zeros((1, LANES), F32).at[0, FG_LO:FG_LO + HEADS].set(b_fgate)
    wqb = _pad_heads(w_q_b, HEAD_DIM + ROPE_DIM).astype(BF16)
    wkv = w_kv_b.reshape(KV_RANK, HEADS, 2 * HEAD_DIM)
    wkbk = wkv[:, :, :HEAD_DIM].reshape(KV_RANK, fw).astype(BF16)
    wkbv = wkv[:, :, HEAD_DIM:].reshape(KV_RANK, fw).T.astype(BF16)
    ltri = jnp.asarray(np.tril(np.ones((ROW_TILE, ROW_TILE), np.float32)), BF16)
    lstrict = jnp.asarray(np.tril(np.ones((MOE_TILE, MOE_TILE), np.float32), -1), BF16)
    ustrict = jnp.asarray(np.triu(np.ones((LANES, LANES), np.float32), 1), BF16)
    place, ones = _aug_constants()
    half = ROPE_DIM // 2
    inv_freq = ROPE_THETA ** (-jnp.arange(half, dtype=F32) / half)
    freq = jnp.zeros((1, LANES), F32).at[0, PE_LO:PE_MID].set(inv_freq).at[0, PE_MID:PE_HI].set(inv_freq)

    qf, kf, vf, qm, km, vm, gates = _inproj(
        x2, pos2, g_attn_norm.reshape(1, d), wq, wk, w_vf.T.astype(BF16), wlat, w_gate.astype(BF16),
        bf128, g_q_a.reshape(1, -1), wqb, g_kv_a.reshape(1, -1), wkbk, wkbv, ltri, place, ones,
        freq, seq_len=seq_len)

    y_fox = _attention(qf, kf, vf, batch=batch, seq_len=seq_len, chunk_mask=False)
    y_mla = _attention(qm, km, vm, batch=batch, seq_len=seq_len, chunk_mask=True)

    wr = jnp.pad(w_router, ((0, 0), (0, LANES - N_EXPERTS)))
    wrh = wr.astype(BF16)
    wrl = (wr - wrh.astype(F32)).astype(BF16)
    br = jnp.full((1, LANES), NEG, F32).at[0, :N_EXPERTS].set(b_router)
    n_blocks = n * TOP_K // EXPERT_ROWS + N_EXPERTS
    x1, route, dest_t, meta, xs = _outproj(
        y_fox, y_mla, gates, x2, b_merge.reshape(1, -1), w_fox_out.astype(BF16),
        w_mla_out.astype(BF16), w_o.astype(BF16), g_ffn_norm.reshape(1, d), wrh, wrl, br, lstrict,
        ustrict, n_blocks=n_blocks)

    block_e = meta[:SUBLANES].reshape(-1)[:n_blocks]
    order = jnp.argsort(block_e, stable=True).astype(I32)
    nused = meta[SUBLANES, 0:1]
    be_sorted = jnp.minimum(block_e[order], N_EXPERTS - 1).astype(I32)
    be_sorted = jnp.where(jnp.arange(n_blocks) < nused[0], be_sorted,
                          be_sorted[jnp.maximum(nused[0] - 1, 0)])
    ys = _experts(order, be_sorted, nused, xs, w_gu, b_gu, w_down, b_down)
    return _combine(x1, route, dest_t, ys, g_out.reshape(1, d))


def kernel(x, positions, g_attn_norm, w_in, b_fgate, g_q_a, w_q_b, g_kv_a, w_kv_b, w_fox_out, w_mla_out, b_merge, w_o, g_ffn_norm, w_router, b_router, w_gu, b_gu, w_down, b_down, g_final):
    batch, seq_len, d = x.shape
    depth = w_in.shape[0]
    assert depth == 1, "the fused combine + final-norm kernel assumes a single layer"
    assert seq_len % ATT_TILE == 0 and d % LANES == 0
    assert (batch * seq_len * TOP_K) % EXPERT_ROWS == 0
    assert batch * seq_len * TOP_K // EXPERT_ROWS + N_EXPERTS <= SUBLANES * LANES
    x2 = x.reshape(batch * seq_len, d)
    pos2 = positions.reshape(batch * seq_len, 1).astype(I32)
    out = _layer(x2, pos2, batch, seq_len, g_attn_norm[0], w_in[0], b_fgate[0], g_q_a[0], w_q_b[0],
                 g_kv_a[0], w_kv_b[0], w_fox_out[0], w_mla_out[0], b_merge[0], w_o[0],
                 g_ffn_norm[0], w_router[0], b_router[0], w_gu[0], b_gu[0], w_down[0], b_down[0],
                 g_final)
    return out.reshape(batch, seq_len, d)
```

```python
import functools

import jax
import jax.numpy as jnp
import numpy as np
from jax import lax
from jax.experimental import pallas as pl
from jax.experimental.pallas import tpu as pltpu

F32 = jnp.float32
BF16 = jnp.bfloat16
I32 = jnp.int32

LANES = 128
SUBLANES = 8
VMEM_LIMIT = 56 * 1024 * 1024

NORM_EPS = 1e-6
HEADS = 8
HEAD_DIM = 64
ROPE_DIM = 32
Q_RANK = 256
KV_RANK = 128
N_EXPERTS = 32
TOP_K = 4
EXPERT_FF = 1024
SWIGLU_LIMIT = 7.0
SWIGLU_ALPHA = 1.702
ROPE_THETA = 10000.0
CHUNK = 64

NEG = -1e30
LOG2E = 1.4426950408889634

ROW_TILE = 256
MOE_TILE = 512
ATT_TILE = 512
ATT_HEADS = 2
EXPERT_ROWS = MOE_TILE

PE_LO, PE_MID, PE_HI = 64, 80, 96
FG_LO = 96
AUG_LO = 64
ROUTE_DEST = 0
ROUTE_W = 8


def _dot(a, b):
    return jnp.dot(a, b, preferred_element_type=F32)


def _dot_nt(a, b):
    return lax.dot_general(a, b, (((1,), (1,)), ((), ())), preferred_element_type=F32)


def _split3(a):
    hi = a.astype(BF16)
    r1 = a - hi.astype(F32)
    mid = r1.astype(BF16)
    lo = (r1 - mid.astype(F32)).astype(BF16)
    return hi, mid, lo


def _rms(x, g):
    return x * lax.rsqrt(jnp.mean(x * x, axis=-1, keepdims=True) + NORM_EPS) * g


def _inproj_kernel(x_ref, pos_ref, g_ref, wq_ref, wk_ref, wv_ref, wlat_ref, wg_ref, bf_ref,
                   gq_ref, wqb_ref, gkv_ref, wkbk_ref, wkbv_ref, ltri_ref, place_ref, ones_ref,
                   freq_ref,
                   qf_ref, kf_ref, vf_ref, qm_ref, km_ref, vm_ref, gate_ref,
                   carry_ref, *, tiles_per_seq, mla_scale):
    i = pl.program_id(0)

    @pl.when(i % tiles_per_seq == 0)
    def _():
        carry_ref[...] = jnp.zeros_like(carry_ref)

    h = _rms(x_ref[...], g_ref[...]).astype(BF16)
    tm = h.shape[0]
    lane = lax.broadcasted_iota(I32, (tm, LANES), 1)

    lat = _dot(h, wlat_ref[...])
    q_lat = lat[:, :Q_RANK]
    c_kv = lat[:, Q_RANK:Q_RANK + KV_RANK]
    misc = lat[:, Q_RANK + KV_RANK:]

    qn = _rms(q_lat, gq_ref[...]).astype(BF16)
    kvn = _rms(c_kv, gkv_ref[...]).astype(BF16)
    qfull = _dot(qn, wqb_ref[...])
    knope = _dot(kvn, wkbk_ref[...])
    vm_ref[0] = _dot_nt(wkbv_ref[...], kvn).astype(BF16)

    z = misc + bf_ref[...]
    logf = jnp.minimum(z, 0.0) - jnp.log1p(jnp.exp(-jnp.abs(z)))
    fmask = (lane >= FG_LO) & (lane < FG_LO + HEADS)
    logf = jnp.where(fmask, logf, 0.0)
    l_hi, l_mid, l_lo = _split3(logf)
    ltri = ltri_ref[...]
    c = _dot(ltri, l_hi) + _dot(ltri, l_mid) + _dot(ltri, l_lo) + carry_ref[...]
    carry_ref[...] = c[tm - 1:tm, :]
    c_hi, c_mid, c_lo = _split3(c * LOG2E)
    c3 = (c_hi.astype(F32) + pltpu.roll(c_mid.astype(F32), HEADS, 1)
          + pltpu.roll(c_lo.astype(F32), 2 * HEADS, 1)).astype(BF16)
    augc = _dot(c3, place_ref[...]) + ones_ref[...]
    low = lane < HEAD_DIM
    amask = (lane >= AUG_LO) & (lane < AUG_LO + 6)

    def aug_block(half, hd):
        src = augc[:, half * LANES:(half + 1) * LANES]
        return jnp.where(amask, pltpu.roll(src, (AUG_LO - 16 * hd) % LANES, 1), 0.0)

    def head_block(compact, hd):
        pair = compact[:, (hd // 2) * LANES:(hd // 2 + 1) * LANES]
        if hd % 2:
            pair = pltpu.roll(pair, HEAD_DIM, 1)
        return jnp.where(low, pair, 0.0)

    qc = _dot(h, wq_ref[...]) * LOG2E
    kc = _dot(h, wk_ref[...])
    vf_ref[0] = _dot_nt(wv_ref[...], h).astype(BF16)
    gate_ref[...] = _dot(h, wg_ref[...]).astype(BF16)

    for hd in range(HEADS):
        sl = slice(hd * LANES, (hd + 1) * LANES)
        qf_ref[:, sl] = (head_block(qc, hd) + aug_block(0, hd)).astype(BF16)
        kf_ref[:, sl] = (head_block(kc, hd) + aug_block(1, hd)).astype(BF16)

    ang = pos_ref[...].astype(F32) * freq_ref[...]
    cosv = jnp.cos(ang)
    sinv = jnp.sin(ang)
    s1 = jnp.where((lane >= PE_LO) & (lane < PE_MID), -sinv, 0.0)
    s2 = jnp.where((lane >= PE_MID) & (lane < PE_HI), sinv, 0.0)

    def rope(v):
        return v * cosv + pltpu.roll(v, LANES - 16, 1) * s1 + pltpu.roll(v, 16, 1) * s2

    kpe = jnp.where((lane >= PE_LO) & (lane < PE_HI), rope(misc), 0.0)
    for hd in range(HEADS):
        sl = slice(hd * LANES, (hd + 1) * LANES)
        qm_ref[:, sl] = (rope(qfull[:, sl]) * mla_scale).astype(BF16)
        km_ref[:, sl] = (head_block(knope, hd) + kpe).astype(BF16)


def _inproj(x2, pos2, g_attn, wq, wk, wv, wlat, wg, bf128, gq, wqb, gkv, wkbk, wkbv, ltri,
            place, ones, freq, *, seq_len):
    n, d = x2.shape
    tm = ROW_TILE
    hw = HEADS * LANES
    vw = HEADS * HEAD_DIM
    const = lambda i: (0, 0)
    row = lambda i: (i, 0)
    full = lambda a: pl.BlockSpec(a.shape, const)
    rows_out = lambda w: (jax.ShapeDtypeStruct((n, w), BF16), pl.BlockSpec((tm, w), row))
    vt_out = (jax.ShapeDtypeStruct((n // tm, vw, tm), BF16),
              pl.BlockSpec((1, vw, tm), lambda i: (i, 0, 0)))
    outs = [rows_out(hw), rows_out(hw), vt_out, rows_out(hw), rows_out(hw), vt_out,
            rows_out(wg.shape[1])]
    consts = (g_attn, wq, wk, wv, wlat, wg, bf128, gq, wqb, gkv, wkbk, wkbv, ltri, place, ones, freq)
    return pl.pallas_call(
        functools.partial(_inproj_kernel, tiles_per_seq=seq_len // tm,
                          mla_scale=float((HEAD_DIM + ROPE_DIM) ** -0.5) * LOG2E),
        out_shape=[o[0] for o in outs],
        grid=(n // tm,),
        in_specs=[pl.BlockSpec((tm, d), row), pl.BlockSpec((tm, 1), row)] + [full(a) for a in consts],
        out_specs=[o[1] for o in outs],
        scratch_shapes=[pltpu.VMEM((1, LANES), F32)],
        compiler_params=pltpu.CompilerParams(dimension_semantics=("arbitrary",),
                                             vmem_limit_bytes=VMEM_LIMIT),
        name="inproj",
    )(x2, pos2, *consts)


def _attn_kernel(q_ref, k_ref, vt_ref, o_ref, *, chunk_mask):
    i = pl.program_id(2)
    tq = q_ref.shape[0]
    tk = vt_ref.shape[2]

    def scores(j, masked, lo=0):
        start = pl.multiple_of(j * tk, tk)
        ss = [_dot_nt(k_ref[pl.ds(start, tk), hd * LANES:(hd + 1) * LANES],
                      q_ref[lo:, hd * LANES:(hd + 1) * LANES])
              for hd in range(ATT_HEADS)]
        if masked:
            keyg = lax.broadcasted_iota(I32, (tk, tq - lo), 0) + j * tk
            qlo = lax.broadcasted_iota(I32, (tk, tq - lo), 1) + (i * tq + lo)
            allowed = (keyg // CHUNK) <= (qlo // CHUNK) if chunk_mask else keyg <= qlo
            ss = [jnp.where(allowed, s, NEG) for s in ss]
        return ss

    ones = jnp.ones((2 * SUBLANES, tk), BF16)

    def update(j, ss, state, lo=0):
        vt = vt_ref[j]
        new = []
        for hd, s in enumerate(ss):
            m_all, acc_all = state[hd]
            m, acc = m_all[:, lo:], acc_all[:, lo:]
            m_new = jnp.maximum(m, jnp.max(s, axis=0, keepdims=True))
            alpha = jnp.exp2(m - m_new)
            p = jnp.exp2((s - m_new[0:1]).astype(BF16))
            va = jnp.concatenate([vt[hd * HEAD_DIM:(hd + 1) * HEAD_DIM, :], ones], axis=0)
            acc = alpha[0:1] * acc + _dot(va, p)
            if lo:
                m_new = jnp.concatenate([m_all[:, :lo], m_new], axis=1)
                acc = jnp.concatenate([acc_all[:, :lo], acc], axis=1)
            new.append((m_new, acc))
        return tuple(new)

    n_full = (i * tq) // tk
    group = max(1, tq // tk)

    def run_group(first, state, masked):
        los = [u * tk if masked else 0 for u in range(group)]
        ss = scores(first, masked, los[0])
        for u in range(group):
            nxt = scores(first + u + 1, masked, los[u + 1]) if u + 1 < group else None
            state = update(first + u, ss, state, los[u])
            ss = nxt
        return state

    init1 = (jnp.full((SUBLANES, tq), NEG, F32), jnp.zeros((HEAD_DIM + 2 * SUBLANES, tq), F32))
    state = lax.fori_loop(0, n_full // group, lambda jj, c: run_group(jj * group, c, False),
                          (init1,) * ATT_HEADS)
    state = run_group(n_full, state, True)
    out_t = jnp.concatenate([acc[:HEAD_DIM] / acc[HEAD_DIM:HEAD_DIM + 1] for _, acc in state],
                            axis=0)
    o_ref[...] = out_t.T.astype(o_ref.dtype)


def _attention(q, k, vt, *, batch, seq_len, chunk_mask):
    n = q.shape[0]
    t = ATT_TILE
    tk = vt.shape[2]
    nq = seq_len // t
    nkb = seq_len // tk
    hp = ATT_HEADS
    pairs = HEADS // hp
    return pl.pallas_call(
        functools.partial(_attn_kernel, chunk_mask=chunk_mask),
        out_shape=jax.ShapeDtypeStruct((n, HEADS * HEAD_DIM), BF16),
        grid=(batch, pairs, nq),
        in_specs=[pl.BlockSpec((t, hp * LANES), lambda b, p, i: (b * nq + i, p)),
                  pl.BlockSpec((seq_len, hp * LANES), lambda b, p, i: (b, p)),
                  pl.BlockSpec((nkb, hp * HEAD_DIM, tk), lambda b, p, i: (b, p, 0))],
        out_specs=pl.BlockSpec((t, hp * HEAD_DIM), lambda b, p, i: (b * nq + i, p)),
        compiler_params=pltpu.CompilerParams(
            dimension_semantics=("arbitrary", "arbitrary", "arbitrary"),
            vmem_limit_bytes=VMEM_LIMIT),
        name="attn_mla" if chunk_mask else "attn_fox",
    )(q, k, vt)


def _outproj_kernel(yf_ref, ym_ref, gate_ref, x_ref, bm_ref, wfo_ref, wmo_ref, wo_ref, gffn_ref,
                    wrh_ref, wrl_ref, br_ref, lstrict_ref, ustrict_ref,
                    x1_ref, route_ref, dest_ref, meta_ref, xs_hbm,
                    fill_ref, cur_ref, nfree_ref, tbl_ref,
                    hbuf, dbuf, dsm, zbuf, mbuf, msm, ssem, isem, zsem, msem):
    i = pl.program_id(0)
    nt = pl.num_programs(0)
    d = x_ref.shape[1]
    tm = x_ref.shape[0]
    sub = d // LANES
    rows = EXPERT_ROWS
    n_blocks = xs_hbm.shape[0] // (rows * sub)
    slot = i % 2

    def idx_copy(s):
        return pltpu.make_async_copy(dbuf.at[s], dsm.at[s], isem.at[s])

    def start_scatters(s):
        for static_s in range(2):
            @pl.when(s == static_s)
            def _(static_s=static_s):
                def body(t, _):
                    src = hbuf.at[static_s, pl.ds(pl.multiple_of(t * sub, sub), sub), :]
                    for k in range(TOP_K):
                        dst = dsm[static_s, k, t]
                        pltpu.make_async_copy(
                            src, xs_hbm.at[pl.ds(pl.multiple_of(dst, sub), sub), :],
                            ssem.at[static_s]).start(priority=k % 2)
                    return 0

                lax.fori_loop(0, tm, body, 0, unroll=4)

    def wait_scatters(s):
        for _ in range(TOP_K):
            pltpu.make_async_copy(hbuf.at[s], xs_hbm.at[pl.ds(0, tm * sub), :], ssem.at[s]).wait()

    @pl.when(i == 0)
    def _():
        fill_ref[...] = jnp.full_like(fill_ref, float(rows))
        cur_ref[...] = jnp.zeros_like(cur_ref)
        nfree_ref[...] = jnp.zeros_like(nfree_ref)
        tbl_ref[...] = jnp.full_like(tbl_ref, float(N_EXPERTS))
        zbuf[...] = jnp.zeros_like(zbuf)

    @pl.when(i > 0)
    def _():
        idx_copy(1 - slot).wait()
        start_scatters(1 - slot)

    a = _dot(yf_ref[...], wfo_ref[...])
    b = _dot(ym_ref[...], wmo_ref[...])
    g = 1.0 / (1.0 + jnp.exp(-(gate_ref[...].astype(F32) + bm_ref[...])))
    merged = (g[:, :d] * a + g[:, d:] * b).astype(BF16)
    x1 = x_ref[...] + _dot(merged, wo_ref[...])
    x1_ref[...] = x1
    h2 = _rms(x1, gffn_ref[...])

    hi = h2.astype(BF16)
    lo = (h2 - hi.astype(F32)).astype(BF16)
    wrh = wrh_ref[...]
    logits = _dot(hi, wrh) + _dot(lo, wrh) + _dot(hi, wrl_ref[...]) + br_ref[...]

    lane = lax.broadcasted_iota(I32, (tm, LANES), 1)
    vals = logits
    sels, tops = [], []
    for _ in range(TOP_K):
        mx = jnp.max(vals, axis=-1, keepdims=True)
        idx = jnp.min(jnp.where(vals == mx, lane, LANES), axis=-1, keepdims=True)
        sel = lane == idx
        vals = jnp.where(sel, NEG, vals)
        sels.append(sel)
        tops.append(mx)
    es = [jnp.exp(tv - tops[0]) for tv in tops]
    den = es[0] + es[1] + es[2] + es[3]

    onehot = jnp.zeros((tm, LANES), F32)
    for sel in sels:
        onehot = onehot + sel.astype(F32)
    before = _dot(lstrict_ref[...], onehot.astype(BF16))
    cnt = jnp.sum(onehot, axis=0, keepdims=True)

    fill = fill_ref[...]
    cur = cur_ref[...]
    nfree = nfree_ref[...]
    need = ((fill + cnt) > float(rows)).astype(F32)
    need8 = jnp.broadcast_to(need, (SUBLANES, LANES)).astype(BF16)
    newid = nfree + _dot(need8, ustrict_ref[...])[0:1, :]
    pos = fill + before
    dest = jnp.where(pos < float(rows), cur * rows + pos, newid * rows + pos - float(rows))
    fill_ref[...] = fill + cnt - need * float(rows)
    cur_ref[...] = jnp.where(need > 0, newid, cur)
    nfree_ref[...] = nfree + jnp.sum(need, axis=-1, keepdims=True)
    blk_id = (lax.broadcasted_iota(I32, (SUBLANES, LANES), 0) * LANES
              + lax.broadcasted_iota(I32, (SUBLANES, LANES), 1)).astype(F32)
    tbl = tbl_ref[...]
    for e in range(N_EXPERTS):
        hit = (blk_id == newid[:, e:e + 1]) & (need[:, e:e + 1] > 0)
        tbl = jnp.where(hit, float(e), tbl)
    tbl_ref[...] = tbl

    route = jnp.zeros((tm, LANES), F32)
    for k in range(TOP_K):
        dest_k = jnp.sum(jnp.where(sels[k], dest, 0.0), axis=-1, keepdims=True)
        route = jnp.where(lane == ROUTE_DEST + k, dest_k, route)
        route = jnp.where(lane == ROUTE_W + k, es[k] / den, route)
    route_ref[...] = route
    dest_t = (route.T[0:SUBLANES, :] * float(sub)).astype(I32)
    dest_ref[0] = dest_t

    @pl.when(i >= 2)
    def _():
        wait_scatters(slot)

    for s in range(sub):
        hbuf[slot, pl.ds(s, tm, stride=sub), :] = h2[:, s * LANES:(s + 1) * LANES]
    dbuf[slot] = dest_t
    idx_copy(slot).start()

    @pl.when(i == nt - 1)
    def _():
        idx_copy(slot).wait()
        start_scatters(slot)

        @pl.when(i >= 1)
        def _():
            wait_scatters(1 - slot)

        wait_scatters(slot)

        meta = jnp.concatenate([tbl_ref[...], jnp.broadcast_to(nfree_ref[...], (SUBLANES, LANES))],
                               axis=0).astype(I32)
        meta_ref[...] = meta
        state = jnp.concatenate([fill_ref[...], cur_ref[...], nfree_ref[...],
                                 jnp.zeros((SUBLANES - 3, LANES), F32)], axis=0).astype(I32)
        mbuf[...] = state
        mcopy = pltpu.make_async_copy(mbuf, msm, msem.at[0])
        mcopy.start()
        mcopy.wait()

        def zero_copy(first_row, n_rows):
            return pltpu.make_async_copy(
                zbuf.at[pl.ds(0, n_rows * sub), :],
                xs_hbm.at[pl.ds(pl.multiple_of(first_row * sub, sub), n_rows * sub), :], zsem.at[0])

        chunks = [rows >> (s + 1) for s in range(rows.bit_length() - 1)]
        plans = []
        for e in range(N_EXPERTS):
            rem = rows - msm[0, e]
            at = msm[1, e] * rows + msm[0, e]
            for c in chunks:
                take = (rem & c) != 0
                plans.append((take, zero_copy(at, c)))
                at = at + jnp.where(take, c, 0)
        for j in range(N_EXPERTS):
            blk = msm[2, 0] + j
            safe = jnp.minimum(blk, n_blocks - 1)
            plans.append((blk < n_blocks, zero_copy(safe * rows, rows)))
        for take, cp in plans:
            pl.when(take)(cp.start)
        for take, cp in plans:
            pl.when(take)(cp.wait)


def _outproj(yf, ym, gates, x2, bm, wfo, wmo, wo, gffn, wrh, wrl, br, lstrict, ustrict, *, n_blocks):
    n, d = x2.shape
    tm = MOE_TILE
    rows = EXPERT_ROWS
    sub = d // LANES
    const = lambda i: (0, 0)
    row = lambda i: (i, 0)
    full = lambda a: pl.BlockSpec(a.shape, const)
    consts = (bm, wfo, wmo, wo, gffn, wrh, wrl, br, lstrict, ustrict)
    return pl.pallas_call(
        _outproj_kernel,
        out_shape=[jax.ShapeDtypeStruct((n, d), F32),
                   jax.ShapeDtypeStruct((n, LANES), F32),
                   jax.ShapeDtypeStruct((n // tm, SUBLANES, tm), I32),
                   jax.ShapeDtypeStruct((2 * SUBLANES, LANES), I32),
                   jax.ShapeDtypeStruct((n_blocks * rows * sub, LANES), F32)],
        grid=(n // tm,),
        in_specs=[pl.BlockSpec((tm, yf.shape[1]), row), pl.BlockSpec((tm, ym.shape[1]), row),
                  pl.BlockSpec((tm, gates.shape[1]), row), pl.BlockSpec((tm, d), row)]
                 + [full(a) for a in consts],
        out_specs=[pl.BlockSpec((tm, d), row), pl.BlockSpec((tm, LANES), row),
                   pl.BlockSpec((1, SUBLANES, tm), lambda i: (i, 0, 0)),
                   pl.BlockSpec((2 * SUBLANES, LANES), const),
                   pl.BlockSpec(memory_space=pl.ANY)],
        scratch_shapes=[pltpu.VMEM((1, LANES), F32), pltpu.VMEM((1, LANES), F32),
                        pltpu.VMEM((1, LANES), F32), pltpu.VMEM((SUBLANES, LANES), F32),
                        pltpu.VMEM((2, tm * sub, LANES), F32),
                        pltpu.VMEM((2, SUBLANES, tm), I32),
                        pltpu.SMEM((2, SUBLANES, tm), I32),
                        pltpu.VMEM((rows * sub, LANES), F32),
                        pltpu.VMEM((SUBLANES, LANES), I32),
                        pltpu.SMEM((SUBLANES, LANES), I32),
                        pltpu.SemaphoreType.DMA((2,)), pltpu.SemaphoreType.DMA((2,)),
                        pltpu.SemaphoreType.DMA((1,)), pltpu.SemaphoreType.DMA((1,))],
        compiler_params=pltpu.CompilerParams(dimension_semantics=("arbitrary",),
                                             vmem_limit_bytes=VMEM_LIMIT),
        name="outproj_router",
    )(yf, ym, gates, x2, *consts)


def _expert_kernel(order_ref, be_ref, nused_ref,
                   xs_ref, wgu_ref, bgu_ref, wd_ref, bd_ref, ys_ref, wgu_b, wd_b):
    b = pl.program_id(0)
    rows = EXPERT_ROWS
    sub = xs_ref.shape[0] // rows

    @pl.when(b < nused_ref[0])
    def _():
        changed = jnp.logical_or(b == 0, be_ref[b] != be_ref[jnp.maximum(b - 1, 0)])

        @pl.when(changed)
        def _():
            wgu_b[...] = wgu_ref[0].astype(BF16)
            wd_b[...] = wd_ref[0].astype(BF16)

        x = jnp.concatenate(
            [xs_ref[pl.ds(s, rows, stride=sub), :].astype(BF16) for s in range(sub)], axis=1)
        gu = _dot(x, wgu_b[...]) + bgu_ref[0]
        gate = jnp.minimum(gu[:, :EXPERT_FF], SWIGLU_LIMIT)
        up = jnp.clip(gu[:, EXPERT_FF:], -SWIGLU_LIMIT, SWIGLU_LIMIT)
        glu = gate * (1.0 / (1.0 + jnp.exp(-SWIGLU_ALPHA * gate)))
        y = _dot(((up + 1.0) * glu).astype(BF16), wd_b[...]) + bd_ref[0]
        for s in range(sub):
            ys_ref[pl.ds(s, rows, stride=sub), :] = y[:, s * LANES:(s + 1) * LANES]

    @pl.when(b >= nused_ref[0])
    def _():
        ys_ref[...] = jnp.zeros_like(ys_ref)


def _experts(order, block_e, nused, xs, w_gu, b_gu, w_d, b_d):
    e, d, ff2 = w_gu.shape
    rows = EXPERT_ROWS
    sub = d // LANES
    n_blocks = xs.shape[0] // (rows * sub)
    wmap = lambda b, od, be, nu: (be[b], 0, 0)
    xmap = lambda b, od, be, nu: (od[b], 0)
    grid_spec = pltpu.PrefetchScalarGridSpec(
        num_scalar_prefetch=3,
        grid=(n_blocks,),
        in_specs=[pl.BlockSpec((rows * sub, LANES), xmap),
                  pl.BlockSpec((1, d, ff2), wmap), pl.BlockSpec((1, 1, ff2), wmap),
                  pl.BlockSpec((1, ff2 // 2, d), wmap), pl.BlockSpec((1, 1, d), wmap)],
        out_specs=pl.BlockSpec((rows * sub, LANES), xmap),
        scratch_shapes=[pltpu.VMEM((d, ff2), BF16), pltpu.VMEM((ff2 // 2, d), BF16)],
    )
    return pl.pallas_call(
        _expert_kernel,
        out_shape=jax.ShapeDtypeStruct(xs.shape, F32),
        grid_spec=grid_spec,
        compiler_params=pltpu.CompilerParams(dimension_semantics=("arbitrary",),
                                             vmem_limit_bytes=VMEM_LIMIT),
        name="experts",
    )(order, block_e, nused, xs, w_gu, b_gu.reshape(e, 1, ff2), w_d, b_d.reshape(e, 1, d))


def _combine_kernel(x1_ref, route_ref, dest_hbm, ys_hbm, g_ref, o_ref, ybuf, dsm, gsem, isem):
    i = pl.program_id(0)
    nt = pl.num_programs(0)
    tm, d = x1_ref.shape
    sub = d // LANES
    slot = i % 2

    def idx_copy(tile):
        s = tile % 2
        return pltpu.make_async_copy(dest_hbm.at[tile], dsm.at[s], isem.at[s])

    def start_gathers(tile):
        for static_s in range(2):
            @pl.when(tile % 2 == static_s)
            def _(static_s=static_s):
                def body(t, _):
                    for k in range(TOP_K):
                        src = dsm[static_s, k, t]
                        pltpu.make_async_copy(
                            ys_hbm.at[pl.ds(pl.multiple_of(src, sub), sub), :],
                            ybuf.at[static_s, pl.ds(pl.multiple_of((k * tm + t) * sub, sub), sub), :],
                            gsem.at[static_s]).start(priority=k % 2)
                    return 0

                lax.fori_loop(0, tm, body, 0, unroll=4)

    def wait_gathers(s):
        pltpu.make_async_copy(ys_hbm.at[pl.ds(0, TOP_K * tm * sub), :], ybuf.at[s], gsem.at[s]).wait()

    @pl.when(i == 0)
    def _():
        idx_copy(0).start()
        idx_copy(0).wait()
        start_gathers(0)

        @pl.when(nt > 1)
        def _():
            idx_copy(1).start()

    @pl.when(i + 1 < nt)
    def _():
        idx_copy(i + 1).wait()
        start_gathers(i + 1)

    @pl.when(i + 2 < nt)
    def _():
        idx_copy(i + 2).start()

    wait_gathers(slot)
    route = route_ref[...]
    ws = [route[:, ROUTE_W + k:ROUTE_W + k + 1] for k in range(TOP_K)]
    chunks = []
    ssq = jnp.zeros((tm, 1), F32)
    for s in range(sub):
        acc = x1_ref[:, s * LANES:(s + 1) * LANES]
        for k in range(TOP_K):
            acc = acc + ws[k] * ybuf[slot, pl.ds(k * tm * sub + s, tm, stride=sub), :]
        chunks.append(acc)
        ssq = ssq + jnp.sum(acc * acc, axis=-1, keepdims=True)
    inv = lax.rsqrt(ssq / d + NORM_EPS)
    for s in range(sub):
        sl = slice(s * LANES, (s + 1) * LANES)
        o_ref[:, sl] = chunks[s] * inv * g_ref[:, sl]


def _combine(x1, route, dest_t, ys, g_final):
    n, d = x1.shape
    tm = MOE_TILE
    sub = d // LANES
    row = lambda i: (i, 0)
    return pl.pallas_call(
        _combine_kernel,
        out_shape=jax.ShapeDtypeStruct((n, d), F32),
        grid=(n // tm,),
        in_specs=[pl.BlockSpec((tm, d), row), pl.BlockSpec((tm, LANES), row),
                  pl.BlockSpec(memory_space=pl.ANY), pl.BlockSpec(memory_space=pl.ANY),
                  pl.BlockSpec((1, d), lambda i: (0, 0))],
        out_specs=pl.BlockSpec((tm, d), row),
        scratch_shapes=[pltpu.VMEM((2, TOP_K * tm * sub, LANES), F32),
                        pltpu.SMEM((2, SUBLANES, tm), I32),
                        pltpu.SemaphoreType.DMA((2,)), pltpu.SemaphoreType.DMA((2,))],
        compiler_params=pltpu.CompilerParams(dimension_semantics=("arbitrary",),
                                             vmem_limit_bytes=VMEM_LIMIT),
        name="combine_norm",
    )(x1, route, dest_t, ys, g_final)


def _pad_heads(w, per_head, width=LANES):
    k = w.shape[0]
    w = w.reshape(k, HEADS, per_head)
    w = jnp.pad(w, ((0, 0), (0, 0), (0, width - per_head)))
    return w.reshape(k, HEADS * width)


def _aug_constants():
    place = np.zeros((LANES, 2 * LANES), np.float32)
    ones = np.zeros((1, 2 * LANES), np.float32)
    for hd in range(HEADS):
        for piece in range(3):
            src = FG_LO + piece * HEADS + hd
            place[src, 16 * hd + piece] = 1.0
            place[src, LANES + 16 * hd + 3 + piece] = -1.0
            ones[0, 16 * hd + 3 + piece] = 1.0
            ones[0, LANES + 16 * hd + piece] = 1.0
    return jnp.asarray(place, BF16), jnp.asarray(ones, F32)


def _layer(x2, pos2, batch, seq_len, g_attn_norm, w_in, b_fgate, g_q_a, w_q_b, g_kv_a, w_kv_b,
           w_fox_out, w_mla_out, b_merge, w_o, g_ffn_norm, w_router, b_router, w_gu, b_gu,
           w_down, b_down, g_out):
    n, d = x2.shape
    fw = HEADS * HEAD_DIM
    o = 0
    w_qf = w_in[:, o:o + fw]; o += fw
    w_kf = w_in[:, o:o + fw]; o += fw
    w_vf = w_in[:, o:o + fw]; o += fw
    w_f = w_in[:, o:o + HEADS]; o += HEADS
    w_ql = w_in[:, o:o + Q_RANK]; o += Q_RANK
    w_ckv = w_in[:, o:o + KV_RANK]; o += KV_RANK
    w_kpe = w_in[:, o:o + ROPE_DIM]; o += ROPE_DIM
    w_gate = w_in[:, o:]

    wq = (w_qf * (HEAD_DIM ** -0.5)).astype(BF16)
    wk = w_kf.astype(BF16)
    wmisc = jnp.concatenate([jnp.zeros((d, PE_LO), F32), w_kpe, w_f,
                             jnp.zeros((d, LANES - FG_LO - HEADS), F32)], axis=1)
    wlat = jnp.concatenate([w_ql, w_ckv, wmisc], axis=1).astype(BF16)
    bf128 = jnp.zeros((1, LANES), F32).at[0, FG_LO:FG_LO + HEADS].set(b_fgate)
    wqb = _pad_heads(w_q_b, HEAD_DIM + ROPE_DIM).astype(BF16)
    wkv = w_kv_b.reshape(KV_RANK, HEADS, 2 * HEAD_DIM)
    wkbk = wkv[:, :, :HEAD_DIM].reshape(KV_RANK, fw).astype(BF16)
    wkbv = wkv[:, :, HEAD_DIM:].reshape(KV_RANK, fw).T.astype(BF16)
    ltri = jnp.asarray(np.tril(np.ones((ROW_TILE, ROW_TILE), np.float32)), BF16)
    lstrict = jnp.asarray(np.tril(np.ones((MOE_TILE, MOE_TILE), np.float32), -1), BF16)
    ustrict = jnp.asarray(np.triu(np.ones((LANES, LANES), np.float32), 1), BF16)
    place, ones = _aug_constants()
    half = ROPE_DIM // 2
    inv_freq = ROPE_THETA ** (-jnp.arange(half, dtype=F32) / half)
    freq = jnp.zeros((1, LANES), F32).at[0, PE_LO:PE_MID].set(inv_freq).at[0, PE_MID:PE_HI].set(inv_freq)

    qf, kf, vf, qm, km, vm, gates = _inproj(
        x2, pos2, g_attn_norm.reshape(1, d), wq, wk, w_vf.T.astype(BF16), wlat, w_gate.astype(BF16),
        bf128, g_q_a.reshape(1, -1), wqb, g_kv_a.reshape(1, -1), wkbk, wkbv, ltri, place, ones,
        freq, seq_len=seq_len)

    y_fox = _attention(qf, kf, vf, batch=batch, seq_len=seq_len, chunk_mask=False)
    y_mla = _attention(qm, km, vm, batch=batch, seq_len=seq_len, chunk_mask=True)

    wr = jnp.pad(w_router, ((0, 0), (0, LANES - N_EXPERTS)))
    wrh = wr.astype(BF16)
    wrl = (wr - wrh.astype(F32)).astype(BF16)
    br = jnp.full((1, LANES), NEG, F32).at[0, :N_EXPERTS].set(b_router)
    n_blocks = n * TOP_K // EXPERT_ROWS + N_EXPERTS
    x1, route, dest_t, meta, xs = _outproj(
        y_fox, y_mla, gates, x2, b_merge.reshape(1, -1), w_fox_out.astype(BF16),
        w_mla_out.astype(BF16), w_o.astype(BF16), g_ffn_norm.reshape(1, d), wrh, wrl, br, lstrict,
        ustrict, n_blocks=n_blocks)

    block_e = meta[:SUBLANES].reshape(-1)[:n_blocks]
    order = jnp.argsort(block_e, stable=True).astype(I32)
    nused = meta[SUBLANES, 0:1]
    be_sorted = jnp.minimum(block_e[order], N_EXPERTS - 1).astype(I32)
    be_sorted = jnp.where(jnp.arange(n_blocks) < nused[0], be_sorted,
                          be_sorted[jnp.maximum(nused[0] - 1, 0)])
    ys = _experts(order, be_sorted, nused, xs, w_gu, b_gu, w_down, b_down)
    return _combine(x1, route, dest_t, ys, g_out.reshape(1, d))


def kernel(x, positions, g_attn_norm, w_in, b_fgate, g_q_a, w_q_b, g_kv_a, w_kv_b, w_fox_out, w_mla_out, b_merge, w_o, g_ffn_norm, w_router, b_router, w_gu, b_gu, w_down, b_down, g_final):
    batch, seq_len, d = x.shape
    depth = w_in.shape[0]
    assert depth == 1, "the fused combine + final-norm kernel assumes a single layer"
    assert seq_len % ATT_TILE == 0 and d % LANES == 0
    assert (batch * seq_len * TOP_K) % EXPERT_ROWS == 0
    assert batch * seq_len * TOP_K // EXPERT_ROWS + N_EXPERTS <= SUBLANES * LANES
    x2 = x.reshape(batch * seq_len, d)
    pos2 = positions.reshape(batch * seq_len, 1).astype(I32)
    out = _layer(x2, pos2, batch, seq_len, g_attn_norm[0], w_in[0], b_fgate[0], g_q_a[0], w_q_b[0],
                 g_kv_a[0], w_kv_b[0], w_fox_out[0], w_mla_out[0], b_merge[0], w_o[0],
                 g_ffn_norm[0], w_router[0], b_router[0], w_gu[0], b_gu[0], w_down[0], b_down[0],
                 g_final)
    return out.reshape(batch, seq_len, d)
```

```python
import functools

import jax
import jax.numpy as jnp
import numpy as np
from jax import lax
from jax.experimental import pallas as pl
from jax.experimental.pallas import tpu as pltpu

F32 = jnp.float32
BF16 = jnp.bfloat16
I32 = jnp.int32

LANES = 128
SUBLANES = 8
VMEM_LIMIT = 56 * 1024 * 1024

NORM_EPS = 1e-6
HEADS = 8
HEAD_DIM = 64
ROPE_DIM = 32
Q_RANK = 256
KV_RANK = 128
N_EXPERTS = 32
TOP_K = 4
EXPERT_FF = 1024
SWIGLU_LIMIT = 7.0
SWIGLU_ALPHA = 1.702
ROPE_THETA = 10000.0
CHUNK = 64

NEG = -1e30
LOG2E = 1.4426950408889634

ROW_TILE = 256
MOE_TILE = 512
ATT_TILE = 512
ATT_HEADS = 2
ATT_GROUP = 4
EXPERT_ROWS = MOE_TILE

PE_LO, PE_MID, PE_HI = 64, 80, 96
FG_LO = 96
AUG_LO = 64
ROUTE_DEST = 0
ROUTE_W = 8


def _dot(a, b):
    return jnp.dot(a, b, preferred_element_type=F32)


def _dot_nt(a, b):
    return lax.dot_general(a, b, (((1,), (1,)), ((), ())), preferred_element_type=F32)


def _split3(a):
    hi = a.astype(BF16)
    r1 = a - hi.astype(F32)
    mid = r1.astype(BF16)
    lo = (r1 - mid.astype(F32)).astype(BF16)
    return hi, mid, lo


def _rms(x, g):
    return x * lax.rsqrt(jnp.mean(x * x, axis=-1, keepdims=True) + NORM_EPS) * g


def _inproj_kernel(x_ref, pos_ref, g_ref, wq_ref, wk_ref, wv_ref, wlat_ref, wg_ref, bf_ref,
                   gq_ref, wqb_ref, gkv_ref, wkbk_ref, wkbv_ref, ltri_ref, place_ref, ones_ref,
                   freq_ref,
                   qf_ref, kf_ref, vf_ref, qm_ref, km_ref, vm_ref, gate_ref,
                   carry_ref, *, tiles_per_seq, mla_scale):
    i = pl.program_id(0)

    @pl.when(i % tiles_per_seq == 0)
    def _():
        carry_ref[...] = jnp.zeros_like(carry_ref)

    h = _rms(x_ref[...], g_ref[...]).astype(BF16)
    tm = h.shape[0]
    lane = lax.broadcasted_iota(I32, (tm, LANES), 1)

    lat = _dot(h, wlat_ref[...])
    q_lat = lat[:, :Q_RANK]
    c_kv = lat[:, Q_RANK:Q_RANK + KV_RANK]
    misc = lat[:, Q_RANK + KV_RANK:]

    qn = _rms(q_lat, gq_ref[...]).astype(BF16)
    kvn = _rms(c_kv, gkv_ref[...]).astype(BF16)
    qfull = _dot(qn, wqb_ref[...])
    knope = _dot(kvn, wkbk_ref[...])
    vm_ref[0] = _dot_nt(wkbv_ref[...], kvn).astype(BF16)

    z = misc + bf_ref[...]
    logf = jnp.minimum(z, 0.0) - jnp.log1p(jnp.exp(-jnp.abs(z)))
    fmask = (lane >= FG_LO) & (lane < FG_LO + HEADS)
    logf = jnp.where(fmask, logf, 0.0)
    l_hi, l_mid, l_lo = _split3(logf)
    ltri = ltri_ref[...]
    c = _dot(ltri, l_hi) + _dot(ltri, l_mid) + _dot(ltri, l_lo) + carry_ref[...]
    carry_ref[...] = c[tm - 1:tm, :]
    c_hi, c_mid, c_lo = _split3(c * LOG2E)
    c3 = (c_hi.astype(F32) + pltpu.roll(c_mid.astype(F32), HEADS, 1)
          + pltpu.roll(c_lo.astype(F32), 2 * HEADS, 1)).astype(BF16)
    augc = _dot(c3, place_ref[...]) + ones_ref[...]
    low = lane < HEAD_DIM
    amask = (lane >= AUG_LO) & (lane < AUG_LO + 6)

    def aug_block(half, hd):
        src = augc[:, half * LANES:(half + 1) * LANES]
        return jnp.where(amask, pltpu.roll(src, (AUG_LO - 16 * hd) % LANES, 1), 0.0)

    def head_block(compact, hd):
        pair = compact[:, (hd // 2) * LANES:(hd // 2 + 1) * LANES]
        if hd % 2:
            pair = pltpu.roll(pair, HEAD_DIM, 1)
        return jnp.where(low, pair, 0.0)

    qc = _dot(h, wq_ref[...]) * LOG2E
    kc = _dot(h, wk_ref[...])
    vf_ref[0] = _dot_nt(wv_ref[...], h).astype(BF16)
    gate_ref[...] = _dot(h, wg_ref[...]).astype(BF16)

    for hd in range(HEADS):
        sl = slice(hd * LANES, (hd + 1) * LANES)
        qf_ref[:, sl] = (head_block(qc, hd) + aug_block(0, hd)).astype(BF16)
        kf_ref[:, sl] = (head_block(kc, hd) + aug_block(1, hd)).astype(BF16)

    ang = pos_ref[...].astype(F32) * freq_ref[...]
    cosv = jnp.cos(ang)
    sinv = jnp.sin(ang)
    s1 = jnp.where((lane >= PE_LO) & (lane < PE_MID), -sinv, 0.0)
    s2 = jnp.where((lane >= PE_MID) & (lane < PE_HI), sinv, 0.0)

    def rope(v):
        return v * cosv + pltpu.roll(v, LANES - 16, 1) * s1 + pltpu.roll(v, 16, 1) * s2

    kpe = jnp.where((lane >= PE_LO) & (lane < PE_HI), rope(misc), 0.0)
    for hd in range(HEADS):
        sl = slice(hd * LANES, (hd + 1) * LANES)
        qm_ref[:, sl] = (rope(qfull[:, sl]) * mla_scale).astype(BF16)
        km_ref[:, sl] = (head_block(knope, hd) + kpe).astype(BF16)


def _inproj(x2, pos2, g_attn, wq, wk, wv, wlat, wg, bf128, gq, wqb, gkv, wkbk, wkbv, ltri,
            place, ones, freq, *, seq_len):
    n, d = x2.shape
    tm = ROW_TILE
    hw = HEADS * LANES
    vw = HEADS * HEAD_DIM
    const = lambda i: (0, 0)
    row = lambda i: (i, 0)
    full = lambda a: pl.BlockSpec(a.shape, const)
    rows_out = lambda w: (jax.ShapeDtypeStruct((n, w), BF16), pl.BlockSpec((tm, w), row))
    vt_out = (jax.ShapeDtypeStruct((n // tm, vw, tm), BF16),
              pl.BlockSpec((1, vw, tm), lambda i: (i, 0, 0)))
    outs = [rows_out(hw), rows_out(hw), vt_out, rows_out(hw), rows_out(hw), vt_out,
            rows_out(wg.shape[1])]
    consts = (g_attn, wq, wk, wv, wlat, wg, bf128, gq, wqb, gkv, wkbk, wkbv, ltri, place, ones, freq)
    return pl.pallas_call(
        functools.partial(_inproj_kernel, tiles_per_seq=seq_len // tm,
                          mla_scale=float((HEAD_DIM + ROPE_DIM) ** -0.5) * LOG2E),
        out_shape=[o[0] for o in outs],
        grid=(n // tm,),
        in_specs=[pl.BlockSpec((tm, d), row), pl.BlockSpec((tm, 1), row)] + [full(a) for a in consts],
        out_specs=[o[1] for o in outs],
        scratch_shapes=[pltpu.VMEM((1, LANES), F32)],
        compiler_params=pltpu.CompilerParams(dimension_semantics=("arbitrary",),
                                             vmem_limit_bytes=VMEM_LIMIT),
        name="inproj",
    )(x2, pos2, *consts)


def _attn_kernel(q_ref, k_ref, vt_ref, o_ref, *, chunk_mask):
    i = pl.program_id(2)
    tq = q_ref.shape[0]
    tk = vt_ref.shape[2]

    def scores(j, masked, lo=0):
        start = pl.multiple_of(j * tk, tk)
        ss = [_dot_nt(k_ref[pl.ds(start, tk), hd * LANES:(hd + 1) * LANES],
                      q_ref[lo:, hd * LANES:(hd + 1) * LANES])
              for hd in range(ATT_HEADS)]
        if masked:
            keyg = lax.broadcasted_iota(I32, (tk, tq - lo), 0) + j * tk
            qlo = lax.broadcasted_iota(I32, (tk, tq - lo), 1) + (i * tq + lo)
            allowed = (keyg // CHUNK) <= (qlo // CHUNK) if chunk_mask else keyg <= qlo
            ss = [jnp.where(allowed, s, NEG) for s in ss]
        return ss

    ones = jnp.ones((2 * SUBLANES, tk), BF16)

    def update(j, ss, state, lo=0):
        vt = vt_ref[j]
        new = []
        for hd, s in enumerate(ss):
            m_all, acc_all = state[hd]
            m, acc = m_all[:, lo:], acc_all[:, lo:]
            m_new = jnp.maximum(m, jnp.max(s, axis=0, keepdims=True))
            alpha = jnp.exp2(m - m_new)
            p = jnp.exp2((s - m_new[0:1]).astype(BF16))
            va = jnp.concatenate([vt[hd * HEAD_DIM:(hd + 1) * HEAD_DIM, :], ones], axis=0)
            acc = alpha[0:1] * acc + _dot(va, p)
            if lo:
                m_new = jnp.concatenate([m_all[:, :lo], m_new], axis=1)
                acc = jnp.concatenate([acc_all[:, :lo], acc], axis=1)
            new.append((m_new, acc))
        return tuple(new)

    n_full = (i * tq) // tk
    group = max(1, tq // tk)

    def run_group(first, state, masked, nblk=group):
        los = [u * tk if masked else 0 for u in range(nblk)]
        ss = scores(first, masked, los[0])
        for u in range(nblk):
            nxt = scores(first + u + 1, masked, los[u + 1]) if u + 1 < nblk else None
            state = update(first + u, ss, state, los[u])
            ss = nxt
        return state

    long = ATT_GROUP * group
    n_long = n_full // long
    init1 = (jnp.full((SUBLANES, tq), NEG, F32), jnp.zeros((HEAD_DIM + 2 * SUBLANES, tq), F32))
    state = lax.fori_loop(0, n_long, lambda jj, c: run_group(jj * long, c, False, long),
                          (init1,) * ATT_HEADS)
    done = n_long * long
    size = long // 2
    while size >= group:
        take = (n_full - done) >= size
        state = lax.cond(take, lambda st, d=done, sz=size: run_group(d, st, False, sz),
                         lambda st: st, state)
        done = done + jnp.where(take, size, 0)
        size //= 2
    state = run_group(n_full, state, True)
    out_t = jnp.concatenate([acc[:HEAD_DIM] / acc[HEAD_DIM:HEAD_DIM + 1] for _, acc in state],
                            axis=0)
    o_ref[...] = out_t.T.astype(o_ref.dtype)


def _attention(q, k, vt, *, batch, seq_len, chunk_mask):
    n = q.shape[0]
    t = ATT_TILE
    tk = vt.shape[2]
    nq = seq_len // t
    nkb = seq_len // tk
    hp = ATT_HEADS
    pairs = HEADS // hp
    return pl.pallas_call(
        functools.partial(_attn_kernel, chunk_mask=chunk_mask),
        out_shape=jax.ShapeDtypeStruct((n, HEADS * HEAD_DIM), BF16),
        grid=(batch, pairs, nq),
        in_specs=[pl.BlockSpec((t, hp * LANES), lambda b, p, i: (b * nq + i, p)),
                  pl.BlockSpec((seq_len, hp * LANES), lambda b, p, i: (b, p)),
                  pl.BlockSpec((nkb, hp * HEAD_DIM, tk), lambda b, p, i: (b, p, 0))],
        out_specs=pl.BlockSpec((t, hp * HEAD_DIM), lambda b, p, i: (b * nq + i, p)),
        compiler_params=pltpu.CompilerParams(
            dimension_semantics=("arbitrary", "arbitrary", "arbitrary"),
            vmem_limit_bytes=VMEM_LIMIT),
        name="attn_mla" if chunk_mask else "attn_fox",
    )(q, k, vt)


def _outproj_kernel(yf_ref, ym_ref, gate_ref, x_ref, bm_ref, wfo_ref, wmo_ref, wo_ref, gffn_ref,
                    wrh_ref, wrl_ref, br_ref, lstrict_ref, ustrict_ref,
                    x1_ref, route_ref, dest_ref, meta_ref, xs_hbm,
                    fill_ref, cur_ref, nfree_ref, tbl_ref,
                    hbuf, dbuf, dsm, zbuf, mbuf, msm, ssem, isem, zsem, msem):
    i = pl.program_id(0)
    nt = pl.num_programs(0)
    d = x_ref.shape[1]
    tm = x_ref.shape[0]
    sub = d // LANES
    rows = EXPERT_ROWS
    n_blocks = xs_hbm.shape[0] // (rows * sub)
    slot = i % 2

    def idx_copy(s):
        return pltpu.make_async_copy(dbuf.at[s], dsm.at[s], isem.at[s])

    def start_scatters(s):
        for static_s in range(2):
            @pl.when(s == static_s)
            def _(static_s=static_s):
                def body(t, _):
                    src = hbuf.at[static_s, pl.ds(pl.multiple_of(t * sub, sub), sub), :]
                    for k in range(TOP_K):
                        dst = dsm[static_s, k, t]
                        pltpu.make_async_copy(
                            src, xs_hbm.at[pl.ds(pl.multiple_of(dst, sub), sub), :],
                            ssem.at[static_s]).start(priority=k % 2)
                    return 0

                lax.fori_loop(0, tm, body, 0, unroll=4)

    def wait_scatters(s):
        for _ in range(TOP_K):
            pltpu.make_async_copy(hbuf.at[s], xs_hbm.at[pl.ds(0, tm * sub), :], ssem.at[s]).wait()

    @pl.when(i == 0)
    def _():
        fill_ref[...] = jnp.full_like(fill_ref, float(rows))
        cur_ref[...] = jnp.zeros_like(cur_ref)
        nfree_ref[...] = jnp.zeros_like(nfree_ref)
        tbl_ref[...] = jnp.full_like(tbl_ref, float(N_EXPERTS))
        zbuf[...] = jnp.zeros_like(zbuf)

    @pl.when(i > 0)
    def _():
        idx_copy(1 - slot).wait()
        start_scatters(1 - slot)

    a = _dot(yf_ref[...], wfo_ref[...])
    b = _dot(ym_ref[...], wmo_ref[...])
    g = 1.0 / (1.0 + jnp.exp(-(gate_ref[...].astype(F32) + bm_ref[...])))
    merged = (g[:, :d] * a + g[:, d:] * b).astype(BF16)
    x1 = x_ref[...] + _dot(merged, wo_ref[...])
    x1_ref[...] = x1
    h2 = _rms(x1, gffn_ref[...])

    hi = h2.astype(BF16)
    lo = (h2 - hi.astype(F32)).astype(BF16)
    wrh = wrh_ref[...]
    logits = _dot(hi, wrh) + _dot(lo, wrh) + _dot(hi, wrl_ref[...]) + br_ref[...]

    lane = lax.broadcasted_iota(I32, (tm, LANES), 1)
    vals = logits
    sels, tops = [], []
    for _ in range(TOP_K):
        mx = jnp.max(vals, axis=-1, keepdims=True)
        idx = jnp.min(jnp.where(vals == mx, lane, LANES), axis=-1, keepdims=True)
        sel = lane == idx
        vals = jnp.where(sel, NEG, vals)
        sels.append(sel)
        tops.append(mx)
    es = [jnp.exp(tv - tops[0]) for tv in tops]
    den = es[0] + es[1] + es[2] + es[3]

    onehot = jnp.zeros((tm, LANES), F32)
    for sel in sels:
        onehot = onehot + sel.astype(F32)
    before = _dot(lstrict_ref[...], onehot.astype(BF16))
    cnt = jnp.sum(onehot, axis=0, keepdims=True)

    fill = fill_ref[...]
    cur = cur_ref[...]
    nfree = nfree_ref[...]
    need = ((fill + cnt) > float(rows)).astype(F32)
    need8 = jnp.broadcast_to(need, (SUBLANES, LANES)).astype(BF16)
    newid = nfree + _dot(need8, ustrict_ref[...])[0:1, :]
    pos = fill + before
    dest = jnp.where(pos < float(rows), cur * rows + pos, newid * rows + pos - float(rows))
    fill_ref[...] = fill + cnt - need * float(rows)
    cur_ref[...] = jnp.where(need > 0, newid, cur)
    nfree_ref[...] = nfree + jnp.sum(need, axis=-1, keepdims=True)
    blk_id = (lax.broadcasted_iota(I32, (SUBLANES, LANES), 0) * LANES
              + lax.broadcasted_iota(I32, (SUBLANES, LANES), 1)).astype(F32)
    tbl = tbl_ref[...]
    for e in range(N_EXPERTS):
        hit = (blk_id == newid[:, e:e + 1]) & (need[:, e:e + 1] > 0)
        tbl = jnp.where(hit, float(e), tbl)
    tbl_ref[...] = tbl

    route = jnp.zeros((tm, LANES), F32)
    for k in range(TOP_K):
        dest_k = jnp.sum(jnp.where(sels[k], dest, 0.0), axis=-1, keepdims=True)
        route = jnp.where(lane == ROUTE_DEST + k, dest_k, route)
        route = jnp.where(lane == ROUTE_W + k, es[k] / den, route)
    route_ref[...] = route
    dest_t = (route.T[0:SUBLANES, :] * float(sub)).astype(I32)
    dest_ref[0] = dest_t

    @pl.when(i >= 2)
    def _():
        wait_scatters(slot)

    for s in range(sub):
        hbuf[slot, pl.ds(s, tm, stride=sub), :] = h2[:, s * LANES:(s + 1) * LANES]
    dbuf[slot] = dest_t
    idx_copy(slot).start()

    @pl.when(i == nt - 1)
    def _():
        idx_copy(slot).wait()
        start_scatters(slot)

        @pl.when(i >= 1)
        def _():
            wait_scatters(1 - slot)

        wait_scatters(slot)

        meta = jnp.concatenate([tbl_ref[...], jnp.broadcast_to(nfree_ref[...], (SUBLANES, LANES))],
                               axis=0).astype(I32)
        meta_ref[...] = meta
        state = jnp.concatenate([fill_ref[...], cur_ref[...], nfree_ref[...],
                                 jnp.zeros((SUBLANES - 3, LANES), F32)], axis=0).astype(I32)
        mbuf[...] = state
        mcopy = pltpu.make_async_copy(mbuf, msm, msem.at[0])
        mcopy.start()
        mcopy.wait()

        def zero_copy(first_row, n_rows):
            return pltpu.make_async_copy(
                zbuf.at[pl.ds(0, n_rows * sub), :],
                xs_hbm.at[pl.ds(pl.multiple_of(first_row * sub, sub), n_rows * sub), :], zsem.at[0])

        chunks = [rows >> (s + 1) for s in range(rows.bit_length() - 1)]
        plans = []
        for e in range(N_EXPERTS):
            rem = rows - msm[0, e]
            at = msm[1, e] * rows + msm[0, e]
            for c in chunks:
                take = (rem & c) != 0
                plans.append((take, zero_copy(at, c)))
                at = at + jnp.where(take, c, 0)
        for j in range(N_EXPERTS):
            blk = msm[2, 0] + j
            safe = jnp.minimum(blk, n_blocks - 1)
            plans.append((blk < n_blocks, zero_copy(safe * rows, rows)))
        for take, cp in plans:
            pl.when(take)(cp.start)
        for take, cp in plans:
            pl.when(take)(cp.wait)


def _outproj(yf, ym, gates, x2, bm, wfo, wmo, wo, gffn, wrh, wrl, br, lstrict, ustrict, *, n_blocks):
    n, d = x2.shape
    tm = MOE_TILE
    rows = EXPERT_ROWS
    sub = d // LANES
    const = lambda i: (0, 0)
    row = lambda i: (i, 0)
    full = lambda a: pl.BlockSpec(a.shape, const)
    consts = (bm, wfo, wmo, wo, gffn, wrh, wrl, br, lstrict, ustrict)
    return pl.pallas_call(
        _outproj_kernel,
        out_shape=[jax.ShapeDtypeStruct((n, d), F32),
                   jax.ShapeDtypeStruct((n, LANES), F32),
                   jax.ShapeDtypeStruct((n // tm, SUBLANES, tm), I32),
                   jax.ShapeDtypeStruct((2 * SUBLANES, LANES), I32),
                   jax.ShapeDtypeStruct((n_blocks * rows * sub, LANES), F32)],
        grid=(n // tm,),
        in_specs=[pl.BlockSpec((tm, yf.shape[1]), row), pl.BlockSpec((tm, ym.shape[1]), row),
                  pl.BlockSpec((tm, gates.shape[1]), row), pl.BlockSpec((tm, d), row)]
                 + [full(a) for a in consts],
        out_specs=[pl.BlockSpec((tm, d), row), pl.BlockSpec((tm, LANES), row),
                   pl.BlockSpec((1, SUBLANES, tm), lambda i: (i, 0, 0)),
                   pl.BlockSpec((2 * SUBLANES, LANES), const),
                   pl.BlockSpec(memory_space=pl.ANY)],
        scratch_shapes=[pltpu.VMEM((1, LANES), F32), pltpu.VMEM((1, LANES), F32),
                        pltpu.VMEM((1, LANES), F32), pltpu.VMEM((SUBLANES, LANES), F32),
                        pltpu.VMEM((2, tm * sub, LANES), F32),
                        pltpu.VMEM((2, SUBLANES, tm), I32),
                        pltpu.SMEM((2, SUBLANES, tm), I32),
                        pltpu.VMEM((rows * sub, LANES), F32),
                        pltpu.VMEM((SUBLANES, LANES), I32),
                        pltpu.SMEM((SUBLANES, LANES), I32),
                        pltpu.SemaphoreType.DMA((2,)), pltpu.SemaphoreType.DMA((2,)),
                        pltpu.SemaphoreType.DMA((1,)), pltpu.SemaphoreType.DMA((1,))],
        compiler_params=pltpu.CompilerParams(dimension_semantics=("arbitrary",),
                                             vmem_limit_bytes=VMEM_LIMIT),
        name="outproj_router",
    )(yf, ym, gates, x2, *consts)


def _expert_kernel(order_ref, be_ref, nused_ref,
                   xs_ref, wgu_ref, bgu_ref, wd_ref, bd_ref, ys_ref, wgu_b, wd_b):
    b = pl.program_id(0)
    rows = EXPERT_ROWS
    sub = xs_ref.shape[0] // rows

    @pl.when(b < nused_ref[0])
    def _():
        changed = jnp.logical_or(b == 0, be_ref[b] != be_ref[jnp.maximum(b - 1, 0)])

        @pl.when(changed)
        def _():
            wgu_b[...] = wgu_ref[0].astype(BF16)
            wd_b[...] = wd_ref[0].astype(BF16)

        x = jnp.concatenate(
            [xs_ref[pl.ds(s, rows, stride=sub), :].astype(BF16) for s in range(sub)], axis=1)
        gu = _dot(x, wgu_b[...]) + bgu_ref[0]
        gate = jnp.minimum(gu[:, :EXPERT_FF], SWIGLU_LIMIT)
        up = jnp.clip(gu[:, EXPERT_FF:], -SWIGLU_LIMIT, SWIGLU_LIMIT)
        glu = gate * (1.0 / (1.0 + jnp.exp(-SWIGLU_ALPHA * gate)))
        y = _dot(((up + 1.0) * glu).astype(BF16), wd_b[...]) + bd_ref[0]
        for s in range(sub):
            ys_ref[pl.ds(s, rows, stride=sub), :] = y[:, s * LANES:(s + 1) * LANES]

    @pl.when(b >= nused_ref[0])
    def _():
        ys_ref[...] = jnp.zeros_like(ys_ref)


def _experts(order, block_e, nused, xs, w_gu, b_gu, w_d, b_d):
    e, d, ff2 = w_gu.shape
    rows = EXPERT_ROWS
    sub = d // LANES
    n_blocks = xs.shape[0] // (rows * sub)
    wmap = lambda b, od, be, nu: (be[b], 0, 0)
    xmap = lambda b, od, be, nu: (od[b], 0)
    grid_spec = pltpu.PrefetchScalarGridSpec(
        num_scalar_prefetch=3,
        grid=(n_blocks,),
        in_specs=[pl.BlockSpec((rows * sub, LANES), xmap),
                  pl.BlockSpec((1, d, ff2), wmap), pl.BlockSpec((1, 1, ff2), wmap),
                  pl.BlockSpec((1, ff2 // 2, d), wmap), pl.BlockSpec((1, 1, d), wmap)],
        out_specs=pl.BlockSpec((rows * sub, LANES), xmap),
        scratch_shapes=[pltpu.VMEM((d, ff2), BF16), pltpu.VMEM((ff2 // 2, d), BF16)],
    )
    return pl.pallas_call(
        _expert_kernel,
        out_shape=jax.ShapeDtypeStruct(xs.shape, F32),
        grid_spec=grid_spec,
        compiler_params=pltpu.CompilerParams(dimension_semantics=("arbitrary",),
                                             vmem_limit_bytes=VMEM_LIMIT),
        name="experts",
    )(order, block_e, nused, xs, w_gu, b_gu.reshape(e, 1, ff2), w_d, b_d.reshape(e, 1, d))


def _combine_kernel(x1_ref, route_ref, dest_hbm, ys_hbm, g_ref, o_ref, ybuf, dsm, gsem, isem):
    i = pl.program_id(0)
    nt = pl.num_programs(0)
    tm, d = x1_ref.shape
    sub = d // LANES
    slot = i % 2

    def idx_copy(tile):
        s = tile % 2
        return pltpu.make_async_copy(dest_hbm.at[tile], dsm.at[s], isem.at[s])

    def start_gathers(tile):
        for static_s in range(2):
            @pl.when(tile % 2 == static_s)
            def _(static_s=static_s):
                def body(t, _):
                    for k in range(TOP_K):
                        src = dsm[static_s, k, t]
                        pltpu.make_async_copy(
                            ys_hbm.at[pl.ds(pl.multiple_of(src, sub), sub), :],
                            ybuf.at[static_s, pl.ds(pl.multiple_of((k * tm + t) * sub, sub), sub), :],
                            gsem.at[static_s]).start(priority=k % 2)
                    return 0

                lax.fori_loop(0, tm, body, 0, unroll=4)

    def wait_gathers(s):
        pltpu.make_async_copy(ys_hbm.at[pl.ds(0, TOP_K * tm * sub), :], ybuf.at[s], gsem.at[s]).wait()

    @pl.when(i == 0)
    def _():
        idx_copy(0).start()
        idx_copy(0).wait()
        start_gathers(0)

        @pl.when(nt > 1)
        def _():
            idx_copy(1).start()

    @pl.when(i + 1 < nt)
    def _():
        idx_copy(i + 1).wait()
        start_gathers(i + 1)

    @pl.when(i + 2 < nt)
    def _():
        idx_copy(i + 2).start()

    wait_gathers(slot)
    route = route_ref[...]
    ws = [route[:, ROUTE_W + k:ROUTE_W + k + 1] for k in range(TOP_K)]
    chunks = []
    ssq = jnp.zeros((tm, 1), F32)
    for s in range(sub):
        acc = x1_ref[:, s * LANES:(s + 1) * LANES]
        for k in range(TOP_K):
            acc = acc + ws[k] * ybuf[slot, pl.ds(k * tm * sub + s, tm, stride=sub), :]
        chunks.append(acc)
        ssq = ssq + jnp.sum(acc * acc, axis=-1, keepdims=True)
    inv = lax.rsqrt(ssq / d + NORM_EPS)
    for s in range(sub):
        sl = slice(s * LANES, (s + 1) * LANES)
        o_ref[:, sl] = chunks[s] * inv * g_ref[:, sl]


def _combine(x1, route, dest_t, ys, g_final):
    n, d = x1.shape
    tm = MOE_TILE
    sub = d // LANES
    row = lambda i: (i, 0)
    return pl.pallas_call(
        _combine_kernel,
        out_shape=jax.ShapeDtypeStruct((n, d), F32),
        grid=(n // tm,),
        in_specs=[pl.BlockSpec((tm, d), row), pl.BlockSpec((tm, LANES), row),
                  pl.BlockSpec(memory_space=pl.ANY), pl.BlockSpec(memory_space=pl.ANY),
                  pl.BlockSpec((1, d), lambda i: (0, 0))],
        out_specs=pl.BlockSpec((tm, d), row),
        scratch_shapes=[pltpu.VMEM((2, TOP_K * tm * sub, LANES), F32),
                        pltpu.SMEM((2, SUBLANES, tm), I32),
                        pltpu.SemaphoreType.DMA((2,)), pltpu.SemaphoreType.DMA((2,))],
        compiler_params=pltpu.CompilerParams(dimension_semantics=("arbitrary",),
                                             vmem_limit_bytes=VMEM_LIMIT),
        name="combine_norm",
    )(x1, route, dest_t, ys, g_final)


def _pad_heads(w, per_head, width=LANES):
    k = w.shape[0]
    w = w.reshape(k, HEADS, per_head)
    w = jnp.pad(w, ((0, 0), (0, 0), (0, width - per_head)))
    return w.reshape(k, HEADS * width)


def _aug_constants():
    place = np.zeros((LANES, 2 * LANES), np.float32)
    ones = np.zeros((1, 2 * LANES), np.float32)
    for hd in range(HEADS):
        for piece in range(3):
            src = FG_LO + piece * HEADS + hd
            place[src, 16 * hd + piece] = 1.0
            place[src, LANES + 16 * hd + 3 + piece] = -1.0
            ones[0, 16 * hd + 3 + piece] = 1.0
            ones[0, LANES + 16 * hd + piece] = 1.0
    return jnp.asarray(place, BF16), jnp.asarray(ones, F32)


def _layer(x2, pos2, batch, seq_len, g_attn_norm, w_in, b_fgate, g_q_a, w_q_b, g_kv_a, w_kv_b,
           w_fox_out, w_mla_out, b_merge, w_o, g_ffn_norm, w_router, b_router, w_gu, b_gu,
           w_down, b_down, g_out):
    n, d = x2.shape
    fw = HEADS * HEAD_DIM
    o = 0
    w_qf = w_in[:, o:o + fw]; o += fw
    w_kf = w_in[:, o:o + fw]; o += fw
    w_vf = w_in[:, o:o + fw]; o += fw
    w_f = w_in[:, o:o + HEADS]; o += HEADS
    w_ql = w_in[:, o:o + Q_RANK]; o += Q_RANK
    w_ckv = w_in[:, o:o + KV_RANK]; o += KV_RANK
    w_kpe = w_in[:, o:o + ROPE_DIM]; o += ROPE_DIM
    w_gate = w_in[:, o:]

    wq = (w_qf * (HEAD_DIM ** -0.5)).astype(BF16)
    wk = w_kf.astype(BF16)
    wmisc = jnp.concatenate([jnp.zeros((d, PE_LO), F32), w_kpe, w_f,
                             jnp.zeros((d, LANES - FG_LO - HEADS), F32)], axis=1)
    wlat = jnp.concatenate([w_ql, w_ckv, wmisc], axis=1).astype(BF16)
    bf128 = jnp.zeros((1, LANES), F32).at[0, FG_LO:FG_LO + HEADS].set(b_fgate)
    wqb = _pad_heads(w_q_b, HEAD_DIM + ROPE_DIM).astype(BF16)
    wkv = w_kv_b.reshape(KV_RANK, HEADS, 2 * HEAD_DIM)
    wkbk = wkv[:, :, :HEAD_DIM].reshape(KV_RANK, fw).astype(BF16)
    wkbv = wkv[:, :, HEAD_DIM:].reshape(KV_RANK, fw).T.astype(BF16)
    ltri = jnp.asarray(np.tril(np.ones((ROW_TILE, ROW_TILE), np.float32)), BF16)
    lstrict = jnp.asarray(np.tril(np.ones((MOE_TILE, MOE_TILE), np.float32), -1), BF16)
    ustrict = jnp.asarray(np.triu(np.ones((LANES, LANES), np.float32), 1), BF16)
    place, ones = _aug_constants()
    half = ROPE_DIM // 2
    inv_freq = ROPE_THETA ** (-jnp.arange(half, dtype=F32) / half)
    freq = jnp.zeros((1, LANES), F32).at[0, PE_LO:PE_MID].set(inv_freq).at[0, PE_MID:PE_HI].set(inv_freq)

    qf, kf, vf, qm, km, vm, gates = _inproj(
        x2, pos2, g_attn_norm.reshape(1, d), wq, wk, w_vf.T.astype(BF16), wlat, w_gate.astype(BF16),
        bf128, g_q_a.reshape(1, -1), wqb, g_kv_a.reshape(1, -1), wkbk, wkbv, ltri, place, ones,
        freq, seq_len=seq_len)

    y_fox = _attention(qf, kf, vf, batch=batch, seq_len=seq_len, chunk_mask=False)
    y_mla = _attention(qm, km, vm, batch=batch, seq_len=seq_len, chunk_mask=True)

    wr = jnp.pad(w_router, ((0, 0), (0, LANES - N_EXPERTS)))
    wrh = wr.astype(BF16)
    wrl = (wr - wrh.astype(F32)).astype(BF16)
    br = jnp.full((1, LANES), NEG, F32).at[0, :N_EXPERTS].set(b_router)
    n_blocks = n * TOP_K // EXPERT_ROWS + N_EXPERTS
    x1, route, dest_t, meta, xs = _outproj(
        y_fox, y_mla, gates, x2, b_merge.reshape(1, -1), w_fox_out.astype(BF16),
        w_mla_out.astype(BF16), w_o.astype(BF16), g_ffn_norm.reshape(1, d), wrh, wrl, br, lstrict,
        ustrict, n_blocks=n_blocks)

    block_e = meta[:SUBLANES].reshape(-1)[:n_blocks]
    order = jnp.argsort(block_e, stable=True).astype(I32)
    nused = meta[SUBLANES, 0:1]
    be_sorted = jnp.minimum(block_e[order], N_EXPERTS - 1).astype(I32)
    be_sorted = jnp.where(jnp.arange(n_blocks) < nused[0], be_sorted,
                          be_sorted[jnp.maximum(nused[0] - 1, 0)])
    ys = _experts(order, be_sorted, nused, xs, w_gu, b_gu, w_down, b_down)
    return _combine(x1, route, dest_t, ys, g_out.reshape(1, d))


def kernel(x, positions, g_attn_norm, w_in, b_fgate, g_q_a, w_q_b, g_kv_a, w_kv_b, w_fox_out, w_mla_out, b_merge, w_o, g_ffn_norm, w_router, b_router, w_gu, b_gu, w_down, b_down, g_final):
    batch, seq_len, d = x.shape
    depth = w_in.shape[0]
    assert depth == 1, "the fused combine + final-norm kernel assumes a single layer"
    assert seq_len % ATT_TILE == 0 and d % LANES == 0
    assert (batch * seq_len * TOP_K) % EXPERT_ROWS == 0
    assert batch * seq_len * TOP_K // EXPERT_ROWS + N_EXPERTS <= SUBLANES * LANES
    x2 = x.reshape(batch * seq_len, d)
    pos2 = positions.reshape(batch * seq_len, 1).astype(I32)
    out = _layer(x2, pos2, batch, seq_len, g_attn_norm[0], w_in[0], b_fgate[0], g_q_a[0], w_q_b[0],
                 g_kv_a[0], w_kv_b[0], w_fox_out[0], w_mla_out[0], b_merge[0], w_o[0],
                 g_ffn_norm[0], w_router[0], b_router[0], w_gu[0], b_gu[0], w_down[0], b_down[0],
                 g_final)
    return out.reshape(batch, seq_len, d)
```

```python
import functools

import jax
import jax.numpy as jnp
import numpy as np
from jax import lax
from jax.experimental import pallas as pl
from jax.experimental.pallas import tpu as pltpu

F32 = jnp.float32
BF16 = jnp.bfloat16
I32 = jnp.int32

LANES = 128
SUBLANES = 8
VMEM_LIMIT = 56 * 1024 * 1024

NORM_EPS = 1e-6
HEADS = 8
HEAD_DIM = 64
ROPE_DIM = 32
Q_RANK = 256
KV_RANK = 128
N_EXPERTS = 32
TOP_K = 4
EXPERT_FF = 1024
SWIGLU_LIMIT = 7.0
SWIGLU_ALPHA = 1.702
ROPE_THETA = 10000.0
CHUNK = 64

NEG = -1e30
LOG2E = 1.4426950408889634

ROW_TILE = 256
MOE_TILE = 512
ATT_TILE = 512
ATT_HEADS = 2
ATT_GROUP = 4
EXPERT_ROWS = MOE_TILE

PE_LO, PE_MID, PE_HI = 64, 80, 96
FG_LO = 96
AUG_LO = 64
ROUTE_DEST = 0
ROUTE_W = 8


def _dot(a, b):
    return jnp.dot(a, b, preferred_element_type=F32)


def _dot_nt(a, b):
    return lax.dot_general(a, b, (((1,), (1,)), ((), ())), preferred_element_type=F32)


def _split3(a):
    hi = a.astype(BF16)
    r1 = a - hi.astype(F32)
    mid = r1.astype(BF16)
    lo = (r1 - mid.astype(F32)).astype(BF16)
    return hi, mid, lo


def _rms(x, g):
    return x * lax.rsqrt(jnp.mean(x * x, axis=-1, keepdims=True) + NORM_EPS) * g


def _inproj_kernel(x_ref, pos_ref, g_ref, wq_ref, wk_ref, wv_ref, wlat_ref, wg_ref, bf_ref,
                   gq_ref, wqb_ref, gkv_ref, wkbk_ref, wkbv_ref, ltri_ref, place_ref, ones_ref,
                   freq_ref,
                   qf_ref, kf_ref, vf_ref, qm_ref, km_ref, vm_ref, gate_ref,
                   carry_ref, *, tiles_per_seq, mla_scale):
    i = pl.program_id(0)

    @pl.when(i % tiles_per_seq == 0)
    def _():
        carry_ref[...] = jnp.zeros_like(carry_ref)

    h = _rms(x_ref[...], g_ref[...]).astype(BF16)
    tm = h.shape[0]
    lane = lax.broadcasted_iota(I32, (tm, LANES), 1)

    lat = _dot(h, wlat_ref[...])
    q_lat = lat[:, :Q_RANK]
    c_kv = lat[:, Q_RANK:Q_RANK + KV_RANK]
    misc = lat[:, Q_RANK + KV_RANK:]

    qn = _rms(q_lat, gq_ref[...]).astype(BF16)
    kvn = _rms(c_kv, gkv_ref[...]).astype(BF16)
    qfull = _dot(qn, wqb_ref[...])
    knope = _dot(kvn, wkbk_ref[...])
    vm_ref[0] = _dot_nt(wkbv_ref[...], kvn).astype(BF16)

    z = misc + bf_ref[...]
    logf = jnp.minimum(z, 0.0) - jnp.log1p(jnp.exp(-jnp.abs(z)))
    fmask = (lane >= FG_LO) & (lane < FG_LO + HEADS)
    logf = jnp.where(fmask, logf, 0.0)
    l_hi, l_mid, l_lo = _split3(logf)
    ltri = ltri_ref[...]
    c = _dot(ltri, l_hi) + _dot(ltri, l_mid) + _dot(ltri, l_lo) + carry_ref[...]
    carry_ref[...] = c[tm - 1:tm, :]
    c_hi, c_mid, c_lo = _split3(c * LOG2E)
    c3 = (c_hi.astype(F32) + pltpu.roll(c_mid.astype(F32), HEADS, 1)
          + pltpu.roll(c_lo.astype(F32), 2 * HEADS, 1)).astype(BF16)
    augc = _dot(c3, place_ref[...]) + ones_ref[...]
    low = lane < HEAD_DIM
    amask = (lane >= AUG_LO) & (lane < AUG_LO + 6)

    def aug_block(half, hd):
        src = augc[:, half * LANES:(half + 1) * LANES]
        return jnp.where(amask, pltpu.roll(src, (AUG_LO - 16 * hd) % LANES, 1), 0.0)

    def head_block(compact, hd):
        pair = compact[:, (hd // 2) * LANES:(hd // 2 + 1) * LANES]
        if hd % 2:
            pair = pltpu.roll(pair, HEAD_DIM, 1)
        return jnp.where(low, pair, 0.0)

    qc = _dot(h, wq_ref[...]) * LOG2E
    kc = _dot(h, wk_ref[...])
    vf_ref[0] = _dot_nt(wv_ref[...], h).astype(BF16)
    gate_ref[...] = _dot(h, wg_ref[...]).astype(BF16)

    for hd in range(HEADS):
        sl = slice(hd * LANES, (hd + 1) * LANES)
        qf_ref[:, sl] = (head_block(qc, hd) + aug_block(0, hd)).astype(BF16)
        kf_ref[:, sl] = (head_block(kc, hd) + aug_block(1, hd)).astype(BF16)

    ang = pos_ref[...].astype(F32) * freq_ref[...]
    cosv = jnp.cos(ang)
    sinv = jnp.sin(ang)
    s1 = jnp.where((lane >= PE_LO) & (lane < PE_MID), -sinv, 0.0)
    s2 = jnp.where((lane >= PE_MID) & (lane < PE_HI), sinv, 0.0)

    def rope(v):
        return v * cosv + pltpu.roll(v, LANES - 16, 1) * s1 + pltpu.roll(v, 16, 1) * s2

    kpe = jnp.where((lane >= PE_LO) & (lane < PE_HI), rope(misc), 0.0)
    for hd in range(HEADS):
        sl = slice(hd * LANES, (hd + 1) * LANES)
        qm_ref[:, sl] = (rope(qfull[:, sl]) * mla_scale).astype(BF16)
        km_ref[:, sl] = (head_block(knope, hd) + kpe).astype(BF16)


def _inproj(x2, pos2, g_attn, wq, wk, wv, wlat, wg, bf128, gq, wqb, gkv, wkbk, wkbv, ltri,
            place, ones, freq, *, seq_len):
    n, d = x2.shape
    tm = ROW_TILE
    hw = HEADS * LANES
    vw = HEADS * HEAD_DIM
    const = lambda i: (0, 0)
    row = lambda i: (i, 0)
    full = lambda a: pl.BlockSpec(a.shape, const)
    rows_out = lambda w: (jax.ShapeDtypeStruct((n, w), BF16), pl.BlockSpec((tm, w), row))
    vt_out = (jax.ShapeDtypeStruct((n // tm, vw, tm), BF16),
              pl.BlockSpec((1, vw, tm), lambda i: (i, 0, 0)))
    outs = [rows_out(hw), rows_out(hw), vt_out, rows_out(hw), rows_out(hw), vt_out,
            rows_out(wg.shape[1])]
    consts = (g_attn, wq, wk, wv, wlat, wg, bf128, gq, wqb, gkv, wkbk, wkbv, ltri, place, ones, freq)
    return pl.pallas_call(
        functools.partial(_inproj_kernel, tiles_per_seq=seq_len // tm,
                          mla_scale=float((HEAD_DIM + ROPE_DIM) ** -0.5) * LOG2E),
        out_shape=[o[0] for o in outs],
        grid=(n // tm,),
        in_specs=[pl.BlockSpec((tm, d), row), pl.BlockSpec((tm, 1), row)] + [full(a) for a in consts],
        out_specs=[o[1] for o in outs],
        scratch_shapes=[pltpu.VMEM((1, LANES), F32)],
        compiler_params=pltpu.CompilerParams(dimension_semantics=("arbitrary",),
                                             vmem_limit_bytes=VMEM_LIMIT),
        name="inproj",
    )(x2, pos2, *consts)


def _attn_kernel(q_ref, k_ref, vt_ref, o_ref, *, chunk_mask):
    i = pl.program_id(2)
    tq = q_ref.shape[0]
    tk = vt_ref.shape[2]

    def scores(j, masked, lo=0):
        start = pl.multiple_of(j * tk, tk)
        ss = [_dot_nt(k_ref[pl.ds(start, tk), hd * LANES:(hd + 1) * LANES],
                      q_ref[lo:, hd * LANES:(hd + 1) * LANES])
              for hd in range(ATT_HEADS)]
        if masked:
            keyg = lax.broadcasted_iota(I32, (tk, tq - lo), 0) + j * tk
            qlo = lax.broadcasted_iota(I32, (tk, tq - lo), 1) + (i * tq + lo)
            allowed = (keyg // CHUNK) <= (qlo // CHUNK) if chunk_mask else keyg <= qlo
            ss = [jnp.where(allowed, s, NEG) for s in ss]
        return ss

    ones = jnp.ones((2 * SUBLANES, tk), BF16)

    def update(j, ss, state, lo=0):
        vt = vt_ref[j]
        new = []
        for hd, s in enumerate(ss):
            m_all, acc_all = state[hd]
            m, acc = m_all[:, lo:], acc_all[:, lo:]
            m_new = jnp.maximum(m, jnp.max(s, axis=0, keepdims=True))
            alpha = jnp.exp2(m - m_new)
            p = jnp.exp2((s - m_new[0:1]).astype(BF16))
            va = jnp.concatenate([vt[hd * HEAD_DIM:(hd + 1) * HEAD_DIM, :], ones], axis=0)
            acc = alpha[0:1] * acc + _dot(va, p)
            if lo:
                m_new = jnp.concatenate([m_all[:, :lo], m_new], axis=1)
                acc = jnp.concatenate([acc_all[:, :lo], acc], axis=1)
            new.append((m_new, acc))
        return tuple(new)

    n_full = (i * tq) // tk
    group = max(1, tq // tk)

    def run_group(first, state, masked, nblk=group):
        los = [u * tk if masked else 0 for u in range(nblk)]
        ss = scores(first, masked, los[0])
        for u in range(nblk):
            nxt = scores(first + u + 1, masked, los[u + 1]) if u + 1 < nblk else None
            state = update(first + u, ss, state, los[u])
            ss = nxt
        return state

    long = ATT_GROUP * group
    n_long = n_full // long
    init1 = (jnp.full((SUBLANES, tq), NEG, F32), jnp.zeros((HEAD_DIM + 2 * SUBLANES, tq), F32))
    state = lax.fori_loop(0, n_long, lambda jj, c: run_group(jj * long, c, False, long),
                          (init1,) * ATT_HEADS)
    done = n_long * long
    size = long // 2
    while size >= group:
        take = (n_full - done) >= size
        state = lax.cond(take, lambda st, d=done, sz=size: run_group(d, st, False, sz),
                         lambda st: st, state)
        done = done + jnp.where(take, size, 0)
        size //= 2
    state = run_group(n_full, state, True)
    out_t = jnp.concatenate([acc[:HEAD_DIM] / acc[HEAD_DIM:HEAD_DIM + 1] for _, acc in state],
                            axis=0)
    o_ref[...] = out_t.T.astype(o_ref.dtype)


def _attention(q, k, vt, *, batch, seq_len, chunk_mask):
    n = q.shape[0]
    t = ATT_TILE
    tk = vt.shape[2]
    nq = seq_len // t
    nkb = seq_len // tk
    hp = ATT_HEADS
    pairs = HEADS // hp
    return pl.pallas_call(
        functools.partial(_attn_kernel, chunk_mask=chunk_mask),
        out_shape=jax.ShapeDtypeStruct((n, HEADS * HEAD_DIM), BF16),
        grid=(batch, pairs, nq),
        in_specs=[pl.BlockSpec((t, hp * LANES), lambda b, p, i: (b * nq + i, p)),
                  pl.BlockSpec((seq_len, hp * LANES), lambda b, p, i: (b, p)),
                  pl.BlockSpec((nkb, hp * HEAD_DIM, tk), lambda b, p, i: (b, p, 0))],
        out_specs=pl.BlockSpec((t, hp * HEAD_DIM), lambda b, p, i: (b * nq + i, p)),
        compiler_params=pltpu.CompilerParams(
            dimension_semantics=("arbitrary", "arbitrary", "arbitrary"),
            vmem_limit_bytes=VMEM_LIMIT),
        name="attn_mla" if chunk_mask else "attn_fox",
    )(q, k, vt)


def _outproj_kernel(yf_ref, ym_ref, gate_ref, x_ref, bm_ref, wfo_ref, wmo_ref, wo_ref, gffn_ref,
                    wrh_ref, wrl_ref, br_ref, lstrict_ref, ustrict_ref,
                    x1_ref, dest_ref, wt_ref, meta_ref, xs_hbm,
                    fill_ref, cur_ref, nfree_ref, tbl_ref,
                    hbuf, dbuf, dsm, zbuf, mbuf, msm, ssem, isem, zsem, msem):
    i = pl.program_id(0)
    nt = pl.num_programs(0)
    d = x_ref.shape[1]
    tm = x_ref.shape[0]
    sub = d // LANES
    rows = EXPERT_ROWS
    n_blocks = xs_hbm.shape[0] // (rows * sub)
    slot = i % 2

    def idx_copy(s):
        return pltpu.make_async_copy(dbuf.at[s], dsm.at[s], isem.at[s])

    def start_scatters(s):
        for static_s in range(2):
            @pl.when(s == static_s)
            def _(static_s=static_s):
                def body(t, _):
                    src = hbuf.at[static_s, pl.ds(pl.multiple_of(t * sub, sub), sub), :]
                    for k in range(TOP_K):
                        dst = dsm[static_s, k, t]
                        pltpu.make_async_copy(
                            src, xs_hbm.at[pl.ds(pl.multiple_of(dst, sub), sub), :],
                            ssem.at[static_s]).start(priority=k % 2)
                    return 0

                lax.fori_loop(0, tm, body, 0, unroll=4)

    def wait_scatters(s):
        for _ in range(TOP_K):
            pltpu.make_async_copy(hbuf.at[s], xs_hbm.at[pl.ds(0, tm * sub), :], ssem.at[s]).wait()

    @pl.when(i == 0)
    def _():
        fill_ref[...] = jnp.full_like(fill_ref, float(rows))
        cur_ref[...] = jnp.zeros_like(cur_ref)
        nfree_ref[...] = jnp.zeros_like(nfree_ref)
        tbl_ref[...] = jnp.full_like(tbl_ref, float(N_EXPERTS))
        zbuf[...] = jnp.zeros_like(zbuf)

    @pl.when(i > 0)
    def _():
        idx_copy(1 - slot).wait()
        start_scatters(1 - slot)

    a = _dot(yf_ref[...], wfo_ref[...])
    b = _dot(ym_ref[...], wmo_ref[...])
    g = 1.0 / (1.0 + jnp.exp(-(gate_ref[...].astype(F32) + bm_ref[...])))
    merged = (g[:, :d] * a + g[:, d:] * b).astype(BF16)
    x1 = x_ref[...] + _dot(merged, wo_ref[...])
    x1_ref[...] = x1
    h2 = _rms(x1, gffn_ref[...])

    hi = h2.astype(BF16)
    lo = (h2 - hi.astype(F32)).astype(BF16)
    wrh = wrh_ref[...]
    logits = _dot(hi, wrh) + _dot(lo, wrh) + _dot(hi, wrl_ref[...]) + br_ref[...]

    lane = lax.broadcasted_iota(I32, (tm, LANES), 1)
    vals = logits
    sels, tops = [], []
    for _ in range(TOP_K):
        mx = jnp.max(vals, axis=-1, keepdims=True)
        idx = jnp.min(jnp.where(vals == mx, lane, LANES), axis=-1, keepdims=True)
        sel = lane == idx
        vals = jnp.where(sel, NEG, vals)
        sels.append(sel)
        tops.append(mx)
    es = [jnp.exp(tv - tops[0]) for tv in tops]
    den = es[0] + es[1] + es[2] + es[3]

    onehot = jnp.zeros((tm, LANES), F32)
    for sel in sels:
        onehot = onehot + sel.astype(F32)
    before = _dot(lstrict_ref[...], onehot.astype(BF16))
    cnt = jnp.sum(onehot, axis=0, keepdims=True)

    fill = fill_ref[...]
    cur = cur_ref[...]
    nfree = nfree_ref[...]
    need = ((fill + cnt) > float(rows)).astype(F32)
    need8 = jnp.broadcast_to(need, (SUBLANES, LANES)).astype(BF16)
    newid = nfree + _dot(need8, ustrict_ref[...])[0:1, :]
    pos = fill + before
    dest = jnp.where(pos < float(rows), cur * rows + pos, newid * rows + pos - float(rows))
    fill_ref[...] = fill + cnt - need * float(rows)
    cur_ref[...] = jnp.where(need > 0, newid, cur)
    nfree_ref[...] = nfree + jnp.sum(need, axis=-1, keepdims=True)
    blk_id = (lax.broadcasted_iota(I32, (SUBLANES, LANES), 0) * LANES
              + lax.broadcasted_iota(I32, (SUBLANES, LANES), 1)).astype(F32)
    tbl = tbl_ref[...]
    for e in range(N_EXPERTS):
        hit = (blk_id == newid[:, e:e + 1]) & (need[:, e:e + 1] > 0)
        tbl = jnp.where(hit, float(e), tbl)
    tbl_ref[...] = tbl

    route = jnp.zeros((tm, LANES), F32)
    for k in range(TOP_K):
        dest_k = jnp.sum(jnp.where(sels[k], dest, 0.0), axis=-1, keepdims=True)
        route = jnp.where(lane == ROUTE_DEST + k, dest_k, route)
        route = jnp.where(lane == ROUTE_W + k, es[k] / den, route)
    route_t = route.T
    dest_t = (route_t[0:SUBLANES, :] * float(sub)).astype(I32)
    dest_ref[0] = dest_t
    wt_ref[0] = route_t[ROUTE_W:ROUTE_W + SUBLANES, :]

    @pl.when(i >= 2)
    def _():
        wait_scatters(slot)

    for s in range(sub):
        hbuf[slot, pl.ds(s, tm, stride=sub), :] = h2[:, s * LANES:(s + 1) * LANES]
    dbuf[slot] = dest_t
    idx_copy(slot).start()

    @pl.when(i == nt - 1)
    def _():
        idx_copy(slot).wait()
        start_scatters(slot)

        @pl.when(i >= 1)
        def _():
            wait_scatters(1 - slot)

        wait_scatters(slot)

        meta = jnp.concatenate([tbl_ref[...], jnp.broadcast_to(nfree_ref[...], (SUBLANES, LANES))],
                               axis=0).astype(I32)
        meta_ref[...] = meta
        state = jnp.concatenate([fill_ref[...], cur_ref[...], nfree_ref[...],
                                 jnp.zeros((SUBLANES - 3, LANES), F32)], axis=0).astype(I32)
        mbuf[...] = state
        mcopy = pltpu.make_async_copy(mbuf, msm, msem.at[0])
        mcopy.start()
        mcopy.wait()

        def zero_copy(first_row, n_rows):
            return pltpu.make_async_copy(
                zbuf.at[pl.ds(0, n_rows * sub), :],
                xs_hbm.at[pl.ds(pl.multiple_of(first_row * sub, sub), n_rows * sub), :], zsem.at[0])

        chunks = [rows >> (s + 1) for s in range(rows.bit_length() - 1)]
        plans = []
        for e in range(N_EXPERTS):
            rem = rows - msm[0, e]
            at = msm[1, e] * rows + msm[0, e]
            for c in chunks:
                take = (rem & c) != 0
                plans.append((take, zero_copy(at, c)))
                at = at + jnp.where(take, c, 0)
        for j in range(N_EXPERTS):
            blk = msm[2, 0] + j
            safe = jnp.minimum(blk, n_blocks - 1)
            plans.append((blk < n_blocks, zero_copy(safe * rows, rows)))
        for take, cp in plans:
            pl.when(take)(cp.start)
        for take, cp in plans:
            pl.when(take)(cp.wait)


def _outproj(yf, ym, gates, x2, bm, wfo, wmo, wo, gffn, wrh, wrl, br, lstrict, ustrict, *, n_blocks):
    n, d = x2.shape
    tm = MOE_TILE
    rows = EXPERT_ROWS
    sub = d // LANES
    const = lambda i: (0, 0)
    row = lambda i: (i, 0)
    full = lambda a: pl.BlockSpec(a.shape, const)
    consts = (bm, wfo, wmo, wo, gffn, wrh, wrl, br, lstrict, ustrict)
    return pl.pallas_call(
        _outproj_kernel,
        out_shape=[jax.ShapeDtypeStruct((n, d), F32),
                   jax.ShapeDtypeStruct((n // tm, SUBLANES, tm), I32),
                   jax.ShapeDtypeStruct((n // tm, SUBLANES, tm), F32),
                   jax.ShapeDtypeStruct((2 * SUBLANES, LANES), I32),
                   jax.ShapeDtypeStruct((n_blocks * rows * sub, LANES), F32)],
        grid=(n // tm,),
        in_specs=[pl.BlockSpec((tm, yf.shape[1]), row), pl.BlockSpec((tm, ym.shape[1]), row),
                  pl.BlockSpec((tm, gates.shape[1]), row), pl.BlockSpec((tm, d), row)]
                 + [full(a) for a in consts],
        out_specs=[pl.BlockSpec((tm, d), row),
                   pl.BlockSpec((1, SUBLANES, tm), lambda i: (i, 0, 0)),
                   pl.BlockSpec((1, SUBLANES, tm), lambda i: (i, 0, 0)),
                   pl.BlockSpec((2 * SUBLANES, LANES), const),
                   pl.BlockSpec(memory_space=pl.ANY)],
        scratch_shapes=[pltpu.VMEM((1, LANES), F32), pltpu.VMEM((1, LANES), F32),
                        pltpu.VMEM((1, LANES), F32), pltpu.VMEM((SUBLANES, LANES), F32),
                        pltpu.VMEM((2, tm * sub, LANES), F32),
                        pltpu.VMEM((2, SUBLANES, tm), I32),
                        pltpu.SMEM((2, SUBLANES, tm), I32),
                        pltpu.VMEM((rows * sub, LANES), F32),
                        pltpu.VMEM((SUBLANES, LANES), I32),
                        pltpu.SMEM((SUBLANES, LANES), I32),
                        pltpu.SemaphoreType.DMA((2,)), pltpu.SemaphoreType.DMA((2,)),
                        pltpu.SemaphoreType.DMA((1,)), pltpu.SemaphoreType.DMA((1,))],
        compiler_params=pltpu.CompilerParams(dimension_semantics=("arbitrary",),
                                             vmem_limit_bytes=VMEM_LIMIT),
        name="outproj_router",
    )(yf, ym, gates, x2, *consts)


def _expert_kernel(order_ref, be_ref, nused_ref,
                   xs_ref, wgu_ref, bgu_ref, wd_ref, bd_ref, ys_ref, wgu_b, wd_b):
    b = pl.program_id(0)
    rows = EXPERT_ROWS
    sub = xs_ref.shape[0] // rows

    @pl.when(b < nused_ref[0])
    def _():
        changed = jnp.logical_or(b == 0, be_ref[b] != be_ref[jnp.maximum(b - 1, 0)])

        @pl.when(changed)
        def _():
            wgu_b[...] = wgu_ref[0].astype(BF16)
            wd_b[...] = wd_ref[0].astype(BF16)

        x = jnp.concatenate(
            [xs_ref[pl.ds(s, rows, stride=sub), :].astype(BF16) for s in range(sub)], axis=1)
        gu = _dot(x, wgu_b[...]) + bgu_ref[0]
        gate = jnp.minimum(gu[:, :EXPERT_FF], SWIGLU_LIMIT)
        up = jnp.clip(gu[:, EXPERT_FF:], -SWIGLU_LIMIT, SWIGLU_LIMIT)
        glu = gate * (1.0 / (1.0 + jnp.exp(-SWIGLU_ALPHA * gate)))
        y = _dot(((up + 1.0) * glu).astype(BF16), wd_b[...]) + bd_ref[0]
        for s in range(sub):
            ys_ref[pl.ds(s, rows, stride=sub), :] = y[:, s * LANES:(s + 1) * LANES]

    @pl.when(b >= nused_ref[0])
    def _():
        ys_ref[...] = jnp.zeros_like(ys_ref)


def _experts(order, block_e, nused, xs, w_gu, b_gu, w_d, b_d):
    e, d, ff2 = w_gu.shape
    rows = EXPERT_ROWS
    sub = d // LANES
    n_blocks = xs.shape[0] // (rows * sub)
    wmap = lambda b, od, be, nu: (be[b], 0, 0)
    xmap = lambda b, od, be, nu: (od[b], 0)
    grid_spec = pltpu.PrefetchScalarGridSpec(
        num_scalar_prefetch=3,
        grid=(n_blocks,),
        in_specs=[pl.BlockSpec((rows * sub, LANES), xmap),
                  pl.BlockSpec((1, d, ff2), wmap), pl.BlockSpec((1, 1, ff2), wmap),
                  pl.BlockSpec((1, ff2 // 2, d), wmap), pl.BlockSpec((1, 1, d), wmap)],
        out_specs=pl.BlockSpec((rows * sub, LANES), xmap),
        scratch_shapes=[pltpu.VMEM((d, ff2), BF16), pltpu.VMEM((ff2 // 2, d), BF16)],
    )
    return pl.pallas_call(
        _expert_kernel,
        out_shape=jax.ShapeDtypeStruct(xs.shape, F32),
        grid_spec=grid_spec,
        compiler_params=pltpu.CompilerParams(dimension_semantics=("arbitrary",),
                                             vmem_limit_bytes=VMEM_LIMIT),
        name="experts",
    )(order, block_e, nused, xs, w_gu, b_gu.reshape(e, 1, ff2), w_d, b_d.reshape(e, 1, d))


def _combine_kernel(x1_ref, dest_hbm, wt_hbm, ys_hbm, g_ref, o_ref,
                    ybuf, accbuf, dsm, wsm, gsem, isem):
    i = pl.program_id(0)
    nt = pl.num_programs(0)
    tm, d = x1_ref.shape
    sub = d // LANES
    slot = i % 2

    def idx_copies(tile):
        s = tile % 2
        return (pltpu.make_async_copy(dest_hbm.at[tile], dsm.at[s], isem.at[s, 0]),
                pltpu.make_async_copy(wt_hbm.at[tile], wsm.at[s], isem.at[s, 1]))

    def start_gathers(tile):
        for static_s in range(2):
            @pl.when(tile % 2 == static_s)
            def _(static_s=static_s):
                def body(t, _):
                    for k in range(TOP_K):
                        src = dsm[static_s, k, t]
                        pltpu.make_async_copy(
                            ys_hbm.at[pl.ds(pl.multiple_of(src, sub), sub), :],
                            ybuf.at[static_s, pl.ds(pl.multiple_of((k * tm + t) * sub, sub), sub), :],
                            gsem.at[static_s]).start(priority=k % 2)
                    return 0

                lax.fori_loop(0, tm, body, 0, unroll=4)

    def wait_gathers(s):
        pltpu.make_async_copy(ys_hbm.at[pl.ds(0, TOP_K * tm * sub), :], ybuf.at[s], gsem.at[s]).wait()

    @pl.when(i == 0)
    def _():
        for c in idx_copies(0):
            c.start()
        for c in idx_copies(0):
            c.wait()
        start_gathers(0)

        @pl.when(nt > 1)
        def _():
            for c in idx_copies(1):
                c.start()

    @pl.when(i + 1 < nt)
    def _():
        for c in idx_copies(i + 1):
            c.wait()
        start_gathers(i + 1)

    wait_gathers(slot)

    for static_s in range(2):
        @pl.when(slot == static_s)
        def _(static_s=static_s):
            def body(t, _):
                row = pl.multiple_of(t * sub, sub)
                acc = wsm[static_s, 0, t] * ybuf[static_s, pl.ds(row, sub), :]
                for k in range(1, TOP_K):
                    acc = acc + wsm[static_s, k, t] * ybuf[static_s, pl.ds(k * tm * sub + row, sub), :]
                accbuf[pl.ds(row, sub), :] = acc
                return 0

            lax.fori_loop(0, tm, body, 0, unroll=8)

    @pl.when(i + 2 < nt)
    def _():
        for c in idx_copies(i + 2):
            c.start()

    chunks = []
    ssq = jnp.zeros((tm, 1), F32)
    for s in range(sub):
        acc = x1_ref[:, s * LANES:(s + 1) * LANES] + accbuf[pl.ds(s, tm, stride=sub), :]
        chunks.append(acc)
        ssq = ssq + jnp.sum(acc * acc, axis=-1, keepdims=True)
    inv = lax.rsqrt(ssq / d + NORM_EPS)
    for s in range(sub):
        sl = slice(s * LANES, (s + 1) * LANES)
        o_ref[:, sl] = chunks[s] * inv * g_ref[:, sl]


def _combine(x1, dest_t, wt, ys, g_final):
    n, d = x1.shape
    tm = MOE_TILE
    sub = d // LANES
    row = lambda i: (i, 0)
    return pl.pallas_call(
        _combine_kernel,
        out_shape=jax.ShapeDtypeStruct((n, d), F32),
        grid=(n // tm,),
        in_specs=[pl.BlockSpec((tm, d), row),
                  pl.BlockSpec(memory_space=pl.ANY), pl.BlockSpec(memory_space=pl.ANY),
                  pl.BlockSpec(memory_space=pl.ANY),
                  pl.BlockSpec((1, d), lambda i: (0, 0))],
        out_specs=pl.BlockSpec((tm, d), row),
        scratch_shapes=[pltpu.VMEM((2, TOP_K * tm * sub, LANES), F32),
                        pltpu.VMEM((tm * sub, LANES), F32),
                        pltpu.SMEM((2, SUBLANES, tm), I32),
                        pltpu.SMEM((2, SUBLANES, tm), F32),
                        pltpu.SemaphoreType.DMA((2,)), pltpu.SemaphoreType.DMA((2, 2))],
        compiler_params=pltpu.CompilerParams(dimension_semantics=("arbitrary",),
                                             vmem_limit_bytes=VMEM_LIMIT),
        name="combine_norm",
    )(x1, dest_t, wt, ys, g_final)


def _pad_heads(w, per_head, width=LANES):
    k = w.shape[0]
    w = w.reshape(k, HEADS, per_head)
    w = jnp.pad(w, ((0, 0), (0, 0), (0, width - per_head)))
    return w.reshape(k, HEADS * width)


def _aug_constants():
    place = np.zeros((LANES, 2 * LANES), np.float32)
    ones = np.zeros((1, 2 * LANES), np.float32)
    for hd in range(HEADS):
        for piece in range(3):
            src = FG_LO + piece * HEADS + hd
            place[src, 16 * hd + piece] = 1.0
            place[src, LANES + 16 * hd + 3 + piece] = -1.0
            ones[0, 16 * hd + 3 + piece] = 1.0
            ones[0, LANES + 16 * hd + piece] = 1.0
    return jnp.asarray(place, BF16), jnp.asarray(ones, F32)


def _layer(x2, pos2, batch, seq_len, g_attn_norm, w_in, b_fgate, g_q_a, w_q_b, g_kv_a, w_kv_b,
           w_fox_out, w_mla_out, b_merge, w_o, g_ffn_norm, w_router, b_router, w_gu, b_gu,
           w_down, b_down, g_out):
    n, d = x2.shape
    fw = HEADS * HEAD_DIM
    o = 0
    w_qf = w_in[:, o:o + fw]; o += fw
    w_kf = w_in[:, o:o + fw]; o += fw
    w_vf = w_in[:, o:o + fw]; o += fw
    w_f = w_in[:, o:o + HEADS]; o += HEADS
    w_ql = w_in[:, o:o + Q_RANK]; o += Q_RANK
    w_ckv = w_in[:, o:o + KV_RANK]; o += KV_RANK
    w_kpe = w_in[:, o:o + ROPE_DIM]; o += ROPE_DIM
    w_gate = w_in[:, o:]

    wq = (w_qf * (HEAD_DIM ** -0.5)).astype(BF16)
    wk = w_kf.astype(BF16)
    wmisc = jnp.concatenate([jnp.zeros((d, PE_LO), F32), w_kpe, w_f,
                             jnp.zeros((d, LANES - FG_LO - HEADS), F32)], axis=1)
    wlat = jnp.concatenate([w_ql, w_ckv, wmisc], axis=1).astype(BF16)
    bf128 = jnp.zeros((1, LANES), F32).at[0, FG_LO:FG_LO + HEADS].set(b_fgate)
    wqb = _pad_heads(w_q_b, HEAD_DIM + ROPE_DIM).astype(BF16)
    wkv = w_kv_b.reshape(KV_RANK, HEADS, 2 * HEAD_DIM)
    wkbk = wkv[:, :, :HEAD_DIM].reshape(KV_RANK, fw).astype(BF16)
    wkbv = wkv[:, :, HEAD_DIM:].reshape(KV_RANK, fw).T.astype(BF16)
    ltri = jnp.asarray(np.tril(np.ones((ROW_TILE, ROW_TILE), np.float32)), BF16)
    lstrict = jnp.asarray(np.tril(np.ones((MOE_TILE, MOE_TILE), np.float32), -1), BF16)
    ustrict = jnp.asarray(np.triu(np.ones((LANES, LANES), np.float32), 1), BF16)
    place, ones = _aug_constants()
    half = ROPE_DIM // 2
    inv_freq = ROPE_THETA ** (-jnp.arange(half, dtype=F32) / half)
    freq = jnp.zeros((1, LANES), F32).at[0, PE_LO:PE_MID].set(inv_freq).at[0, PE_MID:PE_HI].set(inv_freq)

    qf, kf, vf, qm, km, vm, gates = _inproj(
        x2, pos2, g_attn_norm.reshape(1, d), wq, wk, w_vf.T.astype(BF16), wlat, w_gate.astype(BF16),
        bf128, g_q_a.reshape(1, -1), wqb, g_kv_a.reshape(1, -1), wkbk, wkbv, ltri, place, ones,
        freq, seq_len=seq_len)

    y_fox = _attention(qf, kf, vf, batch=batch, seq_len=seq_len, chunk_mask=False)
    y_mla = _attention(qm, km, vm, batch=batch, seq_len=seq_len, chunk_mask=True)

    wr = jnp.pad(w_router, ((0, 0), (0, LANES - N_EXPERTS)))
    wrh = wr.astype(BF16)
    wrl = (wr - wrh.astype(F32)).astype(BF16)
    br = jnp.full((1, LANES), NEG, F32).at[0, :N_EXPERTS].set(b_router)
    n_blocks = n * TOP_K // EXPERT_ROWS + N_EXPERTS
    x1, dest_t, wt, meta, xs = _outproj(
        y_fox, y_mla, gates, x2, b_merge.reshape(1, -1), w_fox_out.astype(BF16),
        w_mla_out.astype(BF16), w_o.astype(BF16), g_ffn_norm.reshape(1, d), wrh, wrl, br, lstrict,
        ustrict, n_blocks=n_blocks)

    block_e = meta[:SUBLANES].reshape(-1)[:n_blocks]
    order = jnp.argsort(block_e, stable=True).astype(I32)
    nused = meta[SUBLANES, 0:1]
    be_sorted = jnp.minimum(block_e[order], N_EXPERTS - 1).astype(I32)
    be_sorted = jnp.where(jnp.arange(n_blocks) < nused[0], be_sorted,
                          be_sorted[jnp.maximum(nused[0] - 1, 0)])
    ys = _experts(order, be_sorted, nused, xs, w_gu, b_gu, w_down, b_down)
    return _combine(x1, dest_t, wt, ys, g_out.reshape(1, d))


def kernel(x, positions, g_attn_norm, w_in, b_fgate, g_q_a, w_q_b, g_kv_a, w_kv_b, w_fox_out, w_mla_out, b_merge, w_o, g_ffn_norm, w_router, b_router, w_gu, b_gu, w_down, b_down, g_final):
    batch, seq_len, d = x.shape
    depth = w_in.shape[0]
    assert depth == 1, "the fused combine + final-norm kernel assumes a single layer"
    assert seq_len % ATT_TILE == 0 and d % LANES == 0
    assert (batch * seq_len * TOP_K) % EXPERT_ROWS == 0
    assert batch * seq_len * TOP_K // EXPERT_ROWS + N_EXPERTS <= SUBLANES * LANES
    x2 = x.reshape(batch * seq_len, d)
    pos2 = positions.reshape(batch * seq_len, 1).astype(I32)
    out = _layer(x2, pos2, batch, seq_len, g_attn_norm[0], w_in[0], b_fgate[0], g_q_a[0], w_q_b[0],
                 g_kv_a[0], w_kv_b[0], w_fox_out[0], w_mla_out[0], b_merge[0], w_o[0],
                 g_ffn_norm[0], w_router[0], b_router[0], w_gu[0], b_gu[0], w_down[0], b_down[0],
                 g_final)
    return out.reshape(batch, seq_len, d)
```

```python
import functools

import jax
import jax.numpy as jnp
import numpy as np
from jax import lax
from jax.experimental import pallas as pl
from jax.experimental.pallas import tpu as pltpu

F32 = jnp.float32
BF16 = jnp.bfloat16
I32 = jnp.int32

LANES = 128
SUBLANES = 8
VMEM_LIMIT = 56 * 1024 * 1024

NORM_EPS = 1e-6
HEADS = 8
HEAD_DIM = 64
ROPE_DIM = 32
Q_RANK = 256
KV_RANK = 128
N_EXPERTS = 32
TOP_K = 4
EXPERT_FF = 1024
SWIGLU_LIMIT = 7.0
SWIGLU_ALPHA = 1.702
ROPE_THETA = 10000.0
CHUNK = 64

NEG = -1e30
LOG2E = 1.4426950408889634

ROW_TILE = 256
IN_TILE = 512
MOE_TILE = 512
ATT_TILE = 512
ATT_HEADS = 2
ATT_GROUP = 4
EXPERT_ROWS = MOE_TILE

PE_LO, PE_MID, PE_HI = 64, 80, 96
FG_LO = 96
AUG_LO = 64
ROUTE_DEST = 0
ROUTE_W = 8


def _dot(a, b):
    return jnp.dot(a, b, preferred_element_type=F32)


def _dot_nt(a, b):
    return lax.dot_general(a, b, (((1,), (1,)), ((), ())), preferred_element_type=F32)


def _split3(a):
    hi = a.astype(BF16)
    r1 = a - hi.astype(F32)
    mid = r1.astype(BF16)
    lo = (r1 - mid.astype(F32)).astype(BF16)
    return hi, mid, lo


def _rms(x, g):
    return x * lax.rsqrt(jnp.mean(x * x, axis=-1, keepdims=True) + NORM_EPS) * g


def _inproj_kernel(x_ref, pos_ref, g_ref, wq_ref, wk_ref, wv_ref, wlat_ref, wg_ref, bf_ref,
                   gq_ref, wqb_ref, gkv_ref, wkbk_ref, wkbv_ref, ltri_ref, place_ref, ones_ref,
                   freq_ref,
                   qf_ref, kf_ref, vf_ref, qm_ref, km_ref, vm_ref, gate_ref,
                   carry_ref, *, tiles_per_seq, mla_scale):
    i = pl.program_id(0)

    @pl.when(i % tiles_per_seq == 0)
    def _():
        carry_ref[...] = jnp.zeros_like(carry_ref)

    h = _rms(x_ref[...], g_ref[...]).astype(BF16)
    tm = h.shape[0]
    lane = lax.broadcasted_iota(I32, (tm, LANES), 1)

    lat = _dot(h, wlat_ref[...])
    q_lat = lat[:, :Q_RANK]
    c_kv = lat[:, Q_RANK:Q_RANK + KV_RANK]
    misc = lat[:, Q_RANK + KV_RANK:]

    qn = _rms(q_lat, gq_ref[...]).astype(BF16)
    kvn = _rms(c_kv, gkv_ref[...]).astype(BF16)
    qfull = _dot(qn, wqb_ref[...])
    knope = _dot(kvn, wkbk_ref[...])
    vmt = _dot_nt(wkbv_ref[...], kvn).astype(BF16)
    for c in range(tm // ROW_TILE):
        vm_ref[c] = vmt[:, c * ROW_TILE:(c + 1) * ROW_TILE]

    z = misc + bf_ref[...]
    logf = jnp.minimum(z, 0.0) - jnp.log1p(jnp.exp(-jnp.abs(z)))
    fmask = (lane >= FG_LO) & (lane < FG_LO + HEADS)
    logf = jnp.where(fmask, logf, 0.0)
    l_hi, l_mid, l_lo = _split3(logf)
    ltri = ltri_ref[...]
    c = _dot(ltri, l_hi) + _dot(ltri, l_mid) + _dot(ltri, l_lo) + carry_ref[...]
    carry_ref[...] = c[tm - 1:tm, :]
    c_hi, c_mid, c_lo = _split3(c * LOG2E)
    c3 = (c_hi.astype(F32) + pltpu.roll(c_mid.astype(F32), HEADS, 1)
          + pltpu.roll(c_lo.astype(F32), 2 * HEADS, 1)).astype(BF16)
    augc = _dot(c3, place_ref[...]) + ones_ref[...]
    low = lane < HEAD_DIM
    amask = (lane >= AUG_LO) & (lane < AUG_LO + 6)

    def aug_block(half, hd):
        src = augc[:, half * LANES:(half + 1) * LANES]
        return jnp.where(amask, pltpu.roll(src, (AUG_LO - 16 * hd) % LANES, 1), 0.0)

    def head_block(compact, hd):
        pair = compact[:, (hd // 2) * LANES:(hd // 2 + 1) * LANES]
        if hd % 2:
            pair = pltpu.roll(pair, HEAD_DIM, 1)
        return jnp.where(low, pair, 0.0)

    qc = _dot(h, wq_ref[...]) * LOG2E
    kc = _dot(h, wk_ref[...])
    vft = _dot_nt(wv_ref[...], h).astype(BF16)
    for c in range(tm // ROW_TILE):
        vf_ref[c] = vft[:, c * ROW_TILE:(c + 1) * ROW_TILE]
    gate_ref[...] = _dot(h, wg_ref[...]).astype(BF16)

    for hd in range(HEADS):
        sl = slice(hd * LANES, (hd + 1) * LANES)
        qf_ref[:, sl] = (head_block(qc, hd) + aug_block(0, hd)).astype(BF16)
        kf_ref[:, sl] = (head_block(kc, hd) + aug_block(1, hd)).astype(BF16)

    ang = pos_ref[...].astype(F32) * freq_ref[...]
    cosv = jnp.cos(ang)
    sinv = jnp.sin(ang)
    s1 = jnp.where((lane >= PE_LO) & (lane < PE_MID), -sinv, 0.0)
    s2 = jnp.where((lane >= PE_MID) & (lane < PE_HI), sinv, 0.0)

    def rope(v):
        return v * cosv + pltpu.roll(v, LANES - 16, 1) * s1 + pltpu.roll(v, 16, 1) * s2

    kpe = jnp.where((lane >= PE_LO) & (lane < PE_HI), rope(misc), 0.0)
    for hd in range(HEADS):
        sl = slice(hd * LANES, (hd + 1) * LANES)
        qm_ref[:, sl] = (rope(qfull[:, sl]) * mla_scale).astype(BF16)
        km_ref[:, sl] = (head_block(knope, hd) + kpe).astype(BF16)


def _inproj(x2, pos2, g_attn, wq, wk, wv, wlat, wg, bf128, gq, wqb, gkv, wkbk, wkbv, ltri,
            place, ones, freq, *, seq_len):
    n, d = x2.shape
    tm = IN_TILE
    hw = HEADS * LANES
    vw = HEADS * HEAD_DIM
    const = lambda i: (0, 0)
    row = lambda i: (i, 0)
    full = lambda a: pl.BlockSpec(a.shape, const)
    rows_out = lambda w: (jax.ShapeDtypeStruct((n, w), BF16), pl.BlockSpec((tm, w), row))
    vt_out = (jax.ShapeDtypeStruct((n // ROW_TILE, vw, ROW_TILE), BF16),
              pl.BlockSpec((tm // ROW_TILE, vw, ROW_TILE), lambda i: (i, 0, 0)))
    outs = [rows_out(hw), rows_out(hw), vt_out, rows_out(hw), rows_out(hw), vt_out,
            rows_out(wg.shape[1])]
    consts = (g_attn, wq, wk, wv, wlat, wg, bf128, gq, wqb, gkv, wkbk, wkbv, ltri, place, ones, freq)
    return pl.pallas_call(
        functools.partial(_inproj_kernel, tiles_per_seq=seq_len // tm,
                          mla_scale=float((HEAD_DIM + ROPE_DIM) ** -0.5) * LOG2E),
        out_shape=[o[0] for o in outs],
        grid=(n // tm,),
        in_specs=[pl.BlockSpec((tm, d), row), pl.BlockSpec((tm, 1), row)] + [full(a) for a in consts],
        out_specs=[o[1] for o in outs],
        scratch_shapes=[pltpu.VMEM((1, LANES), F32)],
        compiler_params=pltpu.CompilerParams(dimension_semantics=("arbitrary",),
                                             vmem_limit_bytes=VMEM_LIMIT),
        name="inproj",
    )(x2, pos2, *consts)


def _attn_kernel(q_ref, k_ref, vt_ref, o_ref, *, chunk_mask):
    i = pl.program_id(2)
    tq = q_ref.shape[0]
    tk = vt_ref.shape[2]

    def scores(j, masked, lo=0):
        start = pl.multiple_of(j * tk, tk)
        ss = [_dot_nt(k_ref[pl.ds(start, tk), hd * LANES:(hd + 1) * LANES],
                      q_ref[lo:, hd * LANES:(hd + 1) * LANES])
              for hd in range(ATT_HEADS)]
        if masked:
            keyg = lax.broadcasted_iota(I32, (tk, tq - lo), 0) + j * tk
            qlo = lax.broadcasted_iota(I32, (tk, tq - lo), 1) + (i * tq + lo)
            allowed = (keyg // CHUNK) <= (qlo // CHUNK) if chunk_mask else keyg <= qlo
            ss = [jnp.where(allowed, s, NEG) for s in ss]
        return ss

    ones = jnp.ones((2 * SUBLANES, tk), BF16)

    def update(j, ss, state, lo=0):
        vt = vt_ref[j]
        new = []
        for hd, s in enumerate(ss):
            m_all, acc_all = state[hd]
            m, acc = m_all[:, lo:], acc_all[:, lo:]
            m_new = jnp.maximum(m, jnp.max(s, axis=0, keepdims=True))
            alpha = jnp.exp2(m - m_new)
            p = jnp.exp2((s - m_new[0:1]).astype(BF16))
            va = jnp.concatenate([vt[hd * HEAD_DIM:(hd + 1) * HEAD_DIM, :], ones], axis=0)
            acc = alpha[0:1] * acc + _dot(va, p)
            if lo:
                m_new = jnp.concatenate([m_all[:, :lo], m_new], axis=1)
                acc = jnp.concatenate([acc_all[:, :lo], acc], axis=1)
            new.append((m_new, acc))
        return tuple(new)

    n_full = (i * tq) // tk
    group = max(1, tq // tk)

    def run_group(first, state, n_plain, n_masked=0):
        los = [0] * n_plain + [u * tk for u in range(n_masked)]
        msk = [False] * n_plain + [True] * n_masked
        nblk = n_plain + n_masked
        ss = scores(first, msk[0], los[0])
        for u in range(nblk):
            nxt = scores(first + u + 1, msk[u + 1], los[u + 1]) if u + 1 < nblk else None
            state = update(first + u, ss, state, los[u])
            ss = nxt
        return state

    long = ATT_GROUP * group
    n_long = n_full // long
    init1 = (jnp.full((SUBLANES, tq), NEG, F32), jnp.zeros((HEAD_DIM + 2 * SUBLANES, tq), F32))
    state = lax.fori_loop(0, n_long, lambda jj, c: run_group(jj * long, c, long),
                          (init1,) * ATT_HEADS)
    done = n_long * long
    left = (n_full - done) // group
    tails = [functools.partial(run_group, done, n_plain=r * group, n_masked=group)
             for r in range(ATT_GROUP)]

    def pick(lo, hi, st):
        if hi - lo == 1:
            return tails[lo](st)
        mid = (lo + hi) // 2
        return lax.cond(left < mid, lambda s: pick(lo, mid, s), lambda s: pick(mid, hi, s), st)

    state = pick(0, ATT_GROUP, state)
    out_t = jnp.concatenate([acc[:HEAD_DIM] / acc[HEAD_DIM:HEAD_DIM + 1] for _, acc in state],
                            axis=0)
    o_ref[...] = out_t.T.astype(o_ref.dtype)


def _attention(q, k, vt, *, batch, seq_len, chunk_mask):
    n = q.shape[0]
    t = ATT_TILE
    tk = vt.shape[2]
    nq = seq_len // t
    nkb = seq_len // tk
    hp = ATT_HEADS
    pairs = HEADS // hp
    return pl.pallas_call(
        functools.partial(_attn_kernel, chunk_mask=chunk_mask),
        out_shape=jax.ShapeDtypeStruct((n, HEADS * HEAD_DIM), BF16),
        grid=(batch, pairs, nq),
        in_specs=[pl.BlockSpec((t, hp * LANES), lambda b, p, i: (b * nq + i, p)),
                  pl.BlockSpec((seq_len, hp * LANES), lambda b, p, i: (b, p)),
                  pl.BlockSpec((nkb, hp * HEAD_DIM, tk), lambda b, p, i: (b, p, 0))],
        out_specs=pl.BlockSpec((t, hp * HEAD_DIM), lambda b, p, i: (b * nq + i, p)),
        compiler_params=pltpu.CompilerParams(
            dimension_semantics=("arbitrary", "arbitrary", "arbitrary"),
            vmem_limit_bytes=VMEM_LIMIT),
        name="attn_mla" if chunk_mask else "attn_fox",
    )(q, k, vt)


def _outproj_kernel(yf_ref, ym_ref, gate_ref, x_ref, bm_ref, wfo_ref, wmo_ref, wo_ref, gffn_ref,
                    wrh_ref, wrl_ref, br_ref, lstrict_ref, ustrict_ref,
                    x1_ref, route_ref, dest_ref, meta_ref, xs_hbm,
                    fill_ref, cur_ref, nfree_ref, tbl_ref,
                    hbuf, dbuf, dsm, zbuf, mbuf, msm, ssem, isem, zsem, msem):
    i = pl.program_id(0)
    nt = pl.num_programs(0)
    d = x_ref.shape[1]
    tm = x_ref.shape[0]
    sub = d // LANES
    rows = EXPERT_ROWS
    n_blocks = xs_hbm.shape[0] // (rows * sub)
    slot = i % 2

    def idx_copy(s):
        return pltpu.make_async_copy(dbuf.at[s], dsm.at[s], isem.at[s])

    def start_scatters(s):
        for static_s in range(2):
            @pl.when(s == static_s)
            def _(static_s=static_s):
                def body(t, _):
                    src = hbuf.at[static_s, pl.ds(pl.multiple_of(t * sub, sub), sub), :]
                    for k in range(TOP_K):
                        dst = dsm[static_s, k, t]
                        pltpu.make_async_copy(
                            src, xs_hbm.at[pl.ds(pl.multiple_of(dst, sub), sub), :],
                            ssem.at[static_s]).start(priority=k % 2)
                    return 0

                lax.fori_loop(0, tm, body, 0, unroll=4)

    def wait_scatters(s):
        for _ in range(TOP_K):
            pltpu.make_async_copy(hbuf.at[s], xs_hbm.at[pl.ds(0, tm * sub), :], ssem.at[s]).wait()

    @pl.when(i == 0)
    def _():
        fill_ref[...] = jnp.full_like(fill_ref, float(rows))
        cur_ref[...] = jnp.zeros_like(cur_ref)
        nfree_ref[...] = jnp.zeros_like(nfree_ref)
        tbl_ref[...] = jnp.full_like(tbl_ref, float(N_EXPERTS))
        zbuf[...] = jnp.zeros_like(zbuf)

    @pl.when(i > 0)
    def _():
        idx_copy(1 - slot).wait()
        start_scatters(1 - slot)

    a = _dot(yf_ref[...], wfo_ref[...])
    b = _dot(ym_ref[...], wmo_ref[...])
    g = 1.0 / (1.0 + jnp.exp(-(gate_ref[...].astype(F32) + bm_ref[...])))
    merged = (g[:, :d] * a + g[:, d:] * b).astype(BF16)
    x1 = x_ref[...] + _dot(merged, wo_ref[...])
    x1_ref[...] = x1
    h2 = _rms(x1, gffn_ref[...])

    hi = h2.astype(BF16)
    lo = (h2 - hi.astype(F32)).astype(BF16)
    wrh = wrh_ref[...]
    logits = _dot(hi, wrh) + _dot(lo, wrh) + _dot(hi, wrl_ref[...]) + br_ref[...]

    lane = lax.broadcasted_iota(I32, (tm, LANES), 1)
    vals = logits
    sels, tops = [], []
    for _ in range(TOP_K):
        mx = jnp.max(vals, axis=-1, keepdims=True)
        idx = jnp.min(jnp.where(vals == mx, lane, LANES), axis=-1, keepdims=True)
        sel = lane == idx
        vals = jnp.where(sel, NEG, vals)
        sels.append(sel)
        tops.append(mx)
    es = [jnp.exp(tv - tops[0]) for tv in tops]
    den = es[0] + es[1] + es[2] + es[3]

    onehot = jnp.zeros((tm, LANES), F32)
    for sel in sels:
        onehot = onehot + sel.astype(F32)
    before = _dot(lstrict_ref[...], onehot.astype(BF16))
    cnt = jnp.sum(onehot, axis=0, keepdims=True)

    fill = fill_ref[...]
    cur = cur_ref[...]
    nfree = nfree_ref[...]
    need = ((fill + cnt) > float(rows)).astype(F32)
    need8 = jnp.broadcast_to(need, (SUBLANES, LANES)).astype(BF16)
    newid = nfree + _dot(need8, ustrict_ref[...])[0:1, :]
    pos = fill + before
    dest = jnp.where(pos < float(rows), cur * rows + pos, newid * rows + pos - float(rows))
    fill_ref[...] = fill + cnt - need * float(rows)
    cur_ref[...] = jnp.where(need > 0, newid, cur)
    nfree_ref[...] = nfree + jnp.sum(need, axis=-1, keepdims=True)
    blk_id = (lax.broadcasted_iota(I32, (SUBLANES, LANES), 0) * LANES
              + lax.broadcasted_iota(I32, (SUBLANES, LANES), 1)).astype(F32)
    tbl = tbl_ref[...]
    for e in range(N_EXPERTS):
        hit = (blk_id == newid[:, e:e + 1]) & (need[:, e:e + 1] > 0)
        tbl = jnp.where(hit, float(e), tbl)
    tbl_ref[...] = tbl

    route = jnp.zeros((tm, LANES), F32)
    for k in range(TOP_K):
        dest_k = jnp.sum(jnp.where(sels[k], dest, 0.0), axis=-1, keepdims=True)
        route = jnp.where(lane == ROUTE_DEST + k, dest_k, route)
        route = jnp.where(lane == ROUTE_W + k, es[k] / den, route)
    route_ref[...] = route
    dest_t = (route.T[0:SUBLANES, :] * float(sub)).astype(I32)
    dest_ref[0] = dest_t

    @pl.when(i >= 2)
    def _():
        wait_scatters(slot)

    for s in range(sub):
        hbuf[slot, pl.ds(s, tm, stride=sub), :] = h2[:, s * LANES:(s + 1) * LANES]
    dbuf[slot] = dest_t
    idx_copy(slot).start()

    @pl.when(i == nt - 1)
    def _():
        idx_copy(slot).wait()
        start_scatters(slot)

        @pl.when(i >= 1)
        def _():
            wait_scatters(1 - slot)

        wait_scatters(slot)

        meta = jnp.concatenate([tbl_ref[...], jnp.broadcast_to(nfree_ref[...], (SUBLANES, LANES))],
                               axis=0).astype(I32)
        meta_ref[...] = meta
        state = jnp.concatenate([fill_ref[...], cur_ref[...], nfree_ref[...],
                                 jnp.zeros((SUBLANES - 3, LANES), F32)], axis=0).astype(I32)
        mbuf[...] = state
        mcopy = pltpu.make_async_copy(mbuf, msm, msem.at[0])
        mcopy.start()
        mcopy.wait()

        def zero_copy(first_row, n_rows):
            return pltpu.make_async_copy(
                zbuf.at[pl.ds(0, n_rows * sub), :],
                xs_hbm.at[pl.ds(pl.multiple_of(first_row * sub, sub), n_rows * sub), :], zsem.at[0])

        chunks = [rows >> (s + 1) for s in range(rows.bit_length() - 1)]
        plans = []
        for e in range(N_EXPERTS):
            rem = rows - msm[0, e]
            at = msm[1, e] * rows + msm[0, e]
            for c in chunks:
                take = (rem & c) != 0
                plans.append((take, zero_copy(at, c)))
                at = at + jnp.where(take, c, 0)
        for j in range(N_EXPERTS):
            blk = msm[2, 0] + j
            safe = jnp.minimum(blk, n_blocks - 1)
            plans.append((blk < n_blocks, zero_copy(safe * rows, rows)))
        for take, cp in plans:
            pl.when(take)(cp.start)
        for take, cp in plans:
            pl.when(take)(cp.wait)


def _outproj(yf, ym, gates, x2, bm, wfo, wmo, wo, gffn, wrh, wrl, br, lstrict, ustrict, *, n_blocks):
    n, d = x2.shape
    tm = MOE_TILE
    rows = EXPERT_ROWS
    sub = d // LANES
    const = lambda i: (0, 0)
    row = lambda i: (i, 0)
    full = lambda a: pl.BlockSpec(a.shape, const)
    consts = (bm, wfo, wmo, wo, gffn, wrh, wrl, br, lstrict, ustrict)
    return pl.pallas_call(
        _outproj_kernel,
        out_shape=[jax.ShapeDtypeStruct((n, d), F32),
                   jax.ShapeDtypeStruct((n, LANES), F32),
                   jax.ShapeDtypeStruct((n // tm, SUBLANES, tm), I32),
                   jax.ShapeDtypeStruct((2 * SUBLANES, LANES), I32),
                   jax.ShapeDtypeStruct((n_blocks * rows * sub, LANES), F32)],
        grid=(n // tm,),
        in_specs=[pl.BlockSpec((tm, yf.shape[1]), row), pl.BlockSpec((tm, ym.shape[1]), row),
                  pl.BlockSpec((tm, gates.shape[1]), row), pl.BlockSpec((tm, d), row)]
                 + [full(a) for a in consts],
        out_specs=[pl.BlockSpec((tm, d), row), pl.BlockSpec((tm, LANES), row),
                   pl.BlockSpec((1, SUBLANES, tm), lambda i: (i, 0, 0)),
                   pl.BlockSpec((2 * SUBLANES, LANES), const),
                   pl.BlockSpec(memory_space=pl.ANY)],
        scratch_shapes=[pltpu.VMEM((1, LANES), F32), pltpu.VMEM((1, LANES), F32),
                        pltpu.VMEM((1, LANES), F32), pltpu.VMEM((SUBLANES, LANES), F32),
                        pltpu.VMEM((2, tm * sub, LANES), F32),
                        pltpu.VMEM((2, SUBLANES, tm), I32),
                        pltpu.SMEM((2, SUBLANES, tm), I32),
                        pltpu.VMEM((rows * sub, LANES), F32),
                        pltpu.VMEM((SUBLANES, LANES), I32),
                        pltpu.SMEM((SUBLANES, LANES), I32),
                        pltpu.SemaphoreType.DMA((2,)), pltpu.SemaphoreType.DMA((2,)),
                        pltpu.SemaphoreType.DMA((1,)), pltpu.SemaphoreType.DMA((1,))],
        compiler_params=pltpu.CompilerParams(dimension_semantics=("arbitrary",),
                                             vmem_limit_bytes=VMEM_LIMIT),
        name="outproj_router",
    )(yf, ym, gates, x2, *consts)


def _expert_kernel(order_ref, be_ref, nused_ref,
                   xs_ref, wgu_ref, bgu_ref, wd_ref, bd_ref, ys_ref, wgu_b, wd_b):
    b = pl.program_id(0)
    rows = EXPERT_ROWS
    sub = xs_ref.shape[0] // rows

    @pl.when(b < nused_ref[0])
    def _():
        changed = jnp.logical_or(b == 0, be_ref[b] != be_ref[jnp.maximum(b - 1, 0)])

        @pl.when(changed)
        def _():
            wgu_b[...] = wgu_ref[0].astype(BF16)
            wd_b[...] = wd_ref[0].astype(BF16)

        x = jnp.concatenate(
            [xs_ref[pl.ds(s, rows, stride=sub), :].astype(BF16) for s in range(sub)], axis=1)
        gu = _dot(x, wgu_b[...]) + bgu_ref[0]
        gate = jnp.minimum(gu[:, :EXPERT_FF], SWIGLU_LIMIT)
        up = jnp.clip(gu[:, EXPERT_FF:], -SWIGLU_LIMIT, SWIGLU_LIMIT)
        glu = gate * (1.0 / (1.0 + jnp.exp(-SWIGLU_ALPHA * gate)))
        y = _dot(((up + 1.0) * glu).astype(BF16), wd_b[...]) + bd_ref[0]
        for s in range(sub):
            ys_ref[pl.ds(s, rows, stride=sub), :] = y[:, s * LANES:(s + 1) * LANES]

    @pl.when(b >= nused_ref[0])
    def _():
        ys_ref[...] = jnp.zeros_like(ys_ref)


def _experts(order, block_e, nused, xs, w_gu, b_gu, w_d, b_d):
    e, d, ff2 = w_gu.shape
    rows = EXPERT_ROWS
    sub = d // LANES
    n_blocks = xs.shape[0] // (rows * sub)
    wmap = lambda b, od, be, nu: (be[b], 0, 0)
    xmap = lambda b, od, be, nu: (od[b], 0)
    grid_spec = pltpu.PrefetchScalarGridSpec(
        num_scalar_prefetch=3,
        grid=(n_blocks,),
        in_specs=[pl.BlockSpec((rows * sub, LANES), xmap),
                  pl.BlockSpec((1, d, ff2), wmap), pl.BlockSpec((1, 1, ff2), wmap),
                  pl.BlockSpec((1, ff2 // 2, d), wmap), pl.BlockSpec((1, 1, d), wmap)],
        out_specs=pl.BlockSpec((rows * sub, LANES), xmap),
        scratch_shapes=[pltpu.VMEM((d, ff2), BF16), pltpu.VMEM((ff2 // 2, d), BF16)],
    )
    return pl.pallas_call(
        _expert_kernel,
        out_shape=jax.ShapeDtypeStruct(xs.shape, F32),
        grid_spec=grid_spec,
        compiler_params=pltpu.CompilerParams(dimension_semantics=("arbitrary",),
                                             vmem_limit_bytes=VMEM_LIMIT),
        name="experts",
    )(order, block_e, nused, xs, w_gu, b_gu.reshape(e, 1, ff2), w_d, b_d.reshape(e, 1, d))


def _combine_kernel(x1_ref, route_ref, dest_hbm, ys_hbm, g_ref, o_ref, ybuf, dsm, gsem, isem):
    i = pl.program_id(0)
    nt = pl.num_programs(0)
    tm, d = x1_ref.shape
    sub = d // LANES
    slot = i % 2

    def idx_copy(tile):
        s = tile % 2
        return pltpu.make_async_copy(dest_hbm.at[tile], dsm.at[s], isem.at[s])

    def start_gathers(tile):
        for static_s in range(2):
            @pl.when(tile % 2 == static_s)
            def _(static_s=static_s):
                def body(t, _):
                    for k in range(TOP_K):
                        src = dsm[static_s, k, t]
                        pltpu.make_async_copy(
                            ys_hbm.at[pl.ds(pl.multiple_of(src, sub), sub), :],
                            ybuf.at[static_s, pl.ds(pl.multiple_of((k * tm + t) * sub, sub), sub), :],
                            gsem.at[static_s]).start(priority=k % 2)
                    return 0

                lax.fori_loop(0, tm, body, 0, unroll=4)

    def wait_gathers(s):
        pltpu.make_async_copy(ys_hbm.at[pl.ds(0, TOP_K * tm * sub), :], ybuf.at[s], gsem.at[s]).wait()

    @pl.when(i == 0)
    def _():
        idx_copy(0).start()
        idx_copy(0).wait()
        start_gathers(0)

        @pl.when(nt > 1)
        def _():
            idx_copy(1).start()

    @pl.when(i + 1 < nt)
    def _():
        idx_copy(i + 1).wait()
        start_gathers(i + 1)

    @pl.when(i + 2 < nt)
    def _():
        idx_copy(i + 2).start()

    wait_gathers(slot)
    route = route_ref[...]
    ws = [route[:, ROUTE_W + k:ROUTE_W + k + 1] for k in range(TOP_K)]
    chunks = []
    ssq = jnp.zeros((tm, 1), F32)
    for s in range(sub):
        acc = x1_ref[:, s * LANES:(s + 1) * LANES]
        for k in range(TOP_K):
            acc = acc + ws[k] * ybuf[slot, pl.ds(k * tm * sub + s, tm, stride=sub), :]
        chunks.append(acc)
        ssq = ssq + jnp.sum(acc * acc, axis=-1, keepdims=True)
    inv = lax.rsqrt(ssq / d + NORM_EPS)
    for s in range(sub):
        sl = slice(s * LANES, (s + 1) * LANES)
        o_ref[:, sl] = chunks[s] * inv * g_ref[:, sl]


def _combine(x1, route, dest_t, ys, g_final):
    n, d = x1.shape
    tm = MOE_TILE
    sub = d // LANES
    row = lambda i: (i, 0)
    return pl.pallas_call(
        _combine_kernel,
        out_shape=jax.ShapeDtypeStruct((n, d), F32),
        grid=(n // tm,),
        in_specs=[pl.BlockSpec((tm, d), row), pl.BlockSpec((tm, LANES), row),
                  pl.BlockSpec(memory_space=pl.ANY), pl.BlockSpec(memory_space=pl.ANY),
                  pl.BlockSpec((1, d), lambda i: (0, 0))],
        out_specs=pl.BlockSpec((tm, d), row),
        scratch_shapes=[pltpu.VMEM((2, TOP_K * tm * sub, LANES), F32),
                        pltpu.SMEM((2, SUBLANES, tm), I32),
                        pltpu.SemaphoreType.DMA((2,)), pltpu.SemaphoreType.DMA((2,))],
        compiler_params=pltpu.CompilerParams(dimension_semantics=("arbitrary",),
                                             vmem_limit_bytes=VMEM_LIMIT),
        name="combine_norm",
    )(x1, route, dest_t, ys, g_final)


def _pad_heads(w, per_head, width=LANES):
    k = w.shape[0]
    w = w.reshape(k, HEADS, per_head)
    w = jnp.pad(w, ((0, 0), (0, 0), (0, width - per_head)))
    return w.reshape(k, HEADS * width)


def _aug_constants():
    place = np.zeros((LANES, 2 * LANES), np.float32)
    ones = np.zeros((1, 2 * LANES), np.float32)
    for hd in range(HEADS):
        for piece in range(3):
            src = FG_LO + piece * HEADS + hd
            place[src, 16 * hd + piece] = 1.0
            place[src, LANES + 16 * hd + 3 + piece] = -1.0
            ones[0, 16 * hd + 3 + piece] = 1.0
            ones[0, LANES + 16 * hd + piece] = 1.0
    return jnp.asarray(place, BF16), jnp.asarray(ones, F32)


def _layer(x2, pos2, batch, seq_len, g_attn_norm, w_in, b_fgate, g_q_a, w_q_b, g_kv_a, w_kv_b,
           w_fox_out, w_mla_out, b_merge, w_o, g_ffn_norm, w_router, b_router, w_gu, b_gu,
           w_down, b_down, g_out):
    n, d = x2.shape
    fw = HEADS * HEAD_DIM
    o = 0
    w_qf = w_in[:, o:o + fw]; o += fw
    w_kf = w_in[:, o:o + fw]; o += fw
    w_vf = w_in[:, o:o + fw]; o += fw
    w_f = w_in[:, o:o + HEADS]; o += HEADS
    w_ql = w_in[:, o:o + Q_RANK]; o += Q_RANK
    w_ckv = w_in[:, o:o + KV_RANK]; o += KV_RANK
    w_kpe = w_in[:, o:o + ROPE_DIM]; o += ROPE_DIM
    w_gate = w_in[:, o:]

    wq = (w_qf * (HEAD_DIM ** -0.5)).astype(BF16)
    wk = w_kf.astype(BF16)
    wmisc = jnp.concatenate([jnp.zeros((d, PE_LO), F32), w_kpe, w_f,
                             jnp.zeros((d, LANES - FG_LO - HEADS), F32)], axis=1)
    wlat = jnp.concatenate([w_ql, w_ckv, wmisc], axis=1).astype(BF16)
    bf128 = jnp.zeros((1, LANES), F32).at[0, FG_LO:FG_LO + HEADS].set(b_fgate)
    wqb = _pad_heads(w_q_b, HEAD_DIM + ROPE_DIM).astype(BF16)
    wkv = w_kv_b.reshape(KV_RANK, HEADS, 2 * HEAD_DIM)
    wkbk = wkv[:, :, :HEAD_DIM].reshape(KV_RANK, fw).astype(BF16)
    wkbv = wkv[:, :, HEAD_DIM:].reshape(KV_RANK, fw).T.astype(BF16)
    ltri = jnp.asarray(np.tril(np.ones((IN_TILE, IN_TILE), np.float32)), BF16)
    lstrict = jnp.asarray(np.tril(np.ones((MOE_TILE, MOE_TILE), np.float32), -1), BF16)
    ustrict = jnp.asarray(np.triu(np.ones((LANES, LANES), np.float32), 1), BF16)
    place, ones = _aug_constants()
    half = ROPE_DIM // 2
    inv_freq = ROPE_THETA ** (-jnp.arange(half, dtype=F32) / half)
    freq = jnp.zeros((1, LANES), F32).at[0, PE_LO:PE_MID].set(inv_freq).at[0, PE_MID:PE_HI].set(inv_freq)

    qf, kf, vf, qm, km, vm, gates = _inproj(
        x2, pos2, g_attn_norm.reshape(1, d), wq, wk, w_vf.T.astype(BF16), wlat, w_gate.astype(BF16),
        bf128, g_q_a.reshape(1, -1), wqb, g_kv_a.reshape(1, -1), wkbk, wkbv, ltri, place, ones,
        freq, seq_len=seq_len)

    y_fox = _attention(qf, kf, vf, batch=batch, seq_len=seq_len, chunk_mask=False)
    y_mla = _attention(qm, km, vm, batch=batch, seq_len=seq_len, chunk_mask=True)

    wr = jnp.pad(w_router, ((0, 0), (0, LANES - N_EXPERTS)))
    wrh = wr.astype(BF16)
    wrl = (wr - wrh.astype(F32)).astype(BF16)
    br = jnp.full((1, LANES), NEG, F32).at[0, :N_EXPERTS].set(b_router)
    n_blocks = n * TOP_K // EXPERT_ROWS + N_EXPERTS
    x1, route, dest_t, meta, xs = _outproj(
        y_fox, y_mla, gates, x2, b_merge.reshape(1, -1), w_fox_out.astype(BF16),
        w_mla_out.astype(BF16), w_o.astype(BF16), g_ffn_norm.reshape(1, d), wrh, wrl, br, lstrict,
        ustrict, n_blocks=n_blocks)

    block_e = meta[:SUBLANES].reshape(-1)[:n_blocks]
    order = jnp.argsort(block_e, stable=True).astype(I32)
    nused = meta[SUBLANES, 0:1]
    be_sorted = jnp.minimum(block_e[order], N_EXPERTS - 1).astype(I32)
    be_sorted = jnp.where(jnp.arange(n_blocks) < nused[0], be_sorted,
                          be_sorted[jnp.maximum(nused[0] - 1, 0)])
    ys = _experts(order, be_sorted, nused, xs, w_gu, b_gu, w_down, b_down)
    return _combine(x1, route, dest_t, ys, g_out.reshape(1, d))


def kernel(x, positions, g_attn_norm, w_in, b_fgate, g_q_a, w_q_b, g_kv_a, w_kv_b, w_fox_out, w_mla_out, b_merge, w_o, g_ffn_norm, w_router, b_router, w_gu, b_gu, w_down, b_down, g_final):
    batch, seq_len, d = x.shape
    depth = w_in.shape[0]
    assert depth == 1, "the fused combine + final-norm kernel assumes a single layer"
    assert seq_len % ATT_TILE == 0 and d % LANES == 0
    assert (batch * seq_len * TOP_K) % EXPERT_ROWS == 0
    assert batch * seq_len * TOP_K // EXPERT_ROWS + N_EXPERTS <= SUBLANES * LANES
    x2 = x.reshape(batch * seq_len, d)
    pos2 = positions.reshape(batch * seq_len, 1).astype(I32)
    out = _layer(x2, pos2, batch, seq_len, g_attn_norm[0], w_in[0], b_fgate[0], g_q_a[0], w_q_b[0],
                 g_kv_a[0], w_kv_b[0], w_fox_out[0], w_mla_out[0], b_merge[0], w_o[0],
                 g_ffn_norm[0], w_router[0], b_router[0], w_gu[0], b_gu[0], w_down[0], b_down[0],
                 g_final)
    return out.reshape(batch, seq_len, d)
```

```python
import functools

import jax
import jax.numpy as jnp
import numpy as np
from jax import lax
from jax.experimental import pallas as pl
from jax.experimental.pallas import tpu as pltpu

F32 = jnp.float32
BF16 = jnp.bfloat16
I32 = jnp.int32

LANES = 128
SUBLANES = 8
VMEM_LIMIT = 56 * 1024 * 1024

NORM_EPS = 1e-6
HEADS = 8
HEAD_DIM = 64
ROPE_DIM = 32
Q_RANK = 256
KV_RANK = 128
N_EXPERTS = 32
TOP_K = 4
EXPERT_FF = 1024
SWIGLU_LIMIT = 7.0
SWIGLU_ALPHA = 1.702
ROPE_THETA = 10000.0
CHUNK = 64

NEG = -1e30
LOG2E = 1.4426950408889634

ROW_TILE = 256
IN_TILE = 512
MOE_TILE = 512
ATT_TILE = 512
ATT_HEADS = 2
ATT_GROUP = 4
EXPERT_ROWS = MOE_TILE

PE_LO, PE_MID, PE_HI = 64, 80, 96
FG_LO = 96
AUG_LO = 64
ROUTE_DEST = 0
ROUTE_W = 8


def _dot(a, b):
    return jnp.dot(a, b, preferred_element_type=F32)


def _dot_nt(a, b):
    return lax.dot_general(a, b, (((1,), (1,)), ((), ())), preferred_element_type=F32)


def _split3(a):
    hi = a.astype(BF16)
    r1 = a - hi.astype(F32)
    mid = r1.astype(BF16)
    lo = (r1 - mid.astype(F32)).astype(BF16)
    return hi, mid, lo


def _rms(x, g):
    return x * lax.rsqrt(jnp.mean(x * x, axis=-1, keepdims=True) + NORM_EPS) * g


def _inproj_kernel(x_ref, pos_ref, g_ref, wq_ref, wk_ref, wv_ref, wlat_ref, wg_ref, bf_ref,
                   gq_ref, wqb_ref, gkv_ref, wkbk_ref, wkbv_ref, ltri_ref, place_ref, ones_ref,
                   freq_ref,
                   qf_ref, kf_ref, vf_ref, qm_ref, km_ref, vm_ref, gate_ref,
                   carry_ref, *, tiles_per_seq, mla_scale):
    i = pl.program_id(0)

    @pl.when(i % tiles_per_seq == 0)
    def _():
        carry_ref[...] = jnp.zeros_like(carry_ref)

    h = _rms(x_ref[...], g_ref[...]).astype(BF16)
    tm = h.shape[0]
    lane = lax.broadcasted_iota(I32, (tm, LANES), 1)

    lat = _dot(h, wlat_ref[...])
    q_lat = lat[:, :Q_RANK]
    c_kv = lat[:, Q_RANK:Q_RANK + KV_RANK]
    misc = lat[:, Q_RANK + KV_RANK:]

    qn = _rms(q_lat, gq_ref[...]).astype(BF16)
    kvn = _rms(c_kv, gkv_ref[...]).astype(BF16)
    qfull = _dot(qn, wqb_ref[...])
    knope = _dot(kvn, wkbk_ref[...])
    vmt = _dot_nt(wkbv_ref[...], kvn).astype(BF16)
    for c in range(tm // ROW_TILE):
        vm_ref[c] = vmt[:, c * ROW_TILE:(c + 1) * ROW_TILE]

    z = misc + bf_ref[...]
    logf = jnp.minimum(z, 0.0) - jnp.log1p(jnp.exp(-jnp.abs(z)))
    fmask = (lane >= FG_LO) & (lane < FG_LO + HEADS)
    logf = jnp.where(fmask, logf, 0.0)
    l_hi, l_mid, l_lo = _split3(logf)
    ltri = ltri_ref[...]
    c = _dot(ltri, l_hi) + _dot(ltri, l_mid) + _dot(ltri, l_lo) + carry_ref[...]
    carry_ref[...] = c[tm - 1:tm, :]
    c_hi, c_mid, c_lo = _split3(c * LOG2E)
    c3 = (c_hi.astype(F32) + pltpu.roll(c_mid.astype(F32), HEADS, 1)
          + pltpu.roll(c_lo.astype(F32), 2 * HEADS, 1)).astype(BF16)
    augc = _dot(c3, place_ref[...]) + ones_ref[...]
    low = lane < HEAD_DIM
    amask = (lane >= AUG_LO) & (lane < AUG_LO + 6)

    def aug_block(half, hd):
        src = augc[:, half * LANES:(half + 1) * LANES]
        return jnp.where(amask, pltpu.roll(src, (AUG_LO - 16 * hd) % LANES, 1), 0.0)

    def head_block(compact, hd):
        pair = compact[:, (hd // 2) * LANES:(hd // 2 + 1) * LANES]
        if hd % 2:
            pair = pltpu.roll(pair, HEAD_DIM, 1)
        return jnp.where(low, pair, 0.0)

    qc = _dot(h, wq_ref[...]) * LOG2E
    kc = _dot(h, wk_ref[...])
    vft = _dot_nt(wv_ref[...], h).astype(BF16)
    for c in range(tm // ROW_TILE):
        vf_ref[c] = vft[:, c * ROW_TILE:(c + 1) * ROW_TILE]
    gate_ref[...] = _dot(h, wg_ref[...]).astype(BF16)

    for hd in range(HEADS):
        sl = slice(hd * LANES, (hd + 1) * LANES)
        qf_ref[:, sl] = (head_block(qc, hd) + aug_block(0, hd)).astype(BF16)
        kf_ref[:, sl] = (head_block(kc, hd) + aug_block(1, hd)).astype(BF16)

    ang = pos_ref[...].astype(F32) * freq_ref[...]
    cosv = jnp.cos(ang)
    sinv = jnp.sin(ang)
    s1 = jnp.where((lane >= PE_LO) & (lane < PE_MID), -sinv, 0.0)
    s2 = jnp.where((lane >= PE_MID) & (lane < PE_HI), sinv, 0.0)

    def rope(v):
        return v * cosv + pltpu.roll(v, LANES - 16, 1) * s1 + pltpu.roll(v, 16, 1) * s2

    kpe = jnp.where((lane >= PE_LO) & (lane < PE_HI), rope(misc), 0.0)
    for hd in range(HEADS):
        sl = slice(hd * LANES, (hd + 1) * LANES)
        qm_ref[:, sl] = (rope(qfull[:, sl]) * mla_scale).astype(BF16)
        km_ref[:, sl] = (head_block(knope, hd) + kpe).astype(BF16)


def _inproj(x2, pos2, g_attn, wq, wk, wv, wlat, wg, bf128, gq, wqb, gkv, wkbk, wkbv, ltri,
            place, ones, freq, *, seq_len):
    n, d = x2.shape
    tm = IN_TILE
    hw = HEADS * LANES
    vw = HEADS * HEAD_DIM
    const = lambda i: (0, 0)
    row = lambda i: (i, 0)
    full = lambda a: pl.BlockSpec(a.shape, const)
    rows_out = lambda w: (jax.ShapeDtypeStruct((n, w), BF16), pl.BlockSpec((tm, w), row))
    vt_out = (jax.ShapeDtypeStruct((n // ROW_TILE, vw, ROW_TILE), BF16),
              pl.BlockSpec((tm // ROW_TILE, vw, ROW_TILE), lambda i: (i, 0, 0)))
    outs = [rows_out(hw), rows_out(hw), vt_out, rows_out(hw), rows_out(hw), vt_out,
            rows_out(wg.shape[1])]
    consts = (g_attn, wq, wk, wv, wlat, wg, bf128, gq, wqb, gkv, wkbk, wkbv, ltri, place, ones, freq)
    return pl.pallas_call(
        functools.partial(_inproj_kernel, tiles_per_seq=seq_len // tm,
                          mla_scale=float((HEAD_DIM + ROPE_DIM) ** -0.5) * LOG2E),
        out_shape=[o[0] for o in outs],
        grid=(n // tm,),
        in_specs=[pl.BlockSpec((tm, d), row), pl.BlockSpec((tm, 1), row)] + [full(a) for a in consts],
        out_specs=[o[1] for o in outs],
        scratch_shapes=[pltpu.VMEM((1, LANES), F32)],
        compiler_params=pltpu.CompilerParams(dimension_semantics=("arbitrary",),
                                             vmem_limit_bytes=VMEM_LIMIT),
        name="inproj",
    )(x2, pos2, *consts)


def _attn_kernel(q_ref, k_ref, vt_ref, o_ref, *, chunk_mask):
    i = pl.program_id(2)
    tq = q_ref.shape[0]
    tk = vt_ref.shape[2]

    def scores(j, masked, lo=0):
        start = pl.multiple_of(j * tk, tk)
        ss = [_dot_nt(k_ref[pl.ds(start, tk), hd * LANES:(hd + 1) * LANES],
                      q_ref[lo:, hd * LANES:(hd + 1) * LANES])
              for hd in range(ATT_HEADS)]
        if masked:
            keyg = lax.broadcasted_iota(I32, (tk, tq - lo), 0) + j * tk
            qlo = lax.broadcasted_iota(I32, (tk, tq - lo), 1) + (i * tq + lo)
            allowed = (keyg // CHUNK) <= (qlo // CHUNK) if chunk_mask else keyg <= qlo
            ss = [jnp.where(allowed, s, NEG) for s in ss]
        return ss

    ones = jnp.ones((2 * SUBLANES, tk), BF16)

    def update(j, ss, state, lo=0):
        vt = vt_ref[j]
        new = []
        for hd, s in enumerate(ss):
            m_all, acc_all = state[hd]
            m, acc = m_all[:, lo:], acc_all[:, lo:]
            m_new = jnp.maximum(m, jnp.max(s, axis=0, keepdims=True))
            alpha = jnp.exp2(m - m_new)
            p = jnp.exp2((s - m_new[0:1]).astype(BF16))
            va = jnp.concatenate([vt[hd * HEAD_DIM:(hd + 1) * HEAD_DIM, :], ones], axis=0)
            acc = alpha[0:1] * acc + _dot(va, p)
            if lo:
                m_new = jnp.concatenate([m_all[:, :lo], m_new], axis=1)
                acc = jnp.concatenate([acc_all[:, :lo], acc], axis=1)
            new.append((m_new, acc))
        return tuple(new)

    n_full = (i * tq) // tk
    group = max(1, tq // tk)

    def run_group(first, state, n_plain, n_masked=0):
        los = [0] * n_plain + [u * tk for u in range(n_masked)]
        msk = [False] * n_plain + [True] * n_masked
        nblk = n_plain + n_masked
        ss = scores(first, msk[0], los[0])
        for u in range(nblk):
            nxt = scores(first + u + 1, msk[u + 1], los[u + 1]) if u + 1 < nblk else None
            state = update(first + u, ss, state, los[u])
            ss = nxt
        return state

    long = ATT_GROUP * group
    n_long = n_full // long
    init1 = (jnp.full((SUBLANES, tq), NEG, F32), jnp.zeros((HEAD_DIM + 2 * SUBLANES, tq), F32))
    state = lax.fori_loop(0, n_long, lambda jj, c: run_group(jj * long, c, long),
                          (init1,) * ATT_HEADS)
    done = n_long * long
    left = (n_full - done) // group
    tails = [functools.partial(run_group, done, n_plain=r * group, n_masked=group)
             for r in range(ATT_GROUP)]

    def pick(lo, hi, st):
        if hi - lo == 1:
            return tails[lo](st)
        mid = (lo + hi) // 2
        return lax.cond(left < mid, lambda s: pick(lo, mid, s), lambda s: pick(mid, hi, s), st)

    state = pick(0, ATT_GROUP, state)
    out_t = jnp.concatenate([acc[:HEAD_DIM] / acc[HEAD_DIM:HEAD_DIM + 1] for _, acc in state],
                            axis=0)
    o_ref[...] = out_t.T.astype(o_ref.dtype)


def _attention(q, k, vt, *, batch, seq_len, chunk_mask):
    n = q.shape[0]
    t = ATT_TILE
    tk = vt.shape[2]
    nq = seq_len // t
    nkb = seq_len // tk
    hp = ATT_HEADS
    pairs = HEADS // hp
    return pl.pallas_call(
        functools.partial(_attn_kernel, chunk_mask=chunk_mask),
        out_shape=jax.ShapeDtypeStruct((n, HEADS * HEAD_DIM), BF16),
        grid=(batch, pairs, nq),
        in_specs=[pl.BlockSpec((t, hp * LANES), lambda b, p, i: (b * nq + i, p)),
                  pl.BlockSpec((seq_len, hp * LANES), lambda b, p, i: (b, p)),
                  pl.BlockSpec((nkb, hp * HEAD_DIM, tk), lambda b, p, i: (b, p, 0))],
        out_specs=pl.BlockSpec((t, hp * HEAD_DIM), lambda b, p, i: (b * nq + i, p)),
        compiler_params=pltpu.CompilerParams(
            dimension_semantics=("arbitrary", "arbitrary", "arbitrary"),
            vmem_limit_bytes=VMEM_LIMIT),
        name="attn_mla" if chunk_mask else "attn_fox",
    )(q, k, vt)


def _outproj_kernel(yf_ref, ym_ref, gate_ref, x_ref, bm_ref, wfo_ref, wmo_ref, wo_ref, gffn_ref,
                    wrh_ref, wrl_ref, br_ref, lstrict_ref, ustrict_ref,
                    x1_ref, route_ref, dest_ref, meta_ref, xs_hbm,
                    fill_ref, cur_ref, nfree_ref, tbl_ref,
                    hbuf, dbuf, dsm, zbuf, mbuf, msm, ssem, isem, zsem, msem):
    i = pl.program_id(0)
    nt = pl.num_programs(0)
    d = x_ref.shape[1]
    tm = x_ref.shape[0]
    sub = d // LANES
    rows = EXPERT_ROWS
    n_blocks = xs_hbm.shape[0] // (rows * sub)
    slot = i % 2

    def idx_copy(s):
        return pltpu.make_async_copy(dbuf.at[s], dsm.at[s], isem.at[s])

    def start_scatters(s):
        for static_s in range(2):
            @pl.when(s == static_s)
            def _(static_s=static_s):
                def body(t, _):
                    src = hbuf.at[static_s, pl.ds(pl.multiple_of(t * sub, sub), sub), :]
                    for k in range(TOP_K):
                        dst = dsm[static_s, k, t]
                        pltpu.make_async_copy(
                            src, xs_hbm.at[pl.ds(pl.multiple_of(dst, sub), sub), :],
                            ssem.at[static_s]).start(priority=k % 2)
                    return 0

                lax.fori_loop(0, tm, body, 0, unroll=4)

    def wait_scatters(s):
        for _ in range(TOP_K):
            pltpu.make_async_copy(hbuf.at[s], xs_hbm.at[pl.ds(0, tm * sub), :], ssem.at[s]).wait()

    @pl.when(i == 0)
    def _():
        fill_ref[...] = jnp.full_like(fill_ref, float(rows))
        cur_ref[...] = jnp.zeros_like(cur_ref)
        nfree_ref[...] = jnp.zeros_like(nfree_ref)
        tbl_ref[...] = jnp.full_like(tbl_ref, float(N_EXPERTS))
        zbuf[...] = jnp.zeros_like(zbuf)

    def tile_step(scatter_prev):
        if scatter_prev:
            inline_scatters(1 - slot)
        a = _dot(yf_ref[...], wfo_ref[...])
        b = _dot(ym_ref[...], wmo_ref[...])
        g = 1.0 / (1.0 + jnp.exp(-(gate_ref[...].astype(F32) + bm_ref[...])))
        merged = (g[:, :d] * a + g[:, d:] * b).astype(BF16)
        x1 = x_ref[...] + _dot(merged, wo_ref[...])
        x1_ref[...] = x1
        h2 = _rms(x1, gffn_ref[...])

        hi = h2.astype(BF16)
        lo = (h2 - hi.astype(F32)).astype(BF16)
        wrh = wrh_ref[...]
        logits = _dot(hi, wrh) + _dot(lo, wrh) + _dot(hi, wrl_ref[...]) + br_ref[...]

        lane = lax.broadcasted_iota(I32, (tm, LANES), 1)
        vals = logits
        sels, tops = [], []
        for _ in range(TOP_K):
            mx = jnp.max(vals, axis=-1, keepdims=True)
            idx = jnp.min(jnp.where(vals == mx, lane, LANES), axis=-1, keepdims=True)
            sel = lane == idx
            vals = jnp.where(sel, NEG, vals)
            sels.append(sel)
            tops.append(mx)
        es = [jnp.exp(tv - tops[0]) for tv in tops]
        den = es[0] + es[1] + es[2] + es[3]

        onehot = jnp.zeros((tm, LANES), F32)
        for sel in sels:
            onehot = onehot + sel.astype(F32)
        before = _dot(lstrict_ref[...], onehot.astype(BF16))
        cnt = jnp.sum(onehot, axis=0, keepdims=True)

        fill = fill_ref[...]
        cur = cur_ref[...]
        nfree = nfree_ref[...]
        need = ((fill + cnt) > float(rows)).astype(F32)
        need8 = jnp.broadcast_to(need, (SUBLANES, LANES)).astype(BF16)
        newid = nfree + _dot(need8, ustrict_ref[...])[0:1, :]
        pos = fill + before
        dest = jnp.where(pos < float(rows), cur * rows + pos, newid * rows + pos - float(rows))
        fill_ref[...] = fill + cnt - need * float(rows)
        cur_ref[...] = jnp.where(need > 0, newid, cur)
        nfree_ref[...] = nfree + jnp.sum(need, axis=-1, keepdims=True)
        blk_id = (lax.broadcasted_iota(I32, (SUBLANES, LANES), 0) * LANES
                  + lax.broadcasted_iota(I32, (SUBLANES, LANES), 1)).astype(F32)
        tbl = tbl_ref[...]
        for e in range(N_EXPERTS):
            hit = (blk_id == newid[:, e:e + 1]) & (need[:, e:e + 1] > 0)
            tbl = jnp.where(hit, float(e), tbl)
        tbl_ref[...] = tbl

        route = jnp.zeros((tm, LANES), F32)
        for k in range(TOP_K):
            dest_k = jnp.sum(jnp.where(sels[k], dest, 0.0), axis=-1, keepdims=True)
            route = jnp.where(lane == ROUTE_DEST + k, dest_k, route)
            route = jnp.where(lane == ROUTE_W + k, es[k] / den, route)
        route_ref[...] = route
        dest_t = (route.T[0:SUBLANES, :] * float(sub)).astype(I32)
        dest_ref[0] = dest_t

        for s in range(sub):
            hbuf[slot, pl.ds(s, tm, stride=sub), :] = h2[:, s * LANES:(s + 1) * LANES]
        dbuf[slot] = dest_t

    def flush_last():
        idx_copy(slot).wait()
        start_scatters(slot)

        @pl.when(i >= 1)
        def _():
            wait_scatters(1 - slot)

        wait_scatters(slot)

        meta = jnp.concatenate([tbl_ref[...], jnp.broadcast_to(nfree_ref[...], (SUBLANES, LANES))],
                               axis=0).astype(I32)
        meta_ref[...] = meta
        state = jnp.concatenate([fill_ref[...], cur_ref[...], nfree_ref[...],
                                 jnp.zeros((SUBLANES - 3, LANES), F32)], axis=0).astype(I32)
        mbuf[...] = state
        mcopy = pltpu.make_async_copy(mbuf, msm, msem.at[0])
        mcopy.start()
        mcopy.wait()

        def zero_copy(first_row, n_rows):
            return pltpu.make_async_copy(
                zbuf.at[pl.ds(0, n_rows * sub), :],
                xs_hbm.at[pl.ds(pl.multiple_of(first_row * sub, sub), n_rows * sub), :], zsem.at[0])

        chunks = [rows >> (s + 1) for s in range(rows.bit_length() - 1)]
        plans = []
        for e in range(N_EXPERTS):
            rem = rows - msm[0, e]
            at = msm[1, e] * rows + msm[0, e]
            for c in chunks:
                take = (rem & c) != 0
                plans.append((take, zero_copy(at, c)))
                at = at + jnp.where(take, c, 0)
        for j in range(N_EXPERTS):
            blk = msm[2, 0] + j
            safe = jnp.minimum(blk, n_blocks - 1)
            plans.append((blk < n_blocks, zero_copy(safe * rows, rows)))
        for take, cp in plans:
            pl.when(take)(cp.start)
        for take, cp in plans:
            pl.when(take)(cp.wait)

    @pl.when(i > 0)
    def _():
        idx_copy(1 - slot).wait()

    @pl.when(i >= 2)
    def _():
        wait_scatters(slot)

    def inline_scatters(s):
        for t in range(tm):
            src = hbuf.at[s, pl.ds(t * sub, sub), :]
            for k in range(TOP_K):
                dst = dsm[s, k, t]
                pltpu.make_async_copy(src, xs_hbm.at[pl.ds(pl.multiple_of(dst, sub), sub), :],
                                      ssem.at[s]).start(priority=k % 2)

    @pl.when(i == 0)
    def _():
        tile_step(False)

    @pl.when(i > 0)
    def _():
        tile_step(True)

    idx_copy(slot).start()

    @pl.when(i == nt - 1)
    def _():
        flush_last()


def _outproj(yf, ym, gates, x2, bm, wfo, wmo, wo, gffn, wrh, wrl, br, lstrict, ustrict, *, n_blocks):
    n, d = x2.shape
    tm = MOE_TILE
    rows = EXPERT_ROWS
    sub = d // LANES
    const = lambda i: (0, 0)
    row = lambda i: (i, 0)
    full = lambda a: pl.BlockSpec(a.shape, const)
    consts = (bm, wfo, wmo, wo, gffn, wrh, wrl, br, lstrict, ustrict)
    return pl.pallas_call(
        _outproj_kernel,
        out_shape=[jax.ShapeDtypeStruct((n, d), F32),
                   jax.ShapeDtypeStruct((n, LANES), F32),
                   jax.ShapeDtypeStruct((n // tm, SUBLANES, tm), I32),
                   jax.ShapeDtypeStruct((2 * SUBLANES, LANES), I32),
                   jax.ShapeDtypeStruct((n_blocks * rows * sub, LANES), F32)],
        grid=(n // tm,),
        in_specs=[pl.BlockSpec((tm, yf.shape[1]), row), pl.BlockSpec((tm, ym.shape[1]), row),
                  pl.BlockSpec((tm, gates.shape[1]), row), pl.BlockSpec((tm, d), row)]
                 + [full(a) for a in consts],
        out_specs=[pl.BlockSpec((tm, d), row), pl.BlockSpec((tm, LANES), row),
                   pl.BlockSpec((1, SUBLANES, tm), lambda i: (i, 0, 0)),
                   pl.BlockSpec((2 * SUBLANES, LANES), const),
                   pl.BlockSpec(memory_space=pl.ANY)],
        scratch_shapes=[pltpu.VMEM((1, LANES), F32), pltpu.VMEM((1, LANES), F32),
                        pltpu.VMEM((1, LANES), F32), pltpu.VMEM((SUBLANES, LANES), F32),
                        pltpu.VMEM((2, tm * sub, LANES), F32),
                        pltpu.VMEM((2, SUBLANES, tm), I32),
                        pltpu.SMEM((2, SUBLANES, tm), I32),
                        pltpu.VMEM((rows * sub, LANES), F32),
                        pltpu.VMEM((SUBLANES, LANES), I32),
                        pltpu.SMEM((SUBLANES, LANES), I32),
                        pltpu.SemaphoreType.DMA((2,)), pltpu.SemaphoreType.DMA((2,)),
                        pltpu.SemaphoreType.DMA((1,)), pltpu.SemaphoreType.DMA((1,))],
        compiler_params=pltpu.CompilerParams(dimension_semantics=("arbitrary",),
                                             vmem_limit_bytes=VMEM_LIMIT),
        name="outproj_router",
    )(yf, ym, gates, x2, *consts)


def _expert_kernel(order_ref, be_ref, nused_ref,
                   xs_ref, wgu_ref, bgu_ref, wd_ref, bd_ref, ys_ref, wgu_b, wd_b):
    b = pl.program_id(0)
    rows = EXPERT_ROWS
    sub = xs_ref.shape[0] // rows

    @pl.when(b < nused_ref[0])
    def _():
        changed = jnp.logical_or(b == 0, be_ref[b] != be_ref[jnp.maximum(b - 1, 0)])

        @pl.when(changed)
        def _():
            wgu_b[...] = wgu_ref[0].astype(BF16)
            wd_b[...] = wd_ref[0].astype(BF16)

        x = jnp.concatenate(
            [xs_ref[pl.ds(s, rows, stride=sub), :].astype(BF16) for s in range(sub)], axis=1)
        gu = _dot(x, wgu_b[...]) + bgu_ref[0]
        gate = jnp.minimum(gu[:, :EXPERT_FF], SWIGLU_LIMIT)
        up = jnp.clip(gu[:, EXPERT_FF:], -SWIGLU_LIMIT, SWIGLU_LIMIT)
        glu = gate * (1.0 / (1.0 + jnp.exp(-SWIGLU_ALPHA * gate)))
        y = _dot(((up + 1.0) * glu).astype(BF16), wd_b[...]) + bd_ref[0]
        for s in range(sub):
            ys_ref[pl.ds(s, rows, stride=sub), :] = y[:, s * LANES:(s + 1) * LANES]

    @pl.when(b >= nused_ref[0])
    def _():
        ys_ref[...] = jnp.zeros_like(ys_ref)


def _experts(order, block_e, nused, xs, w_gu, b_gu, w_d, b_d):
    e, d, ff2 = w_gu.shape
    rows = EXPERT_ROWS
    sub = d // LANES
    n_blocks = xs.shape[0] // (rows * sub)
    wmap = lambda b, od, be, nu: (be[b], 0, 0)
    xmap = lambda b, od, be, nu: (od[b], 0)
    grid_spec = pltpu.PrefetchScalarGridSpec(
        num_scalar_prefetch=3,
        grid=(n_blocks,),
        in_specs=[pl.BlockSpec((rows * sub, LANES), xmap),
                  pl.BlockSpec((1, d, ff2), wmap), pl.BlockSpec((1, 1, ff2), wmap),
                  pl.BlockSpec((1, ff2 // 2, d), wmap), pl.BlockSpec((1, 1, d), wmap)],
        out_specs=pl.BlockSpec((rows * sub, LANES), xmap),
        scratch_shapes=[pltpu.VMEM((d, ff2), BF16), pltpu.VMEM((ff2 // 2, d), BF16)],
    )
    return pl.pallas_call(
        _expert_kernel,
        out_shape=jax.ShapeDtypeStruct(xs.shape, F32),
        grid_spec=grid_spec,
        compiler_params=pltpu.CompilerParams(dimension_semantics=("arbitrary",),
                                             vmem_limit_bytes=VMEM_LIMIT),
        name="experts",
    )(order, block_e, nused, xs, w_gu, b_gu.reshape(e, 1, ff2), w_d, b_d.reshape(e, 1, d))


def _combine_kernel(x1_ref, route_ref, dest_hbm, ys_hbm, g_ref, o_ref, ybuf, dsm, gsem, isem):
    i = pl.program_id(0)
    nt = pl.num_programs(0)
    tm, d = x1_ref.shape
    sub = d // LANES
    slot = i % 2

    def idx_copy(tile):
        s = tile % 2
        return pltpu.make_async_copy(dest_hbm.at[tile], dsm.at[s], isem.at[s])

    def start_gathers(tile):
        for static_s in range(2):
            @pl.when(tile % 2 == static_s)
            def _(static_s=static_s):
                def body(t, _):
                    for k in range(TOP_K):
                        src = dsm[static_s, k, t]
                        pltpu.make_async_copy(
                            ys_hbm.at[pl.ds(pl.multiple_of(src, sub), sub), :],
                            ybuf.at[static_s, pl.ds(pl.multiple_of((k * tm + t) * sub, sub), sub), :],
                            gsem.at[static_s]).start(priority=k % 2)
                    return 0

                lax.fori_loop(0, tm, body, 0, unroll=4)

    def wait_gathers(s):
        pltpu.make_async_copy(ys_hbm.at[pl.ds(0, TOP_K * tm * sub), :], ybuf.at[s], gsem.at[s]).wait()

    @pl.when(i == 0)
    def _():
        idx_copy(0).start()
        idx_copy(0).wait()
        start_gathers(0)

        @pl.when(nt > 1)
        def _():
            idx_copy(1).start()

    @pl.when(i + 1 < nt)
    def _():
        idx_copy(i + 1).wait()
        start_gathers(i + 1)

    @pl.when(i + 2 < nt)
    def _():
        idx_copy(i + 2).start()

    wait_gathers(slot)
    route = route_ref[...]
    ws = [route[:, ROUTE_W + k:ROUTE_W + k + 1] for k in range(TOP_K)]
    chunks = []
    ssq = jnp.zeros((tm, 1), F32)
    for s in range(sub):
        acc = x1_ref[:, s * LANES:(s + 1) * LANES]
        for k in range(TOP_K):
            acc = acc + ws[k] * ybuf[slot, pl.ds(k * tm * sub + s, tm, stride=sub), :]
        chunks.append(acc)
        ssq = ssq + jnp.sum(acc * acc, axis=-1, keepdims=True)
    inv = lax.rsqrt(ssq / d + NORM_EPS)
    for s in range(sub):
        sl = slice(s * LANES, (s + 1) * LANES)
        o_ref[:, sl] = chunks[s] * inv * g_ref[:, sl]


def _combine(x1, route, dest_t, ys, g_final):
    n, d = x1.shape
    tm = MOE_TILE
    sub = d // LANES
    row = lambda i: (i, 0)
    return pl.pallas_call(
        _combine_kernel,
        out_shape=jax.ShapeDtypeStruct((n, d), F32),
        grid=(n // tm,),
        in_specs=[pl.BlockSpec((tm, d), row), pl.BlockSpec((tm, LANES), row),
                  pl.BlockSpec(memory_space=pl.ANY), pl.BlockSpec(memory_space=pl.ANY),
                  pl.BlockSpec((1, d), lambda i: (0, 0))],
        out_specs=pl.BlockSpec((tm, d), row),
        scratch_shapes=[pltpu.VMEM((2, TOP_K * tm * sub, LANES), F32),
                        pltpu.SMEM((2, SUBLANES, tm), I32),
                        pltpu.SemaphoreType.DMA((2,)), pltpu.SemaphoreType.DMA((2,))],
        compiler_params=pltpu.CompilerParams(dimension_semantics=("arbitrary",),
                                             vmem_limit_bytes=VMEM_LIMIT),
        name="combine_norm",
    )(x1, route, dest_t, ys, g_final)


def _pad_heads(w, per_head, width=LANES):
    k = w.shape[0]
    w = w.reshape(k, HEADS, per_head)
    w = jnp.pad(w, ((0, 0), (0, 0), (0, width - per_head)))
    return w.reshape(k, HEADS * width)


def _aug_constants():
    place = np.zeros((LANES, 2 * LANES), np.float32)
    ones = np.zeros((1, 2 * LANES), np.float32)
    for hd in range(HEADS):
        for piece in range(3):
            src = FG_LO + piece * HEADS + hd
            place[src, 16 * hd + piece] = 1.0
            place[src, LANES + 16 * hd + 3 + piece] = -1.0
            ones[0, 16 * hd + 3 + piece] = 1.0
            ones[0, LANES + 16 * hd + piece] = 1.0
    return jnp.asarray(place, BF16), jnp.asarray(ones, F32)


def _layer(x2, pos2, batch, seq_len, g_attn_norm, w_in, b_fgate, g_q_a, w_q_b, g_kv_a, w_kv_b,
           w_fox_out, w_mla_out, b_merge, w_o, g_ffn_norm, w_router, b_router, w_gu, b_gu,
           w_down, b_down, g_out):
    n, d = x2.shape
    fw = HEADS * HEAD_DIM
    o = 0
    w_qf = w_in[:, o:o + fw]; o += fw
    w_kf = w_in[:, o:o + fw]; o += fw
    w_vf = w_in[:, o:o + fw]; o += fw
    w_f = w_in[:, o:o + HEADS]; o += HEADS
    w_ql = w_in[:, o:o + Q_RANK]; o += Q_RANK
    w_ckv = w_in[:, o:o + KV_RANK]; o += KV_RANK
    w_kpe = w_in[:, o:o + ROPE_DIM]; o += ROPE_DIM
    w_gate = w_in[:, o:]

    wq = (w_qf * (HEAD_DIM ** -0.5)).astype(BF16)
    wk = w_kf.astype(BF16)
    wmisc = jnp.concatenate([jnp.zeros((d, PE_LO), F32), w_kpe, w_f,
                             jnp.zeros((d, LANES - FG_LO - HEADS), F32)], axis=1)
    wlat = jnp.concatenate([w_ql, w_ckv, wmisc], axis=1).astype(BF16)
    bf128 = jnp.zeros((1, LANES), F32).at[0, FG_LO:FG_LO + HEADS].set(b_fgate)
    wqb = _pad_heads(w_q_b, HEAD_DIM + ROPE_DIM).astype(BF16)
    wkv = w_kv_b.reshape(KV_RANK, HEADS, 2 * HEAD_DIM)
    wkbk = wkv[:, :, :HEAD_DIM].reshape(KV_RANK, fw).astype(BF16)
    wkbv = wkv[:, :, HEAD_DIM:].reshape(KV_RANK, fw).T.astype(BF16)
    ltri = jnp.asarray(np.tril(np.ones((IN_TILE, IN_TILE), np.float32)), BF16)
    lstrict = jnp.asarray(np.tril(np.ones((MOE_TILE, MOE_TILE), np.float32), -1), BF16)
    ustrict = jnp.asarray(np.triu(np.ones((LANES, LANES), np.float32), 1), BF16)
    place, ones = _aug_constants()
    half = ROPE_DIM // 2
    inv_freq = ROPE_THETA ** (-jnp.arange(half, dtype=F32) / half)
    freq = jnp.zeros((1, LANES), F32).at[0, PE_LO:PE_MID].set(inv_freq).at[0, PE_MID:PE_HI].set(inv_freq)

    qf, kf, vf, qm, km, vm, gates = _inproj(
        x2, pos2, g_attn_norm.reshape(1, d), wq, wk, w_vf.T.astype(BF16), wlat, w_gate.astype(BF16),
        bf128, g_q_a.reshape(1, -1), wqb, g_kv_a.reshape(1, -1), wkbk, wkbv, ltri, place, ones,
        freq, seq_len=seq_len)

    y_fox = _attention(qf, kf, vf, batch=batch, seq_len=seq_len, chunk_mask=False)
    y_mla = _attention(qm, km, vm, batch=batch, seq_len=seq_len, chunk_mask=True)

    wr = jnp.pad(w_router, ((0, 0), (0, LANES - N_EXPERTS)))
    wrh = wr.astype(BF16)
    wrl = (wr - wrh.astype(F32)).astype(BF16)
    br = jnp.full((1, LANES), NEG, F32).at[0, :N_EXPERTS].set(b_router)
    n_blocks = n * TOP_K // EXPERT_ROWS + N_EXPERTS
    x1, route, dest_t, meta, xs = _outproj(
        y_fox, y_mla, gates, x2, b_merge.reshape(1, -1), w_fox_out.astype(BF16),
        w_mla_out.astype(BF16), w_o.astype(BF16), g_ffn_norm.reshape(1, d), wrh, wrl, br, lstrict,
        ustrict, n_blocks=n_blocks)

    block_e = meta[:SUBLANES].reshape(-1)[:n_blocks]
    order = jnp.argsort(block_e, stable=True).astype(I32)
    nused = meta[SUBLANES, 0:1]
    be_sorted = jnp.minimum(block_e[order], N_EXPERTS - 1).astype(I32)
    be_sorted = jnp.where(jnp.arange(n_blocks) < nused[0], be_sorted,
                          be_sorted[jnp.maximum(nused[0] - 1, 0)])
    ys = _experts(order, be_sorted, nused, xs, w_gu, b_gu, w_down, b_down)
    return _combine(x1, route, dest_t, ys, g_out.reshape(1, d))


def kernel(x, positions, g_attn_norm, w_in, b_fgate, g_q_a, w_q_b, g_kv_a, w_kv_b, w_fox_out, w_mla_out, b_merge, w_o, g_ffn_norm, w_router, b_router, w_gu, b_gu, w_down, b_down, g_final):
    batch, seq_len, d = x.shape
    depth = w_in.shape[0]
    assert depth == 1, "the fused combine + final-norm kernel assumes a single layer"
    assert seq_len % ATT_TILE == 0 and d % LANES == 0
    assert (batch * seq_len * TOP_K) % EXPERT_ROWS == 0
    assert batch * seq_len * TOP_K // EXPERT_ROWS + N_EXPERTS <= SUBLANES * LANES
    x2 = x.reshape(batch * seq_len, d)
    pos2 = positions.reshape(batch * seq_len, 1).astype(I32)
    out = _layer(x2, pos2, batch, seq_len, g_attn_norm[0], w_in[0], b_fgate[0], g_q_a[0], w_q_b[0],
                 g_kv_a[0], w_kv_b[0], w_fox_out[0], w_mla_out[0], b_merge[0], w_o[0],
                 g_ffn_norm[0], w_router[0], b_router[0], w_gu[0], b_gu[0], w_down[0], b_down[0],
                 g_final)
    return out.reshape(batch, seq_len, d)
```

```python
import functools

import jax
import jax.numpy as jnp
import numpy as np
from jax import lax
from jax.experimental import pallas as pl
from jax.experimental.pallas import tpu as pltpu

F32 = jnp.float32
BF16 = jnp.bfloat16
I32 = jnp.int32

LANES = 128
SUBLANES = 8
VMEM_LIMIT = 56 * 1024 * 1024

NORM_EPS = 1e-6
HEADS = 8
HEAD_DIM = 64
ROPE_DIM = 32
Q_RANK = 256
KV_RANK = 128
N_EXPERTS = 32
TOP_K = 4
EXPERT_FF = 1024
SWIGLU_LIMIT = 7.0
SWIGLU_ALPHA = 1.702
ROPE_THETA = 10000.0
CHUNK = 64

NEG = -1e30
LOG2E = 1.4426950408889634

ROW_TILE = 256
IN_TILE = 512
MOE_TILE = 512
ATT_TILE = 512
ATT_HEADS = 2
ATT_GROUP = 4
EXPERT_ROWS = MOE_TILE

PE_LO, PE_MID, PE_HI = 64, 80, 96
FG_LO = 96
AUG_LO = 64
ROUTE_DEST = 0
ROUTE_W = 8


def _dot(a, b):
    return jnp.dot(a, b, preferred_element_type=F32)


def _dot_nt(a, b):
    return lax.dot_general(a, b, (((1,), (1,)), ((), ())), preferred_element_type=F32)


def _split3(a):
    hi = a.astype(BF16)
    r1 = a - hi.astype(F32)
    mid = r1.astype(BF16)
    lo = (r1 - mid.astype(F32)).astype(BF16)
    return hi, mid, lo


def _rms(x, g):
    return x * lax.rsqrt(jnp.mean(x * x, axis=-1, keepdims=True) + NORM_EPS) * g


def _inproj_kernel(x_ref, pos_ref, g_ref, wq_ref, wk_ref, wv_ref, wlat_ref, wg_ref, bf_ref,
                   gq_ref, wqb_ref, gkv_ref, wkbk_ref, wkbv_ref, ltri_ref, place_ref, ones_ref,
                   freq_ref,
                   qf_ref, kf_ref, vf_ref, qm_ref, km_ref, vm_ref, gate_ref,
                   carry_ref, *, tiles_per_seq, mla_scale):
    i = pl.program_id(0)

    @pl.when(i % tiles_per_seq == 0)
    def _():
        carry_ref[...] = jnp.zeros_like(carry_ref)

    h = _rms(x_ref[...], g_ref[...]).astype(BF16)
    tm = h.shape[0]
    lane = lax.broadcasted_iota(I32, (tm, LANES), 1)

    lat = _dot(h, wlat_ref[...])
    q_lat = lat[:, :Q_RANK]
    c_kv = lat[:, Q_RANK:Q_RANK + KV_RANK]
    misc = lat[:, Q_RANK + KV_RANK:]

    qn = _rms(q_lat, gq_ref[...]).astype(BF16)
    kvn = _rms(c_kv, gkv_ref[...]).astype(BF16)
    qfull = _dot(qn, wqb_ref[...])
    knope = _dot(kvn, wkbk_ref[...])
    vmt = _dot_nt(wkbv_ref[...], kvn).astype(BF16)
    for c in range(tm // ROW_TILE):
        vm_ref[c] = vmt[:, c * ROW_TILE:(c + 1) * ROW_TILE]

    z = misc + bf_ref[...]
    logf = jnp.minimum(z, 0.0) - jnp.log1p(jnp.exp(-jnp.abs(z)))
    fmask = (lane >= FG_LO) & (lane < FG_LO + HEADS)
    logf = jnp.where(fmask, logf, 0.0)
    l_hi, l_mid, l_lo = _split3(logf)
    ltri = ltri_ref[...]
    c = _dot(ltri, l_hi) + _dot(ltri, l_mid) + _dot(ltri, l_lo) + carry_ref[...]
    carry_ref[...] = c[tm - 1:tm, :]
    c_hi, c_mid, c_lo = _split3(c * LOG2E)
    c3 = (c_hi.astype(F32) + pltpu.roll(c_mid.astype(F32), HEADS, 1)
          + pltpu.roll(c_lo.astype(F32), 2 * HEADS, 1)).astype(BF16)
    augc = _dot(c3, place_ref[...]) + ones_ref[...]
    low = lane < HEAD_DIM
    amask = (lane >= AUG_LO) & (lane < AUG_LO + 6)

    def aug_block(half, hd):
        src = augc[:, half * LANES:(half + 1) * LANES]
        return jnp.where(amask, pltpu.roll(src, (AUG_LO - 16 * hd) % LANES, 1), 0.0)

    def head_block(compact, hd):
        pair = compact[:, (hd // 2) * LANES:(hd // 2 + 1) * LANES]
        if hd % 2:
            pair = pltpu.roll(pair, HEAD_DIM, 1)
        return jnp.where(low, pair, 0.0)

    qc = _dot(h, wq_ref[...]) * LOG2E
    kc = _dot(h, wk_ref[...])
    vft = _dot_nt(wv_ref[...], h).astype(BF16)
    for c in range(tm // ROW_TILE):
        vf_ref[c] = vft[:, c * ROW_TILE:(c + 1) * ROW_TILE]
    gate_ref[...] = _dot(h, wg_ref[...]).astype(BF16)

    for hd in range(HEADS):
        sl = slice(hd * LANES, (hd + 1) * LANES)
        qf_ref[:, sl] = (head_block(qc, hd) + aug_block(0, hd)).astype(BF16)
        kf_ref[:, sl] = (head_block(kc, hd) + aug_block(1, hd)).astype(BF16)

    ang = pos_ref[...].astype(F32) * freq_ref[...]
    cosv = jnp.cos(ang)
    sinv = jnp.sin(ang)
    s1 = jnp.where((lane >= PE_LO) & (lane < PE_MID), -sinv, 0.0)
    s2 = jnp.where((lane >= PE_MID) & (lane < PE_HI), sinv, 0.0)

    def rope(v):
        return v * cosv + pltpu.roll(v, LANES - 16, 1) * s1 + pltpu.roll(v, 16, 1) * s2

    kpe = jnp.where((lane >= PE_LO) & (lane < PE_HI), rope(misc), 0.0)
    for hd in range(HEADS):
        sl = slice(hd * LANES, (hd + 1) * LANES)
        qm_ref[:, sl] = (rope(qfull[:, sl]) * mla_scale).astype(BF16)
        km_ref[:, sl] = (head_block(knope, hd) + kpe).astype(BF16)


def _inproj(x2, pos2, g_attn, wq, wk, wv, wlat, wg, bf128, gq, wqb, gkv, wkbk, wkbv, ltri,
            place, ones, freq, *, seq_len):
    n, d = x2.shape
    tm = IN_TILE
    hw = HEADS * LANES
    vw = HEADS * HEAD_DIM
    const = lambda i: (0, 0)
    row = lambda i: (i, 0)
    full = lambda a: pl.BlockSpec(a.shape, const)
    rows_out = lambda w: (jax.ShapeDtypeStruct((n, w), BF16), pl.BlockSpec((tm, w), row))
    vt_out = (jax.ShapeDtypeStruct((n // ROW_TILE, vw, ROW_TILE), BF16),
              pl.BlockSpec((tm // ROW_TILE, vw, ROW_TILE), lambda i: (i, 0, 0)))
    outs = [rows_out(hw), rows_out(hw), vt_out, rows_out(hw), rows_out(hw), vt_out,
            rows_out(wg.shape[1])]
    consts = (g_attn, wq, wk, wv, wlat, wg, bf128, gq, wqb, gkv, wkbk, wkbv, ltri, place, ones, freq)
    return pl.pallas_call(
        functools.partial(_inproj_kernel, tiles_per_seq=seq_len // tm,
                          mla_scale=float((HEAD_DIM + ROPE_DIM) ** -0.5) * LOG2E),
        out_shape=[o[0] for o in outs],
        grid=(n // tm,),
        in_specs=[pl.BlockSpec((tm, d), row), pl.BlockSpec((tm, 1), row)] + [full(a) for a in consts],
        out_specs=[o[1] for o in outs],
        scratch_shapes=[pltpu.VMEM((1, LANES), F32)],
        compiler_params=pltpu.CompilerParams(dimension_semantics=("arbitrary",),
                                             vmem_limit_bytes=VMEM_LIMIT),
        name="inproj",
    )(x2, pos2, *consts)


def _attn_kernel(q_ref, k_ref, vt_ref, o_ref, *, chunk_mask):
    i = pl.program_id(2)
    tq = q_ref.shape[0]
    tk = vt_ref.shape[2]

    def scores(j, masked, lo=0):
        start = pl.multiple_of(j * tk, tk)
        ss = [_dot_nt(k_ref[pl.ds(start, tk), hd * LANES:(hd + 1) * LANES],
                      q_ref[lo:, hd * LANES:(hd + 1) * LANES])
              for hd in range(ATT_HEADS)]
        if masked:
            keyg = lax.broadcasted_iota(I32, (tk, tq - lo), 0) + j * tk
            qlo = lax.broadcasted_iota(I32, (tk, tq - lo), 1) + (i * tq + lo)
            allowed = (keyg // CHUNK) <= (qlo // CHUNK) if chunk_mask else keyg <= qlo
            ss = [jnp.where(allowed, s, NEG) for s in ss]
        return ss

    ones = jnp.ones((2 * SUBLANES, tk), BF16)

    def update(j, ss, state, lo=0):
        vt = vt_ref[j]
        new = []
        for hd, s in enumerate(ss):
            m_all, acc_all = state[hd]
            m, acc = m_all[:, lo:], acc_all[:, lo:]
            m_new = jnp.maximum(m, jnp.max(s, axis=0, keepdims=True))
            alpha = jnp.exp2(m - m_new)
            p = jnp.exp2((s - m_new[0:1]).astype(BF16))
            va = jnp.concatenate([vt[hd * HEAD_DIM:(hd + 1) * HEAD_DIM, :], ones], axis=0)
            acc = alpha[0:1] * acc + _dot(va, p)
            if lo:
                m_new = jnp.concatenate([m_all[:, :lo], m_new], axis=1)
                acc = jnp.concatenate([acc_all[:, :lo], acc], axis=1)
            new.append((m_new, acc))
        return tuple(new)

    n_full = (i * tq) // tk
    group = max(1, tq // tk)

    def run_group(first, state, n_plain, n_masked=0):
        los = [0] * n_plain + [u * tk for u in range(n_masked)]
        msk = [False] * n_plain + [True] * n_masked
        nblk = n_plain + n_masked
        ss = scores(first, msk[0], los[0])
        for u in range(nblk):
            nxt = scores(first + u + 1, msk[u + 1], los[u + 1]) if u + 1 < nblk else None
            state = update(first + u, ss, state, los[u])
            ss = nxt
        return state

    long = ATT_GROUP * group
    n_long = n_full // long
    init1 = (jnp.full((SUBLANES, tq), NEG, F32), jnp.zeros((HEAD_DIM + 2 * SUBLANES, tq), F32))
    state = lax.fori_loop(0, n_long, lambda jj, c: run_group(jj * long, c, long),
                          (init1,) * ATT_HEADS)
    done = n_long * long
    left = (n_full - done) // group
    tails = [functools.partial(run_group, done, n_plain=r * group, n_masked=group)
             for r in range(ATT_GROUP)]

    def pick(lo, hi, st):
        if hi - lo == 1:
            return tails[lo](st)
        mid = (lo + hi) // 2
        return lax.cond(left < mid, lambda s: pick(lo, mid, s), lambda s: pick(mid, hi, s), st)

    state = pick(0, ATT_GROUP, state)
    out_t = jnp.concatenate([acc[:HEAD_DIM] / acc[HEAD_DIM:HEAD_DIM + 1] for _, acc in state],
                            axis=0)
    o_ref[...] = out_t.T.astype(o_ref.dtype)


def _attention(q, k, vt, *, batch, seq_len, chunk_mask):
    n = q.shape[0]
    t = ATT_TILE
    tk = vt.shape[2]
    nq = seq_len // t
    nkb = seq_len // tk
    hp = ATT_HEADS
    pairs = HEADS // hp
    return pl.pallas_call(
        functools.partial(_attn_kernel, chunk_mask=chunk_mask),
        out_shape=jax.ShapeDtypeStruct((n, HEADS * HEAD_DIM), BF16),
        grid=(batch, pairs, nq),
        in_specs=[pl.BlockSpec((t, hp * LANES), lambda b, p, i: (b * nq + i, p)),
                  pl.BlockSpec((seq_len, hp * LANES), lambda b, p, i: (b, p)),
                  pl.BlockSpec((nkb, hp * HEAD_DIM, tk), lambda b, p, i: (b, p, 0))],
        out_specs=pl.BlockSpec((t, hp * HEAD_DIM), lambda b, p, i: (b * nq + i, p)),
        compiler_params=pltpu.CompilerParams(
            dimension_semantics=("arbitrary", "arbitrary", "arbitrary"),
            vmem_limit_bytes=VMEM_LIMIT),
        name="attn_mla" if chunk_mask else "attn_fox",
    )(q, k, vt)


def _outproj_kernel(yf_ref, ym_ref, gate_ref, x_ref, bm_ref, wfo_ref, wmo_ref, wo_ref, gffn_ref,
                    wrh_ref, wrl_ref, br_ref, lstrict_ref, ustrict_ref,
                    x1_ref, route_ref, dest_ref, meta_ref, xs_hbm,
                    fill_ref, cur_ref, nfree_ref, tbl_ref,
                    hbuf, dbuf, dsm, zbuf, mbuf, msm, ssem, isem, zsem, msem):
    i = pl.program_id(0)
    nt = pl.num_programs(0)
    d = x_ref.shape[1]
    tm = x_ref.shape[0]
    sub = d // LANES
    rows = EXPERT_ROWS
    n_blocks = xs_hbm.shape[0] // (rows * sub)
    slot = i % 2

    def idx_copy(s):
        return pltpu.make_async_copy(dbuf.at[s], dsm.at[s], isem.at[s])

    def start_scatters(s):
        for static_s in range(2):
            @pl.when(s == static_s)
            def _(static_s=static_s):
                def body(t, _):
                    src = hbuf.at[static_s, pl.ds(pl.multiple_of(t * sub, sub), sub), :]
                    for k in range(TOP_K):
                        dst = dsm[static_s, k, t]
                        pltpu.make_async_copy(
                            src, xs_hbm.at[pl.ds(pl.multiple_of(dst, sub), sub), :],
                            ssem.at[static_s]).start(priority=k % 2)
                    return 0

                lax.fori_loop(0, tm, body, 0, unroll=4)

    def wait_scatters(s):
        for _ in range(TOP_K):
            pltpu.make_async_copy(hbuf.at[s], xs_hbm.at[pl.ds(0, tm * sub), :], ssem.at[s]).wait()

    @pl.when(i == 0)
    def _():
        fill_ref[...] = jnp.full_like(fill_ref, float(rows))
        cur_ref[...] = jnp.zeros_like(cur_ref)
        nfree_ref[...] = jnp.zeros_like(nfree_ref)
        tbl_ref[...] = jnp.full_like(tbl_ref, float(N_EXPERTS))
        zbuf[...] = jnp.zeros_like(zbuf)

    def tile_step(scatter_prev):
        if scatter_prev:
            inline_scatters(1 - slot)
        a = _dot(yf_ref[...], wfo_ref[...])
        b = _dot(ym_ref[...], wmo_ref[...])
        g = 1.0 / (1.0 + jnp.exp(-(gate_ref[...].astype(F32) + bm_ref[...])))
        merged = (g[:, :d] * a + g[:, d:] * b).astype(BF16)
        x1 = x_ref[...] + _dot(merged, wo_ref[...])
        x1_ref[...] = x1
        h2 = _rms(x1, gffn_ref[...])

        hi = h2.astype(BF16)
        lo = (h2 - hi.astype(F32)).astype(BF16)
        wrh = wrh_ref[...]
        logits = _dot(hi, wrh) + _dot(lo, wrh) + _dot(hi, wrl_ref[...]) + br_ref[...]

        lane = lax.broadcasted_iota(I32, (tm, LANES), 1)
        vals = logits
        sels, tops = [], []
        for _ in range(TOP_K):
            mx = jnp.max(vals, axis=-1, keepdims=True)
            idx = jnp.min(jnp.where(vals == mx, lane, LANES), axis=-1, keepdims=True)
            sel = lane == idx
            vals = jnp.where(sel, NEG, vals)
            sels.append(sel)
            tops.append(mx)
        es = [jnp.exp(tv - tops[0]) for tv in tops]
        den = es[0] + es[1] + es[2] + es[3]

        onehot = jnp.zeros((tm, LANES), F32)
        for sel in sels:
            onehot = onehot + sel.astype(F32)
        before = _dot(lstrict_ref[...], onehot.astype(BF16))
        cnt = jnp.sum(onehot, axis=0, keepdims=True)

        fill = fill_ref[...]
        cur = cur_ref[...]
        nfree = nfree_ref[...]
        need = ((fill + cnt) > float(rows)).astype(F32)
        need8 = jnp.broadcast_to(need, (SUBLANES, LANES)).astype(BF16)
        newid = nfree + _dot(need8, ustrict_ref[...])[0:1, :]
        pos = fill + before
        dest = jnp.where(pos < float(rows), cur * rows + pos, newid * rows + pos - float(rows))
        fill_ref[...] = fill + cnt - need * float(rows)
        cur_ref[...] = jnp.where(need > 0, newid, cur)
        nfree_ref[...] = nfree + jnp.sum(need, axis=-1, keepdims=True)
        blk_id = (lax.broadcasted_iota(I32, (SUBLANES, LANES), 0) * LANES
                  + lax.broadcasted_iota(I32, (SUBLANES, LANES), 1)).astype(F32)
        tbl = tbl_ref[...]
        for e in range(N_EXPERTS):
            hit = (blk_id == newid[:, e:e + 1]) & (need[:, e:e + 1] > 0)
            tbl = jnp.where(hit, float(e), tbl)
        tbl_ref[...] = tbl

        route = jnp.zeros((tm, LANES), F32)
        for k in range(TOP_K):
            dest_k = jnp.sum(jnp.where(sels[k], dest, 0.0), axis=-1, keepdims=True)
            route = jnp.where(lane == ROUTE_DEST + k, dest_k, route)
            route = jnp.where(lane == ROUTE_W + k, es[k] / den, route)
        route_ref[...] = route
        dest_t = (route.T[0:SUBLANES, :] * float(sub)).astype(I32)
        dest_ref[0] = dest_t

        for s in range(sub):
            hbuf[slot, pl.ds(s, tm, stride=sub), :] = h2[:, s * LANES:(s + 1) * LANES]
        dbuf[slot] = dest_t

    def flush_last():
        idx_copy(slot).wait()
        start_scatters(slot)

        @pl.when(i >= 1)
        def _():
            wait_scatters(1 - slot)

        wait_scatters(slot)

        meta = jnp.concatenate([tbl_ref[...], jnp.broadcast_to(nfree_ref[...], (SUBLANES, LANES))],
                               axis=0).astype(I32)
        meta_ref[...] = meta
        state = jnp.concatenate([fill_ref[...], cur_ref[...], nfree_ref[...],
                                 jnp.zeros((SUBLANES - 3, LANES), F32)], axis=0).astype(I32)
        mbuf[...] = state
        mcopy = pltpu.make_async_copy(mbuf, msm, msem.at[0])
        mcopy.start()
        mcopy.wait()

        def zero_copy(first_row, n_rows):
            return pltpu.make_async_copy(
                zbuf.at[pl.ds(0, n_rows * sub), :],
                xs_hbm.at[pl.ds(pl.multiple_of(first_row * sub, sub), n_rows * sub), :], zsem.at[0])

        chunks = [rows >> (s + 1) for s in range(rows.bit_length() - 1)]
        plans = []
        for e in range(N_EXPERTS):
            rem = rows - msm[0, e]
            at = msm[1, e] * rows + msm[0, e]
            for c in chunks:
                take = (rem & c) != 0
                plans.append((take, zero_copy(at, c)))
                at = at + jnp.where(take, c, 0)
        for j in range(N_EXPERTS):
            blk = msm[2, 0] + j
            safe = jnp.minimum(blk, n_blocks - 1)
            plans.append((blk < n_blocks, zero_copy(safe * rows, rows)))
        for take, cp in plans:
            pl.when(take)(cp.start)
        for take, cp in plans:
            pl.when(take)(cp.wait)

    @pl.when(i > 0)
    def _():
        idx_copy(1 - slot).wait()

    @pl.when(i >= 2)
    def _():
        wait_scatters(slot)

    def inline_scatters(s):
        for t in range(tm):
            src = hbuf.at[s, pl.ds(t * sub, sub), :]
            for k in range(TOP_K):
                dst = dsm[s, k, t]
                pltpu.make_async_copy(src, xs_hbm.at[pl.ds(pl.multiple_of(dst, sub), sub), :],
                                      ssem.at[s]).start(priority=k % 2)

    @pl.when(i == 0)
    def _():
        tile_step(False)

    @pl.when(i > 0)
    def _():
        tile_step(True)

    idx_copy(slot).start()

    @pl.when(i == nt - 1)
    def _():
        flush_last()


def _outproj(yf, ym, gates, x2, bm, wfo, wmo, wo, gffn, wrh, wrl, br, lstrict, ustrict, *, n_blocks):
    n, d = x2.shape
    tm = MOE_TILE
    rows = EXPERT_ROWS
    sub = d // LANES
    const = lambda i: (0, 0)
    row = lambda i: (i, 0)
    full = lambda a: pl.BlockSpec(a.shape, const)
    consts = (bm, wfo, wmo, wo, gffn, wrh, wrl, br, lstrict, ustrict)
    return pl.pallas_call(
        _outproj_kernel,
        out_shape=[jax.ShapeDtypeStruct((n, d), F32),
                   jax.ShapeDtypeStruct((n, LANES), F32),
                   jax.ShapeDtypeStruct((n // tm, SUBLANES, tm), I32),
                   jax.ShapeDtypeStruct((2 * SUBLANES, LANES), I32),
                   jax.ShapeDtypeStruct((n_blocks * rows * sub, LANES), F32)],
        grid=(n // tm,),
        in_specs=[pl.BlockSpec((tm, yf.shape[1]), row), pl.BlockSpec((tm, ym.shape[1]), row),
                  pl.BlockSpec((tm, gates.shape[1]), row), pl.BlockSpec((tm, d), row)]
                 + [full(a) for a in consts],
        out_specs=[pl.BlockSpec((tm, d), row), pl.BlockSpec((tm, LANES), row),
                   pl.BlockSpec((1, SUBLANES, tm), lambda i: (i, 0, 0)),
                   pl.BlockSpec((2 * SUBLANES, LANES), const),
                   pl.BlockSpec(memory_space=pl.ANY)],
        scratch_shapes=[pltpu.VMEM((1, LANES), F32), pltpu.VMEM((1, LANES), F32),
                        pltpu.VMEM((1, LANES), F32), pltpu.VMEM((SUBLANES, LANES), F32),
                        pltpu.VMEM((2, tm * sub, LANES), F32),
                        pltpu.VMEM((2, SUBLANES, tm), I32),
                        pltpu.SMEM((2, SUBLANES, tm), I32),
                        pltpu.VMEM((rows * sub, LANES), F32),
                        pltpu.VMEM((SUBLANES, LANES), I32),
                        pltpu.SMEM((SUBLANES, LANES), I32),
                        pltpu.SemaphoreType.DMA((2,)), pltpu.SemaphoreType.DMA((2,)),
                        pltpu.SemaphoreType.DMA((1,)), pltpu.SemaphoreType.DMA((1,))],
        compiler_params=pltpu.CompilerParams(dimension_semantics=("arbitrary",),
                                             vmem_limit_bytes=VMEM_LIMIT),
        name="outproj_router",
    )(yf, ym, gates, x2, *consts)


def _expert_kernel(order_ref, be_ref, nused_ref,
                   xs_ref, wgu_ref, bgu_ref, wd_ref, bd_ref, ys_ref, wgu_b, wd_b):
    b = pl.program_id(0)
    rows = EXPERT_ROWS
    sub = xs_ref.shape[0] // rows

    @pl.when(b < nused_ref[0])
    def _():
        changed = jnp.logical_or(b == 0, be_ref[b] != be_ref[jnp.maximum(b - 1, 0)])

        @pl.when(changed)
        def _():
            wgu_b[...] = wgu_ref[0].astype(BF16)
            wd_b[...] = wd_ref[0].astype(BF16)

        x = jnp.concatenate(
            [xs_ref[pl.ds(s, rows, stride=sub), :].astype(BF16) for s in range(sub)], axis=1)
        gu = _dot(x, wgu_b[...]) + bgu_ref[0]
        gate = jnp.minimum(gu[:, :EXPERT_FF], SWIGLU_LIMIT)
        up = jnp.clip(gu[:, EXPERT_FF:], -SWIGLU_LIMIT, SWIGLU_LIMIT)
        glu = gate * (1.0 / (1.0 + jnp.exp(-SWIGLU_ALPHA * gate)))
        y = _dot(((up + 1.0) * glu).astype(BF16), wd_b[...]) + bd_ref[0]
        for s in range(sub):
            ys_ref[pl.ds(s, rows, stride=sub), :] = y[:, s * LANES:(s + 1) * LANES]

    @pl.when(b >= nused_ref[0])
    def _():
        ys_ref[...] = jnp.zeros_like(ys_ref)


def _experts(order, block_e, nused, xs, w_gu, b_gu, w_d, b_d):
    e, d, ff2 = w_gu.shape
    rows = EXPERT_ROWS
    sub = d // LANES
    n_blocks = xs.shape[0] // (rows * sub)
    wmap = lambda b, od, be, nu: (be[b], 0, 0)
    xmap = lambda b, od, be, nu: (od[b], 0)
    grid_spec = pltpu.PrefetchScalarGridSpec(
        num_scalar_prefetch=3,
        grid=(n_blocks,),
        in_specs=[pl.BlockSpec((rows * sub, LANES), xmap),
                  pl.BlockSpec((1, d, ff2), wmap), pl.BlockSpec((1, 1, ff2), wmap),
                  pl.BlockSpec((1, ff2 // 2, d), wmap), pl.BlockSpec((1, 1, d), wmap)],
        out_specs=pl.BlockSpec((rows * sub, LANES), xmap),
        scratch_shapes=[pltpu.VMEM((d, ff2), BF16), pltpu.VMEM((ff2 // 2, d), BF16)],
    )
    return pl.pallas_call(
        _expert_kernel,
        out_shape=jax.ShapeDtypeStruct(xs.shape, F32),
        grid_spec=grid_spec,
        compiler_params=pltpu.CompilerParams(dimension_semantics=("arbitrary",),
                                             vmem_limit_bytes=VMEM_LIMIT),
        name="experts",
    )(order, block_e, nused, xs, w_gu, b_gu.reshape(e, 1, ff2), w_d, b_d.reshape(e, 1, d))


def _combine_kernel(x1_ref, route_ref, dest_hbm, ys_hbm, g_ref, o_ref, ybuf, dsm, gsem, isem):
    i = pl.program_id(0)
    nt = pl.num_programs(0)
    tm, d = x1_ref.shape
    sub = d // LANES
    slot = i % 2

    def idx_copy(tile):
        s = tile % 2
        return pltpu.make_async_copy(dest_hbm.at[tile], dsm.at[s], isem.at[s])

    def start_gathers(tile):
        for static_s in range(2):
            @pl.when(tile % 2 == static_s)
            def _(static_s=static_s):
                def body(t, _):
                    for k in range(TOP_K):
                        src = dsm[static_s, k, t]
                        pltpu.make_async_copy(
                            ys_hbm.at[pl.ds(pl.multiple_of(src, sub), sub), :],
                            ybuf.at[static_s, pl.ds(pl.multiple_of((k * tm + t) * sub, sub), sub), :],
                            gsem.at[static_s]).start(priority=k % 2)
                    return 0

                lax.fori_loop(0, tm, body, 0, unroll=4)

    def wait_gathers(s):
        pltpu.make_async_copy(ys_hbm.at[pl.ds(0, TOP_K * tm * sub), :], ybuf.at[s], gsem.at[s]).wait()

    @pl.when(i == 0)
    def _():
        idx_copy(0).start()
        idx_copy(0).wait()
        start_gathers(0)

        @pl.when(nt > 1)
        def _():
            idx_copy(1).start()

    @pl.when(i + 1 < nt)
    def _():
        idx_copy(i + 1).wait()

    wait_gathers(slot)

    def inline_gathers(s):
        for t in range(tm):
            for k in range(TOP_K):
                src = dsm[s, k, t]
                pltpu.make_async_copy(
                    ys_hbm.at[pl.ds(pl.multiple_of(src, sub), sub), :],
                    ybuf.at[s, pl.ds((k * tm + t) * sub, sub), :],
                    gsem.at[s]).start(priority=k % 2)

    def tile_step(prefetch_next):
        if prefetch_next:
            inline_gathers(1 - slot)
        route = route_ref[...]
        ws = [route[:, ROUTE_W + k:ROUTE_W + k + 1] for k in range(TOP_K)]
        chunks = []
        ssq = jnp.zeros((tm, 1), F32)
        for s in range(sub):
            acc = x1_ref[:, s * LANES:(s + 1) * LANES]
            for k in range(TOP_K):
                acc = acc + ws[k] * ybuf[slot, pl.ds(k * tm * sub + s, tm, stride=sub), :]
            chunks.append(acc)
            ssq = ssq + jnp.sum(acc * acc, axis=-1, keepdims=True)
        inv = lax.rsqrt(ssq / d + NORM_EPS)
        for s in range(sub):
            sl = slice(s * LANES, (s + 1) * LANES)
            o_ref[:, sl] = chunks[s] * inv * g_ref[:, sl]

    @pl.when(i + 1 < nt)
    def _():
        tile_step(True)

    @pl.when(i + 1 >= nt)
    def _():
        tile_step(False)

    @pl.when(i + 2 < nt)
    def _():
        idx_copy(i + 2).start()


def _combine(x1, route, dest_t, ys, g_final):
    n, d = x1.shape
    tm = MOE_TILE
    sub = d // LANES
    row = lambda i: (i, 0)
    return pl.pallas_call(
        _combine_kernel,
        out_shape=jax.ShapeDtypeStruct((n, d), F32),
        grid=(n // tm,),
        in_specs=[pl.BlockSpec((tm, d), row), pl.BlockSpec((tm, LANES), row),
                  pl.BlockSpec(memory_space=pl.ANY), pl.BlockSpec(memory_space=pl.ANY),
                  pl.BlockSpec((1, d), lambda i: (0, 0))],
        out_specs=pl.BlockSpec((tm, d), row),
        scratch_shapes=[pltpu.VMEM((2, TOP_K * tm * sub, LANES), F32),
                        pltpu.SMEM((2, SUBLANES, tm), I32),
                        pltpu.SemaphoreType.DMA((2,)), pltpu.SemaphoreType.DMA((2,))],
        compiler_params=pltpu.CompilerParams(dimension_semantics=("arbitrary",),
                                             vmem_limit_bytes=VMEM_LIMIT),
        name="combine_norm",
    )(x1, route, dest_t, ys, g_final)


def _pad_heads(w, per_head, width=LANES):
    k = w.shape[0]
    w = w.reshape(k, HEADS, per_head)
    w = jnp.pad(w, ((0, 0), (0, 0), (0, width - per_head)))
    return w.reshape(k, HEADS * width)


def _aug_constants():
    place = np.zeros((LANES, 2 * LANES), np.float32)
    ones = np.zeros((1, 2 * LANES), np.float32)
    for hd in range(HEADS):
        for piece in range(3):
            src = FG_LO + piece * HEADS + hd
            place[src, 16 * hd + piece] = 1.0
            place[src, LANES + 16 * hd + 3 + piece] = -1.0
            ones[0, 16 * hd + 3 + piece] = 1.0
            ones[0, LANES + 16 * hd + piece] = 1.0
    return jnp.asarray(place, BF16), jnp.asarray(ones, F32)


def _layer(x2, pos2, batch, seq_len, g_attn_norm, w_in, b_fgate, g_q_a, w_q_b, g_kv_a, w_kv_b,
           w_fox_out, w_mla_out, b_merge, w_o, g_ffn_norm, w_router, b_router, w_gu, b_gu,
           w_down, b_down, g_out):
    n, d = x2.shape
    fw = HEADS * HEAD_DIM
    o = 0
    w_qf = w_in[:, o:o + fw]; o += fw
    w_kf = w_in[:, o:o + fw]; o += fw
    w_vf = w_in[:, o:o + fw]; o += fw
    w_f = w_in[:, o:o + HEADS]; o += HEADS
    w_ql = w_in[:, o:o + Q_RANK]; o += Q_RANK
    w_ckv = w_in[:, o:o + KV_RANK]; o += KV_RANK
    w_kpe = w_in[:, o:o + ROPE_DIM]; o += ROPE_DIM
    w_gate = w_in[:, o:]

    wq = (w_qf * (HEAD_DIM ** -0.5)).astype(BF16)
    wk = w_kf.astype(BF16)
    wmisc = jnp.concatenate([jnp.zeros((d, PE_LO), F32), w_kpe, w_f,
                             jnp.zeros((d, LANES - FG_LO - HEADS), F32)], axis=1)
    wlat = jnp.concatenate([w_ql, w_ckv, wmisc], axis=1).astype(BF16)
    bf128 = jnp.zeros((1, LANES), F32).at[0, FG_LO:FG_LO + HEADS].set(b_fgate)
    wqb = _pad_heads(w_q_b, HEAD_DIM + ROPE_DIM).astype(BF16)
    wkv = w_kv_b.reshape(KV_RANK, HEADS, 2 * HEAD_DIM)
    wkbk = wkv[:, :, :HEAD_DIM].reshape(KV_RANK, fw).astype(BF16)
    wkbv = wkv[:, :, HEAD_DIM:].reshape(KV_RANK, fw).T.astype(BF16)
    ltri = jnp.asarray(np.tril(np.ones((IN_TILE, IN_TILE), np.float32)), BF16)
    lstrict = jnp.asarray(np.tril(np.ones((MOE_TILE, MOE_TILE), np.float32), -1), BF16)
    ustrict = jnp.asarray(np.triu(np.ones((LANES, LANES), np.float32), 1), BF16)
    place, ones = _aug_constants()
    half = ROPE_DIM // 2
    inv_freq = ROPE_THETA ** (-jnp.arange(half, dtype=F32) / half)
    freq = jnp.zeros((1, LANES), F32).at[0, PE_LO:PE_MID].set(inv_freq).at[0, PE_MID:PE_HI].set(inv_freq)

    qf, kf, vf, qm, km, vm, gates = _inproj(
        x2, pos2, g_attn_norm.reshape(1, d), wq, wk, w_vf.T.astype(BF16), wlat, w_gate.astype(BF16),
        bf128, g_q_a.reshape(1, -1), wqb, g_kv_a.reshape(1, -1), wkbk, wkbv, ltri, place, ones,
        freq, seq_len=seq_len)

    y_fox = _attention(qf, kf, vf, batch=batch, seq_len=seq_len, chunk_mask=False)
    y_mla = _attention(qm, km, vm, batch=batch, seq_len=seq_len, chunk_mask=True)

    wr = jnp.pad(w_router, ((0, 0), (0, LANES - N_EXPERTS)))
    wrh = wr.astype(BF16)
    wrl = (wr - wrh.astype(F32)).astype(BF16)
    br = jnp.full((1, LANES), NEG, F32).at[0, :N_EXPERTS].set(b_router)
    n_blocks = n * TOP_K // EXPERT_ROWS + N_EXPERTS
    x1, route, dest_t, meta, xs = _outproj(
        y_fox, y_mla, gates, x2, b_merge.reshape(1, -1), w_fox_out.astype(BF16),
        w_mla_out.astype(BF16), w_o.astype(BF16), g_ffn_norm.reshape(1, d), wrh, wrl, br, lstrict,
        ustrict, n_blocks=n_blocks)

    block_e = meta[:SUBLANES].reshape(-1)[:n_blocks]
    order = jnp.argsort(block_e, stable=True).astype(I32)
    nused = meta[SUBLANES, 0:1]
    be_sorted = jnp.minimum(block_e[order], N_EXPERTS - 1).astype(I32)
    be_sorted = jnp.where(jnp.arange(n_blocks) < nused[0], be_sorted,
                          be_sorted[jnp.maximum(nused[0] - 1, 0)])
    ys = _experts(order, be_sorted, nused, xs, w_gu, b_gu, w_down, b_down)
    return _combine(x1, route, dest_t, ys, g_out.reshape(1, d))


def kernel(x, positions, g_attn_norm, w_in, b_fgate, g_q_a, w_q_b, g_kv_a, w_kv_b, w_fox_out, w_mla_out, b_merge, w_o, g_ffn_norm, w_router, b_router, w_gu, b_gu, w_down, b_down, g_final):
    batch, seq_len, d = x.shape
    depth = w_in.shape[0]
    assert depth == 1, "the fused combine + final-norm kernel assumes a single layer"
    assert seq_len % ATT_TILE == 0 and d % LANES == 0
    assert (batch * seq_len * TOP_K) % EXPERT_ROWS == 0
    assert batch * seq_len * TOP_K // EXPERT_ROWS + N_EXPERTS <= SUBLANES * LANES
    x2 = x.reshape(batch * seq_len, d)
    pos2 = positions.reshape(batch * seq_len, 1).astype(I32)
    out = _layer(x2, pos2, batch, seq_len, g_attn_norm[0], w_in[0], b_fgate[0], g_q_a[0], w_q_b[0],
                 g_kv_a[0], w_kv_b[0], w_fox_out[0], w_mla_out[0], b_merge[0], w_o[0],
                 g_ffn_norm[0], w_router[0], b_router[0], w_gu[0], b_gu[0], w_down[0], b_down[0],
                 g_final)
    return out.reshape(batch, seq_len, d)
```

```python
import functools

import jax
import jax.numpy as jnp
import numpy as np
from jax import lax
from jax.experimental import pallas as pl
from jax.experimental.pallas import tpu as pltpu

F32 = jnp.float32
BF16 = jnp.bfloat16
I32 = jnp.int32

LANES = 128
SUBLANES = 8
VMEM_LIMIT = 56 * 1024 * 1024

NORM_EPS = 1e-6
HEADS = 8
HEAD_DIM = 64
ROPE_DIM = 32
Q_RANK = 256
KV_RANK = 128
N_EXPERTS = 32
TOP_K = 4
EXPERT_FF = 1024
SWIGLU_LIMIT = 7.0
SWIGLU_ALPHA = 1.702
ROPE_THETA = 10000.0
CHUNK = 64

NEG = -1e30
LOG2E = 1.4426950408889634

ROW_TILE = 256
IN_TILE = 512
MOE_TILE = 512
ATT_TILE = 512
ATT_HEADS = 2
ATT_GROUP = 8
EXPERT_ROWS = MOE_TILE

PE_LO, PE_MID, PE_HI = 64, 80, 96
FG_LO = 96
AUG_LO = 64
ROUTE_DEST = 0
ROUTE_W = 8


def _dot(a, b):
    return jnp.dot(a, b, preferred_element_type=F32)


def _dot_nt(a, b):
    return lax.dot_general(a, b, (((1,), (1,)), ((), ())), preferred_element_type=F32)


def _split3(a):
    hi = a.astype(BF16)
    r1 = a - hi.astype(F32)
    mid = r1.astype(BF16)
    lo = (r1 - mid.astype(F32)).astype(BF16)
    return hi, mid, lo


def _rms(x, g):
    return x * lax.rsqrt(jnp.mean(x * x, axis=-1, keepdims=True) + NORM_EPS) * g


def _inproj_kernel(x_ref, pos_ref, g_ref, wq_ref, wk_ref, wv_ref, wlat_ref, wg_ref, bf_ref,
                   gq_ref, wqb_ref, gkv_ref, wkbk_ref, wkbv_ref, ltri_ref, place_ref, ones_ref,
                   freq_ref,
                   qf_ref, kf_ref, vf_ref, qm_ref, km_ref, vm_ref, gate_ref,
                   carry_ref, *, tiles_per_seq, mla_scale):
    i = pl.program_id(0)

    @pl.when(i % tiles_per_seq == 0)
    def _():
        carry_ref[...] = jnp.zeros_like(carry_ref)

    h = _rms(x_ref[...], g_ref[...]).astype(BF16)
    tm = h.shape[0]
    lane = lax.broadcasted_iota(I32, (tm, LANES), 1)

    lat = _dot(h, wlat_ref[...])
    q_lat = lat[:, :Q_RANK]
    c_kv = lat[:, Q_RANK:Q_RANK + KV_RANK]
    misc = lat[:, Q_RANK + KV_RANK:]

    qn = _rms(q_lat, gq_ref[...]).astype(BF16)
    kvn = _rms(c_kv, gkv_ref[...]).astype(BF16)
    qfull = _dot(qn, wqb_ref[...])
    knope = _dot(kvn, wkbk_ref[...])
    vmt = _dot_nt(wkbv_ref[...], kvn).astype(BF16)
    for c in range(tm // ROW_TILE):
        vm_ref[c] = vmt[:, c * ROW_TILE:(c + 1) * ROW_TILE]

    z = misc + bf_ref[...]
    logf = jnp.minimum(z, 0.0) - jnp.log1p(jnp.exp(-jnp.abs(z)))
    fmask = (lane >= FG_LO) & (lane < FG_LO + HEADS)
    logf = jnp.where(fmask, logf, 0.0)
    l_hi, l_mid, l_lo = _split3(logf)
    ltri = ltri_ref[...]
    c = _dot(ltri, l_hi) + _dot(ltri, l_mid) + _dot(ltri, l_lo) + carry_ref[...]
    carry_ref[...] = c[tm - 1:tm, :]
    c_hi, c_mid, c_lo = _split3(c * LOG2E)
    c3 = (c_hi.astype(F32) + pltpu.roll(c_mid.astype(F32), HEADS, 1)
          + pltpu.roll(c_lo.astype(F32), 2 * HEADS, 1)).astype(BF16)
    augc = _dot(c3, place_ref[...]) + ones_ref[...]
    low = lane < HEAD_DIM
    amask = (lane >= AUG_LO) & (lane < AUG_LO + 6)

    def aug_block(half, hd):
        src = augc[:, half * LANES:(half + 1) * LANES]
        return jnp.where(amask, pltpu.roll(src, (AUG_LO - 16 * hd) % LANES, 1), 0.0)

    def head_block(compact, hd):
        pair = compact[:, (hd // 2) * LANES:(hd // 2 + 1) * LANES]
        if hd % 2:
            pair = pltpu.roll(pair, HEAD_DIM, 1)
        return jnp.where(low, pair, 0.0)

    qc = _dot(h, wq_ref[...]) * LOG2E
    kc = _dot(h, wk_ref[...])
    vft = _dot_nt(wv_ref[...], h).astype(BF16)
    for c in range(tm // ROW_TILE):
        vf_ref[c] = vft[:, c * ROW_TILE:(c + 1) * ROW_TILE]
    gate_ref[...] = _dot(h, wg_ref[...]).astype(BF16)

    for hd in range(HEADS):
        sl = slice(hd * LANES, (hd + 1) * LANES)
        qf_ref[:, sl] = (head_block(qc, hd) + aug_block(0, hd)).astype(BF16)
        kf_ref[:, sl] = (head_block(kc, hd) + aug_block(1, hd)).astype(BF16)

    ang = pos_ref[...].astype(F32) * freq_ref[...]
    cosv = jnp.cos(ang)
    sinv = jnp.sin(ang)
    s1 = jnp.where((lane >= PE_LO) & (lane < PE_MID), -sinv, 0.0)
    s2 = jnp.where((lane >= PE_MID) & (lane < PE_HI), sinv, 0.0)

    def rope(v):
        return v * cosv + pltpu.roll(v, LANES - 16, 1) * s1 + pltpu.roll(v, 16, 1) * s2

    kpe = jnp.where((lane >= PE_LO) & (lane < PE_HI), rope(misc), 0.0)
    for hd in range(HEADS):
        sl = slice(hd * LANES, (hd + 1) * LANES)
        qm_ref[:, sl] = (rope(qfull[:, sl]) * mla_scale).astype(BF16)
        km_ref[:, sl] = (head_block(knope, hd) + kpe).astype(BF16)


def _inproj(x2, pos2, g_attn, wq, wk, wv, wlat, wg, bf128, gq, wqb, gkv, wkbk, wkbv, ltri,
            place, ones, freq, *, seq_len):
    n, d = x2.shape
    tm = IN_TILE
    hw = HEADS * LANES
    vw = HEADS * HEAD_DIM
    const = lambda i: (0, 0)
    row = lambda i: (i, 0)
    full = lambda a: pl.BlockSpec(a.shape, const)
    rows_out = lambda w: (jax.ShapeDtypeStruct((n, w), BF16), pl.BlockSpec((tm, w), row))
    vt_out = (jax.ShapeDtypeStruct((n // ROW_TILE, vw, ROW_TILE), BF16),
              pl.BlockSpec((tm // ROW_TILE, vw, ROW_TILE), lambda i: (i, 0, 0)))
    outs = [rows_out(hw), rows_out(hw), vt_out, rows_out(hw), rows_out(hw), vt_out,
            rows_out(wg.shape[1])]
    consts = (g_attn, wq, wk, wv, wlat, wg, bf128, gq, wqb, gkv, wkbk, wkbv, ltri, place, ones, freq)
    return pl.pallas_call(
        functools.partial(_inproj_kernel, tiles_per_seq=seq_len // tm,
                          mla_scale=float((HEAD_DIM + ROPE_DIM) ** -0.5) * LOG2E),
        out_shape=[o[0] for o in outs],
        grid=(n // tm,),
        in_specs=[pl.BlockSpec((tm, d), row), pl.BlockSpec((tm, 1), row)] + [full(a) for a in consts],
        out_specs=[o[1] for o in outs],
        scratch_shapes=[pltpu.VMEM((1, LANES), F32)],
        compiler_params=pltpu.CompilerParams(dimension_semantics=("arbitrary",),
                                             vmem_limit_bytes=VMEM_LIMIT),
        name="inproj",
    )(x2, pos2, *consts)


def _attn_kernel(q_ref, k_ref, vt_ref, o_ref, *, chunk_mask):
    i = pl.program_id(2)
    tq = q_ref.shape[0]
    tk = vt_ref.shape[2]

    def scores(j, masked, lo=0):
        start = pl.multiple_of(j * tk, tk)
        ss = [_dot_nt(k_ref[pl.ds(start, tk), hd * LANES:(hd + 1) * LANES],
                      q_ref[lo:, hd * LANES:(hd + 1) * LANES])
              for hd in range(ATT_HEADS)]
        if masked:
            keyg = lax.broadcasted_iota(I32, (tk, tq - lo), 0) + j * tk
            qlo = lax.broadcasted_iota(I32, (tk, tq - lo), 1) + (i * tq + lo)
            allowed = (keyg // CHUNK) <= (qlo // CHUNK) if chunk_mask else keyg <= qlo
            ss = [jnp.where(allowed, s, NEG) for s in ss]
        return ss

    ones = jnp.ones((2 * SUBLANES, tk), BF16)

    def update(j, ss, state, lo=0):
        vt = vt_ref[j]
        new = []
        for hd, s in enumerate(ss):
            m_all, acc_all = state[hd]
            m, acc = m_all[:, lo:], acc_all[:, lo:]
            m_new = jnp.maximum(m, jnp.max(s, axis=0, keepdims=True))
            alpha = jnp.exp2(m - m_new)
            p = jnp.exp2((s - m_new[0:1]).astype(BF16))
            va = jnp.concatenate([vt[hd * HEAD_DIM:(hd + 1) * HEAD_DIM, :], ones], axis=0)
            acc = alpha[0:1] * acc + _dot(va, p)
            if lo:
                m_new = jnp.concatenate([m_all[:, :lo], m_new], axis=1)
                acc = jnp.concatenate([acc_all[:, :lo], acc], axis=1)
            new.append((m_new, acc))
        return tuple(new)

    n_full = (i * tq) // tk
    group = max(1, tq // tk)

    def run_group(first, state, n_plain, n_masked=0):
        los = [0] * n_plain + [u * tk for u in range(n_masked)]
        msk = [False] * n_plain + [True] * n_masked
        nblk = n_plain + n_masked
        ss = scores(first, msk[0], los[0])
        for u in range(nblk):
            nxt = scores(first + u + 1, msk[u + 1], los[u + 1]) if u + 1 < nblk else None
            state = update(first + u, ss, state, los[u])
            ss = nxt
        return state

    long = ATT_GROUP * group
    n_long = n_full // long
    init1 = (jnp.full((SUBLANES, tq), NEG, F32), jnp.zeros((HEAD_DIM + 2 * SUBLANES, tq), F32))
    state = lax.fori_loop(0, n_long, lambda jj, c: run_group(jj * long, c, long),
                          (init1,) * ATT_HEADS)
    done = n_long * long
    left = (n_full - done) // group
    tails = [functools.partial(run_group, done, n_plain=r * group, n_masked=group)
             for r in range(ATT_GROUP)]

    def pick(lo, hi, st):
        if hi - lo == 1:
            return tails[lo](st)
        mid = (lo + hi) // 2
        return lax.cond(left < mid, lambda s: pick(lo, mid, s), lambda s: pick(mid, hi, s), st)

    state = pick(0, ATT_GROUP, state)
    out_t = jnp.concatenate([acc[:HEAD_DIM] / acc[HEAD_DIM:HEAD_DIM + 1] for _, acc in state],
                            axis=0)
    o_ref[...] = out_t.T.astype(o_ref.dtype)


def _attention(q, k, vt, *, batch, seq_len, chunk_mask):
    n = q.shape[0]
    t = ATT_TILE
    tk = vt.shape[2]
    nq = seq_len // t
    nkb = seq_len // tk
    hp = ATT_HEADS
    pairs = HEADS // hp
    return pl.pallas_call(
        functools.partial(_attn_kernel, chunk_mask=chunk_mask),
        out_shape=jax.ShapeDtypeStruct((n, HEADS * HEAD_DIM), BF16),
        grid=(batch, pairs, nq),
        in_specs=[pl.BlockSpec((t, hp * LANES), lambda b, p, i: (b * nq + i, p)),
                  pl.BlockSpec((seq_len, hp * LANES), lambda b, p, i: (b, p)),
                  pl.BlockSpec((nkb, hp * HEAD_DIM, tk), lambda b, p, i: (b, p, 0))],
        out_specs=pl.BlockSpec((t, hp * HEAD_DIM), lambda b, p, i: (b * nq + i, p)),
        compiler_params=pltpu.CompilerParams(
            dimension_semantics=("arbitrary", "arbitrary", "arbitrary"),
            vmem_limit_bytes=VMEM_LIMIT),
        name="attn_mla" if chunk_mask else "attn_fox",
    )(q, k, vt)


def _outproj_kernel(yf_ref, ym_ref, gate_ref, x_ref, bm_ref, wfo_ref, wmo_ref, wo_ref, gffn_ref,
                    wrh_ref, wrl_ref, br_ref, lstrict_ref, ustrict_ref,
                    x1_ref, route_ref, dest_ref, meta_ref, xs_hbm,
                    fill_ref, cur_ref, nfree_ref, tbl_ref,
                    hbuf, dbuf, dsm, zbuf, mbuf, msm, ssem, isem, zsem, msem):
    i = pl.program_id(0)
    nt = pl.num_programs(0)
    d = x_ref.shape[1]
    tm = x_ref.shape[0]
    sub = d // LANES
    rows = EXPERT_ROWS
    n_blocks = xs_hbm.shape[0] // (rows * sub)
    slot = i % 2

    def idx_copy(s):
        return pltpu.make_async_copy(dbuf.at[s], dsm.at[s], isem.at[s])

    def start_scatters(s):
        for static_s in range(2):
            @pl.when(s == static_s)
            def _(static_s=static_s):
                def body(t, _):
                    src = hbuf.at[static_s, pl.ds(pl.multiple_of(t * sub, sub), sub), :]
                    for k in range(TOP_K):
                        dst = dsm[static_s, k, t]
                        pltpu.make_async_copy(
                            src, xs_hbm.at[pl.ds(pl.multiple_of(dst, sub), sub), :],
                            ssem.at[static_s]).start(priority=k % 2)
                    return 0

                lax.fori_loop(0, tm, body, 0, unroll=4)

    def wait_scatters(s):
        for _ in range(TOP_K):
            pltpu.make_async_copy(hbuf.at[s], xs_hbm.at[pl.ds(0, tm * sub), :], ssem.at[s]).wait()

    @pl.when(i == 0)
    def _():
        fill_ref[...] = jnp.full_like(fill_ref, float(rows))
        cur_ref[...] = jnp.zeros_like(cur_ref)
        nfree_ref[...] = jnp.zeros_like(nfree_ref)
        tbl_ref[...] = jnp.full_like(tbl_ref, float(N_EXPERTS))
        zbuf[...] = jnp.zeros_like(zbuf)

    def tile_step(scatter_prev):
        if scatter_prev:
            inline_scatters(1 - slot)
        a = _dot(yf_ref[...], wfo_ref[...])
        b = _dot(ym_ref[...], wmo_ref[...])
        g = 1.0 / (1.0 + jnp.exp(-(gate_ref[...].astype(F32) + bm_ref[...])))
        merged = (g[:, :d] * a + g[:, d:] * b).astype(BF16)
        x1 = x_ref[...] + _dot(merged, wo_ref[...])
        x1_ref[...] = x1
        h2 = _rms(x1, gffn_ref[...])

        hi = h2.astype(BF16)
        lo = (h2 - hi.astype(F32)).astype(BF16)
        wrh = wrh_ref[...]
        logits = _dot(hi, wrh) + _dot(lo, wrh) + _dot(hi, wrl_ref[...]) + br_ref[...]

        lane = lax.broadcasted_iota(I32, (tm, LANES), 1)
        vals = logits
        sels, tops = [], []
        for _ in range(TOP_K):
            mx = jnp.max(vals, axis=-1, keepdims=True)
            idx = jnp.min(jnp.where(vals == mx, lane, LANES), axis=-1, keepdims=True)
            sel = lane == idx
            vals = jnp.where(sel, NEG, vals)
            sels.append(sel)
            tops.append(mx)
        es = [jnp.exp(tv - tops[0]) for tv in tops]
        den = es[0] + es[1] + es[2] + es[3]

        onehot = jnp.zeros((tm, LANES), F32)
        for sel in sels:
            onehot = onehot + sel.astype(F32)
        before = _dot(lstrict_ref[...], onehot.astype(BF16))
        cnt = jnp.sum(onehot, axis=0, keepdims=True)

        fill = fill_ref[...]
        cur = cur_ref[...]
        nfree = nfree_ref[...]
        need = ((fill + cnt) > float(rows)).astype(F32)
        need8 = jnp.broadcast_to(need, (SUBLANES, LANES)).astype(BF16)
        newid = nfree + _dot(need8, ustrict_ref[...])[0:1, :]
        pos = fill + before
        dest = jnp.where(pos < float(rows), cur * rows + pos, newid * rows + pos - float(rows))
        fill_ref[...] = fill + cnt - need * float(rows)
        cur_ref[...] = jnp.where(need > 0, newid, cur)
        nfree_ref[...] = nfree + jnp.sum(need, axis=-1, keepdims=True)
        blk_id = (lax.broadcasted_iota(I32, (SUBLANES, LANES), 0) * LANES
                  + lax.broadcasted_iota(I32, (SUBLANES, LANES), 1)).astype(F32)
        tbl = tbl_ref[...]
        for e in range(N_EXPERTS):
            hit = (blk_id == newid[:, e:e + 1]) & (need[:, e:e + 1] > 0)
            tbl = jnp.where(hit, float(e), tbl)
        tbl_ref[...] = tbl

        route = jnp.zeros((tm, LANES), F32)
        for k in range(TOP_K):
            dest_k = jnp.sum(jnp.where(sels[k], dest, 0.0), axis=-1, keepdims=True)
            route = jnp.where(lane == ROUTE_DEST + k, dest_k, route)
            route = jnp.where(lane == ROUTE_W + k, es[k] / den, route)
        route_ref[...] = route
        dest_t = (route.T[0:SUBLANES, :] * float(sub)).astype(I32)
        dest_ref[0] = dest_t

        for s in range(sub):
            hbuf[slot, pl.ds(s, tm, stride=sub), :] = h2[:, s * LANES:(s + 1) * LANES]
        dbuf[slot] = dest_t

    def flush_last():
        idx_copy(slot).wait()
        start_scatters(slot)

        @pl.when(i >= 1)
        def _():
            wait_scatters(1 - slot)

        wait_scatters(slot)

        meta = jnp.concatenate([tbl_ref[...], jnp.broadcast_to(nfree_ref[...], (SUBLANES, LANES))],
                               axis=0).astype(I32)
        meta_ref[...] = meta
        state = jnp.concatenate([fill_ref[...], cur_ref[...], nfree_ref[...],
                                 jnp.zeros((SUBLANES - 3, LANES), F32)], axis=0).astype(I32)
        mbuf[...] = state
        mcopy = pltpu.make_async_copy(mbuf, msm, msem.at[0])
        mcopy.start()
        mcopy.wait()

        def zero_copy(first_row, n_rows):
            return pltpu.make_async_copy(
                zbuf.at[pl.ds(0, n_rows * sub), :],
                xs_hbm.at[pl.ds(pl.multiple_of(first_row * sub, sub), n_rows * sub), :], zsem.at[0])

        chunks = [rows >> (s + 1) for s in range(rows.bit_length() - 1)]
        plans = []
        for e in range(N_EXPERTS):
            rem = rows - msm[0, e]
            at = msm[1, e] * rows + msm[0, e]
            for c in chunks:
                take = (rem & c) != 0
                plans.append((take, zero_copy(at, c)))
                at = at + jnp.where(take, c, 0)
        for j in range(N_EXPERTS):
            blk = msm[2, 0] + j
            safe = jnp.minimum(blk, n_blocks - 1)
            plans.append((blk < n_blocks, zero_copy(safe * rows, rows)))
        for take, cp in plans:
            pl.when(take)(cp.start)
        for take, cp in plans:
            pl.when(take)(cp.wait)

    @pl.when(i > 0)
    def _():
        idx_copy(1 - slot).wait()

    @pl.when(i >= 2)
    def _():
        wait_scatters(slot)

    def inline_scatters(s):
        for t in range(tm):
            src = hbuf.at[s, pl.ds(t * sub, sub), :]
            for k in range(TOP_K):
                dst = dsm[s, k, t]
                pltpu.make_async_copy(src, xs_hbm.at[pl.ds(pl.multiple_of(dst, sub), sub), :],
                                      ssem.at[s]).start(priority=k % 2)

    @pl.when(i == 0)
    def _():
        tile_step(False)

    @pl.when(i > 0)
    def _():
        tile_step(True)

    idx_copy(slot).start()

    @pl.when(i == nt - 1)
    def _():
        flush_last()


def _outproj(yf, ym, gates, x2, bm, wfo, wmo, wo, gffn, wrh, wrl, br, lstrict, ustrict, *, n_blocks):
    n, d = x2.shape
    tm = MOE_TILE
    rows = EXPERT_ROWS
    sub = d // LANES
    const = lambda i: (0, 0)
    row = lambda i: (i, 0)
    full = lambda a: pl.BlockSpec(a.shape, const)
    consts = (bm, wfo, wmo, wo, gffn, wrh, wrl, br, lstrict, ustrict)
    return pl.pallas_call(
        _outproj_kernel,
        out_shape=[jax.ShapeDtypeStruct((n, d), F32),
                   jax.ShapeDtypeStruct((n, LANES), F32),
                   jax.ShapeDtypeStruct((n // tm, SUBLANES, tm), I32),
                   jax.ShapeDtypeStruct((2 * SUBLANES, LANES), I32),
                   jax.ShapeDtypeStruct((n_blocks * rows * sub, LANES), F32)],
        grid=(n // tm,),
        in_specs=[pl.BlockSpec((tm, yf.shape[1]), row), pl.BlockSpec((tm, ym.shape[1]), row),
                  pl.BlockSpec((tm, gates.shape[1]), row), pl.BlockSpec((tm, d), row)]
                 + [full(a) for a in consts],
        out_specs=[pl.BlockSpec((tm, d), row), pl.BlockSpec((tm, LANES), row),
                   pl.BlockSpec((1, SUBLANES, tm), lambda i: (i, 0, 0)),
                   pl.BlockSpec((2 * SUBLANES, LANES), const),
                   pl.BlockSpec(memory_space=pl.ANY)],
        scratch_shapes=[pltpu.VMEM((1, LANES), F32), pltpu.VMEM((1, LANES), F32),
                        pltpu.VMEM((1, LANES), F32), pltpu.VMEM((SUBLANES, LANES), F32),
                        pltpu.VMEM((2, tm * sub, LANES), F32),
                        pltpu.VMEM((2, SUBLANES, tm), I32),
                        pltpu.SMEM((2, SUBLANES, tm), I32),
                        pltpu.VMEM((rows * sub, LANES), F32),
                        pltpu.VMEM((SUBLANES, LANES), I32),
                        pltpu.SMEM((SUBLANES, LANES), I32),
                        pltpu.SemaphoreType.DMA((2,)), pltpu.SemaphoreType.DMA((2,)),
                        pltpu.SemaphoreType.DMA((1,)), pltpu.SemaphoreType.DMA((1,))],
        compiler_params=pltpu.CompilerParams(dimension_semantics=("arbitrary",),
                                             vmem_limit_bytes=VMEM_LIMIT),
        name="outproj_router",
    )(yf, ym, gates, x2, *consts)


def _expert_kernel(order_ref, be_ref, nused_ref,
                   xs_ref, wgu_ref, bgu_ref, wd_ref, bd_ref, ys_ref, wgu_b, wd_b):
    b = pl.program_id(0)
    rows = EXPERT_ROWS
    sub = xs_ref.shape[0] // rows

    @pl.when(b < nused_ref[0])
    def _():
        changed = jnp.logical_or(b == 0, be_ref[b] != be_ref[jnp.maximum(b - 1, 0)])

        @pl.when(changed)
        def _():
            wgu_b[...] = wgu_ref[0].astype(BF16)
            wd_b[...] = wd_ref[0].astype(BF16)

        x = jnp.concatenate(
            [xs_ref[pl.ds(s, rows, stride=sub), :].astype(BF16) for s in range(sub)], axis=1)
        gu = _dot(x, wgu_b[...]) + bgu_ref[0]
        gate = jnp.minimum(gu[:, :EXPERT_FF], SWIGLU_LIMIT)
        up = jnp.clip(gu[:, EXPERT_FF:], -SWIGLU_LIMIT, SWIGLU_LIMIT)
        glu = gate * (1.0 / (1.0 + jnp.exp(-SWIGLU_ALPHA * gate)))
        y = _dot(((up + 1.0) * glu).astype(BF16), wd_b[...]) + bd_ref[0]
        for s in range(sub):
            ys_ref[pl.ds(s, rows, stride=sub), :] = y[:, s * LANES:(s + 1) * LANES]

    @pl.when(b >= nused_ref[0])
    def _():
        ys_ref[...] = jnp.zeros_like(ys_ref)


def _experts(order, block_e, nused, xs, w_gu, b_gu, w_d, b_d):
    e, d, ff2 = w_gu.shape
    rows = EXPERT_ROWS
    sub = d // LANES
    n_blocks = xs.shape[0] // (rows * sub)
    wmap = lambda b, od, be, nu: (be[b], 0, 0)
    xmap = lambda b, od, be, nu: (od[b], 0)
    grid_spec = pltpu.PrefetchScalarGridSpec(
        num_scalar_prefetch=3,
        grid=(n_blocks,),
        in_specs=[pl.BlockSpec((rows * sub, LANES), xmap),
                  pl.BlockSpec((1, d, ff2), wmap), pl.BlockSpec((1, 1, ff2), wmap),
                  pl.BlockSpec((1, ff2 // 2, d), wmap), pl.BlockSpec((1, 1, d), wmap)],
        out_specs=pl.BlockSpec((rows * sub, LANES), xmap),
        scratch_shapes=[pltpu.VMEM((d, ff2), BF16), pltpu.VMEM((ff2 // 2, d), BF16)],
    )
    return pl.pallas_call(
        _expert_kernel,
        out_shape=jax.ShapeDtypeStruct(xs.shape, F32),
        grid_spec=grid_spec,
        compiler_params=pltpu.CompilerParams(dimension_semantics=("arbitrary",),
                                             vmem_limit_bytes=VMEM_LIMIT),
        name="experts",
    )(order, block_e, nused, xs, w_gu, b_gu.reshape(e, 1, ff2), w_d, b_d.reshape(e, 1, d))


def _combine_kernel(x1_ref, route_ref, dest_hbm, ys_hbm, g_ref, o_ref, ybuf, dsm, gsem, isem):
    i = pl.program_id(0)
    nt = pl.num_programs(0)
    tm, d = x1_ref.shape
    sub = d // LANES
    slot = i % 2

    def idx_copy(tile):
        s = tile % 2
        return pltpu.make_async_copy(dest_hbm.at[tile], dsm.at[s], isem.at[s])

    def start_gathers(tile):
        for static_s in range(2):
            @pl.when(tile % 2 == static_s)
            def _(static_s=static_s):
                def body(t, _):
                    for k in range(TOP_K):
                        src = dsm[static_s, k, t]
                        pltpu.make_async_copy(
                            ys_hbm.at[pl.ds(pl.multiple_of(src, sub), sub), :],
                            ybuf.at[static_s, pl.ds(pl.multiple_of((k * tm + t) * sub, sub), sub), :],
                            gsem.at[static_s]).start(priority=k % 2)
                    return 0

                lax.fori_loop(0, tm, body, 0, unroll=4)

    def wait_gathers(s):
        pltpu.make_async_copy(ys_hbm.at[pl.ds(0, TOP_K * tm * sub), :], ybuf.at[s], gsem.at[s]).wait()

    @pl.when(i == 0)
    def _():
        idx_copy(0).start()
        idx_copy(0).wait()
        start_gathers(0)

        @pl.when(nt > 1)
        def _():
            idx_copy(1).start()

    @pl.when(i + 1 < nt)
    def _():
        idx_copy(i + 1).wait()

    wait_gathers(slot)

    def inline_gathers(s):
        for t in range(tm):
            for k in range(TOP_K):
                src = dsm[s, k, t]
                pltpu.make_async_copy(
                    ys_hbm.at[pl.ds(pl.multiple_of(src, sub), sub), :],
                    ybuf.at[s, pl.ds((k * tm + t) * sub, sub), :],
                    gsem.at[s]).start(priority=k % 2)

    def tile_step(prefetch_next):
        if prefetch_next:
            inline_gathers(1 - slot)
        route = route_ref[...]
        ws = [route[:, ROUTE_W + k:ROUTE_W + k + 1] for k in range(TOP_K)]
        chunks = []
        ssq = jnp.zeros((tm, 1), F32)
        for s in range(sub):
            acc = x1_ref[:, s * LANES:(s + 1) * LANES]
            for k in range(TOP_K):
                acc = acc + ws[k] * ybuf[slot, pl.ds(k * tm * sub + s, tm, stride=sub), :]
            chunks.append(acc)
            ssq = ssq + jnp.sum(acc * acc, axis=-1, keepdims=True)
        inv = lax.rsqrt(ssq / d + NORM_EPS)
        for s in range(sub):
            sl = slice(s * LANES, (s + 1) * LANES)
            o_ref[:, sl] = chunks[s] * inv * g_ref[:, sl]

    @pl.when(i + 1 < nt)
    def _():
        tile_step(True)

    @pl.when(i + 1 >= nt)
    def _():
        tile_step(False)

    @pl.when(i + 2 < nt)
    def _():
        idx_copy(i + 2).start()


def _combine(x1, route, dest_t, ys, g_final):
    n, d = x1.shape
    tm = MOE_TILE
    sub = d // LANES
    row = lambda i: (i, 0)
    return pl.pallas_call(
        _combine_kernel,
        out_shape=jax.ShapeDtypeStruct((n, d), F32),
        grid=(n // tm,),
        in_specs=[pl.BlockSpec((tm, d), row), pl.BlockSpec((tm, LANES), row),
                  pl.BlockSpec(memory_space=pl.ANY), pl.BlockSpec(memory_space=pl.ANY),
                  pl.BlockSpec((1, d), lambda i: (0, 0))],
        out_specs=pl.BlockSpec((tm, d), row),
        scratch_shapes=[pltpu.VMEM((2, TOP_K * tm * sub, LANES), F32),
                        pltpu.SMEM((2, SUBLANES, tm), I32),
                        pltpu.SemaphoreType.DMA((2,)), pltpu.SemaphoreType.DMA((2,))],
        compiler_params=pltpu.CompilerParams(dimension_semantics=("arbitrary",),
                                             vmem_limit_bytes=VMEM_LIMIT),
        name="combine_norm",
    )(x1, route, dest_t, ys, g_final)


def _pad_heads(w, per_head, width=LANES):
    k = w.shape[0]
    w = w.reshape(k, HEADS, per_head)
    w = jnp.pad(w, ((0, 0), (0, 0), (0, width - per_head)))
    return w.reshape(k, HEADS * width)


def _aug_constants():
    place = np.zeros((LANES, 2 * LANES), np.float32)
    ones = np.zeros((1, 2 * LANES), np.float32)
    for hd in range(HEADS):
        for piece in range(3):
            src = FG_LO + piece * HEADS + hd
            place[src, 16 * hd + piece] = 1.0
            place[src, LANES + 16 * hd + 3 + piece] = -1.0
            ones[0, 16 * hd + 3 + piece] = 1.0
            ones[0, LANES + 16 * hd + piece] = 1.0
    return jnp.asarray(place, BF16), jnp.asarray(ones, F32)


def _layer(x2, pos2, batch, seq_len, g_attn_norm, w_in, b_fgate, g_q_a, w_q_b, g_kv_a, w_kv_b,
           w_fox_out, w_mla_out, b_merge, w_o, g_ffn_norm, w_router, b_router, w_gu, b_gu,
           w_down, b_down, g_out):
    n, d = x2.shape
    fw = HEADS * HEAD_DIM
    o = 0
    w_qf = w_in[:, o:o + fw]; o += fw
    w_kf = w_in[:, o:o + fw]; o += fw
    w_vf = w_in[:, o:o + fw]; o += fw
    w_f = w_in[:, o:o + HEADS]; o += HEADS
    w_ql = w_in[:, o:o + Q_RANK]; o += Q_RANK
    w_ckv = w_in[:, o:o + KV_RANK]; o += KV_RANK
    w_kpe = w_in[:, o:o + ROPE_DIM]; o += ROPE_DIM
    w_gate = w_in[:, o:]

    wq = (w_qf * (HEAD_DIM ** -0.5)).astype(BF16)
    wk = w_kf.astype(BF16)
    wmisc = jnp.concatenate([jnp.zeros((d, PE_LO), F32), w_kpe, w_f,
                             jnp.zeros((d, LANES - FG_LO - HEADS), F32)], axis=1)
    wlat = jnp.concatenate([w_ql, w_ckv, wmisc], axis=1).astype(BF16)
    bf128 = jnp.zeros((1, LANES), F32).at[0, FG_LO:FG_LO + HEADS].set(b_fgate)
    wqb = _pad_heads(w_q_b, HEAD_DIM + ROPE_DIM).astype(BF16)
    wkv = w_kv_b.reshape(KV_RANK, HEADS, 2 * HEAD_DIM)
    wkbk = wkv[:, :, :HEAD_DIM].reshape(KV_RANK, fw).astype(BF16)
    wkbv = wkv[:, :, HEAD_DIM:].reshape(KV_RANK, fw).T.astype(BF16)
    ltri = jnp.asarray(np.tril(np.ones((IN_TILE, IN_TILE), np.float32)), BF16)
    lstrict = jnp.asarray(np.tril(np.ones((MOE_TILE, MOE_TILE), np.float32), -1), BF16)
    ustrict = jnp.asarray(np.triu(np.ones((LANES, LANES), np.float32), 1), BF16)
    place, ones = _aug_constants()
    half = ROPE_DIM // 2
    inv_freq = ROPE_THETA ** (-jnp.arange(half, dtype=F32) / half)
    freq = jnp.zeros((1, LANES), F32).at[0, PE_LO:PE_MID].set(inv_freq).at[0, PE_MID:PE_HI].set(inv_freq)

    qf, kf, vf, qm, km, vm, gates = _inproj(
        x2, pos2, g_attn_norm.reshape(1, d), wq, wk, w_vf.T.astype(BF16), wlat, w_gate.astype(BF16),
        bf128, g_q_a.reshape(1, -1), wqb, g_kv_a.reshape(1, -1), wkbk, wkbv, ltri, place, ones,
        freq, seq_len=seq_len)

    y_fox = _attention(qf, kf, vf, batch=batch, seq_len=seq_len, chunk_mask=False)
    y_mla = _attention(qm, km, vm, batch=batch, seq_len=seq_len, chunk_mask=True)

    wr = jnp.pad(w_router, ((0, 0), (0, LANES - N_EXPERTS)))
    wrh = wr.astype(BF16)
    wrl = (wr - wrh.astype(F32)).astype(BF16)
    br = jnp.full((1, LANES), NEG, F32).at[0, :N_EXPERTS].set(b_router)
    n_blocks = n * TOP_K // EXPERT_ROWS + N_EXPERTS
    x1, route, dest_t, meta, xs = _outproj(
        y_fox, y_mla, gates, x2, b_merge.reshape(1, -1), w_fox_out.astype(BF16),
        w_mla_out.astype(BF16), w_o.astype(BF16), g_ffn_norm.reshape(1, d), wrh, wrl, br, lstrict,
        ustrict, n_blocks=n_blocks)

    block_e = meta[:SUBLANES].reshape(-1)[:n_blocks]
    order = jnp.argsort(block_e, stable=True).astype(I32)
    nused = meta[SUBLANES, 0:1]
    be_sorted = jnp.minimum(block_e[order], N_EXPERTS - 1).astype(I32)
    be_sorted = jnp.where(jnp.arange(n_blocks) < nused[0], be_sorted,
                          be_sorted[jnp.maximum(nused[0] - 1, 0)])
    ys = _experts(order, be_sorted, nused, xs, w_gu, b_gu, w_down, b_down)
    return _combine(x1, route, dest_t, ys, g_out.reshape(1, d))


def kernel(x, positions, g_attn_norm, w_in, b_fgate, g_q_a, w_q_b, g_kv_a, w_kv_b, w_fox_out, w_mla_out, b_merge, w_o, g_ffn_norm, w_router, b_router, w_gu, b_gu, w_down, b_down, g_final):
    batch, seq_len, d = x.shape
    depth = w_in.shape[0]
    assert depth == 1, "the fused combine + final-norm kernel assumes a single layer"
    assert seq_len % ATT_TILE == 0 and d % LANES == 0
    assert (batch * seq_len * TOP_K) % EXPERT_ROWS == 0
    assert batch * seq_len * TOP_K // EXPERT_ROWS + N_EXPERTS <= SUBLANES * LANES
    x2 = x.reshape(batch * seq_len, d)
    pos2 = positions.reshape(batch * seq_len, 1).astype(I32)
    out = _layer(x2, pos2, batch, seq_len, g_attn_norm[0], w_in[0], b_fgate[0], g_q_a[0], w_q_b[0],
                 g_kv_a[0], w_kv_b[0], w_fox_out[0], w_mla_out[0], b_merge[0], w_o[0],
                 g_ffn_norm[0], w_router[0], b_router[0], w_gu[0], b_gu[0], w_down[0], b_down[0],
                 g_final)
    return out.reshape(batch, seq_len, d)
```

```python
import functools

import jax
import jax.numpy as jnp
import numpy as np
from jax import lax
from jax.experimental import pallas as pl
from jax.experimental.pallas import tpu as pltpu

F32 = jnp.float32
BF16 = jnp.bfloat16
I32 = jnp.int32

LANES = 128
SUBLANES = 8
VMEM_LIMIT = 56 * 1024 * 1024

NORM_EPS = 1e-6
HEADS = 8
HEAD_DIM = 64
ROPE_DIM = 32
Q_RANK = 256
KV_RANK = 128
N_EXPERTS = 32
TOP_K = 4
EXPERT_FF = 1024
SWIGLU_LIMIT = 7.0
SWIGLU_ALPHA = 1.702
ROPE_THETA = 10000.0
CHUNK = 64

NEG = -1e30
LOG2E = 1.4426950408889634

ROW_TILE = 256
IN_TILE = 512
MOE_TILE = 512
ATT_TILE = 512
ATT_HEADS = 2
ATT_GROUP = 8
EXPERT_ROWS = MOE_TILE

PE_LO, PE_MID, PE_HI = 64, 80, 96
FG_LO = 96
AUG_LO = 64
ROUTE_DEST = 0
ROUTE_W = 8


def _dot(a, b):
    return jnp.dot(a, b, preferred_element_type=F32)


def _dot_nt(a, b):
    return lax.dot_general(a, b, (((1,), (1,)), ((), ())), preferred_element_type=F32)


def _split3(a):
    hi = a.astype(BF16)
    r1 = a - hi.astype(F32)
    mid = r1.astype(BF16)
    lo = (r1 - mid.astype(F32)).astype(BF16)
    return hi, mid, lo


def _rms(x, g):
    return x * lax.rsqrt(jnp.mean(x * x, axis=-1, keepdims=True) + NORM_EPS) * g


def _inproj_kernel(x_ref, pos_ref, g_ref, wq_ref, wk_ref, wv_ref, wlat_ref, wg_ref, bf_ref,
                   gq_ref, wqb_ref, gkv_ref, wkbk_ref, wkbv_ref, ltri_ref, place_ref, ones_ref,
                   freq_ref,
                   qf_ref, kf_ref, vf_ref, qm_ref, km_ref, vm_ref, gate_ref,
                   carry_ref, *, tiles_per_seq, mla_scale):
    i = pl.program_id(0)

    @pl.when(i % tiles_per_seq == 0)
    def _():
        carry_ref[...] = jnp.zeros_like(carry_ref)

    h = _rms(x_ref[...], g_ref[...]).astype(BF16)
    tm = h.shape[0]
    lane = lax.broadcasted_iota(I32, (tm, LANES), 1)

    lat = _dot(h, wlat_ref[...])
    q_lat = lat[:, :Q_RANK]
    c_kv = lat[:, Q_RANK:Q_RANK + KV_RANK]
    misc = lat[:, Q_RANK + KV_RANK:]

    qn = _rms(q_lat, gq_ref[...]).astype(BF16)
    kvn = _rms(c_kv, gkv_ref[...]).astype(BF16)
    qfull = _dot(qn, wqb_ref[...])
    knope = _dot(kvn, wkbk_ref[...])
    vmt = _dot_nt(wkbv_ref[...], kvn).astype(BF16)
    for c in range(tm // ROW_TILE):
        vm_ref[c] = vmt[:, c * ROW_TILE:(c + 1) * ROW_TILE]

    z = misc + bf_ref[...]
    logf = jnp.minimum(z, 0.0) - jnp.log1p(jnp.exp(-jnp.abs(z)))
    fmask = (lane >= FG_LO) & (lane < FG_LO + HEADS)
    logf = jnp.where(fmask, logf, 0.0)
    l_hi, l_mid, l_lo = _split3(logf)
    ltri = ltri_ref[...]
    c = _dot(ltri, l_hi) + _dot(ltri, l_mid) + _dot(ltri, l_lo) + carry_ref[...]
    carry_ref[...] = c[tm - 1:tm, :]
    c_hi, c_mid, c_lo = _split3(c * LOG2E)
    c3 = (c_hi.astype(F32) + pltpu.roll(c_mid.astype(F32), HEADS, 1)
          + pltpu.roll(c_lo.astype(F32), 2 * HEADS, 1)).astype(BF16)
    augc = _dot(c3, place_ref[...]) + ones_ref[...]
    low = lane < HEAD_DIM
    amask = (lane >= AUG_LO) & (lane < AUG_LO + 6)

    def aug_block(half, hd):
        src = augc[:, half * LANES:(half + 1) * LANES]
        return jnp.where(amask, pltpu.roll(src, (AUG_LO - 16 * hd) % LANES, 1), 0.0)

    def head_block(compact, hd):
        pair = compact[:, (hd // 2) * LANES:(hd // 2 + 1) * LANES]
        if hd % 2:
            pair = pltpu.roll(pair, HEAD_DIM, 1)
        return jnp.where(low, pair, 0.0)

    qc = _dot(h, wq_ref[...]) * LOG2E
    kc = _dot(h, wk_ref[...])
    vft = _dot_nt(wv_ref[...], h).astype(BF16)
    for c in range(tm // ROW_TILE):
        vf_ref[c] = vft[:, c * ROW_TILE:(c + 1) * ROW_TILE]
    gate_ref[...] = _dot(h, wg_ref[...]).astype(BF16)

    for hd in range(HEADS):
        sl = slice(hd * LANES, (hd + 1) * LANES)
        qf_ref[:, sl] = (head_block(qc, hd) + aug_block(0, hd)).astype(BF16)
        kf_ref[:, sl] = (head_block(kc, hd) + aug_block(1, hd)).astype(BF16)

    ang = pos_ref[...].astype(F32) * freq_ref[...]
    cosv = jnp.cos(ang)
    sinv = jnp.sin(ang)
    s1 = jnp.where((lane >= PE_LO) & (lane < PE_MID), -sinv, 0.0)
    s2 = jnp.where((lane >= PE_MID) & (lane < PE_HI), sinv, 0.0)

    def rope(v):
        return v * cosv + pltpu.roll(v, LANES - 16, 1) * s1 + pltpu.roll(v, 16, 1) * s2

    kpe = jnp.where((lane >= PE_LO) & (lane < PE_HI), rope(misc), 0.0)
    for hd in range(HEADS):
        sl = slice(hd * LANES, (hd + 1) * LANES)
        qm_ref[:, sl] = (rope(qfull[:, sl]) * mla_scale).astype(BF16)
        km_ref[:, sl] = (head_block(knope, hd) + kpe).astype(BF16)


def _inproj(x2, pos2, g_attn, wq, wk, wv, wlat, wg, bf128, gq, wqb, gkv, wkbk, wkbv, ltri,
            place, ones, freq, *, seq_len):
    n, d = x2.shape
    tm = IN_TILE
    hw = HEADS * LANES
    vw = HEADS * HEAD_DIM
    const = lambda i: (0, 0)
    row = lambda i: (i, 0)
    full = lambda a: pl.BlockSpec(a.shape, const)
    rows_out = lambda w: (jax.ShapeDtypeStruct((n, w), BF16), pl.BlockSpec((tm, w), row))
    vt_out = (jax.ShapeDtypeStruct((n // ROW_TILE, vw, ROW_TILE), BF16),
              pl.BlockSpec((tm // ROW_TILE, vw, ROW_TILE), lambda i: (i, 0, 0)))
    outs = [rows_out(hw), rows_out(hw), vt_out, rows_out(hw), rows_out(hw), vt_out,
            rows_out(wg.shape[1])]
    consts = (g_attn, wq, wk, wv, wlat, wg, bf128, gq, wqb, gkv, wkbk, wkbv, ltri, place, ones, freq)
    return pl.pallas_call(
        functools.partial(_inproj_kernel, tiles_per_seq=seq_len // tm,
                          mla_scale=float((HEAD_DIM + ROPE_DIM) ** -0.5) * LOG2E),
        out_shape=[o[0] for o in outs],
        grid=(n // tm,),
        in_specs=[pl.BlockSpec((tm, d), row), pl.BlockSpec((tm, 1), row)] + [full(a) for a in consts],
        out_specs=[o[1] for o in outs],
        scratch_shapes=[pltpu.VMEM((1, LANES), F32)],
        compiler_params=pltpu.CompilerParams(dimension_semantics=("arbitrary",),
                                             vmem_limit_bytes=VMEM_LIMIT),
        name="inproj",
    )(x2, pos2, *consts)


def _attn_kernel(q_ref, k_ref, vt_ref, o_ref, *, chunk_mask):
    i = pl.program_id(2)
    tq = q_ref.shape[0]
    tk = vt_ref.shape[2]

    def scores(j, masked, lo=0):
        start = pl.multiple_of(j * tk, tk)
        ss = [_dot_nt(k_ref[pl.ds(start, tk), hd * LANES:(hd + 1) * LANES],
                      q_ref[lo:, hd * LANES:(hd + 1) * LANES])
              for hd in range(ATT_HEADS)]
        if masked:
            keyg = lax.broadcasted_iota(I32, (tk, tq - lo), 0) + j * tk
            qlo = lax.broadcasted_iota(I32, (tk, tq - lo), 1) + (i * tq + lo)
            allowed = (keyg // CHUNK) <= (qlo // CHUNK) if chunk_mask else keyg <= qlo
            ss = [jnp.where(allowed, s, NEG) for s in ss]
        return ss

    ones = jnp.ones((2 * SUBLANES, tk), BF16)

    def update(j, ss, state, lo=0):
        vt = vt_ref[j]
        new = []
        for hd, s in enumerate(ss):
            m_all, acc_all = state[hd]
            m, acc = m_all[:, lo:], acc_all[:, lo:]
            m_new = jnp.maximum(m, jnp.max(s, axis=0, keepdims=True))
            alpha = jnp.exp2(m - m_new)
            p = jnp.exp2((s - m_new[0:1]).astype(BF16))
            va = jnp.concatenate([vt[hd * HEAD_DIM:(hd + 1) * HEAD_DIM, :], ones], axis=0)
            acc = alpha[0:1] * acc + _dot(va, p)
            if lo:
                m_new = jnp.concatenate([m_all[:, :lo], m_new], axis=1)
                acc = jnp.concatenate([acc_all[:, :lo], acc], axis=1)
            new.append((m_new, acc))
        return tuple(new)

    n_full = (i * tq) // tk
    group = max(1, tq // tk)

    def run_group(first, state, n_plain, n_masked=0):
        los = [0] * n_plain + [u * tk for u in range(n_masked)]
        msk = [False] * n_plain + [True] * n_masked
        nblk = n_plain + n_masked
        ss = scores(first, msk[0], los[0])
        for u in range(nblk):
            nxt = scores(first + u + 1, msk[u + 1], los[u + 1]) if u + 1 < nblk else None
            state = update(first + u, ss, state, los[u])
            ss = nxt
        return state

    long = ATT_GROUP * group
    n_long = n_full // long
    init1 = (jnp.full((SUBLANES, tq), NEG, F32), jnp.zeros((HEAD_DIM + 2 * SUBLANES, tq), F32))
    state = lax.fori_loop(0, n_long, lambda jj, c: run_group(jj * long, c, long),
                          (init1,) * ATT_HEADS)
    done = n_long * long
    left = (n_full - done) // group
    tails = [functools.partial(run_group, done, n_plain=r * group, n_masked=group)
             for r in range(ATT_GROUP)]

    def pick(lo, hi, st):
        if hi - lo == 1:
            return tails[lo](st)
        mid = (lo + hi) // 2
        return lax.cond(left < mid, lambda s: pick(lo, mid, s), lambda s: pick(mid, hi, s), st)

    state = pick(0, ATT_GROUP, state)
    out_t = jnp.concatenate([acc[:HEAD_DIM] / acc[HEAD_DIM:HEAD_DIM + 1] for _, acc in state],
                            axis=0)
    o_ref[...] = out_t.T.astype(o_ref.dtype)


def _attention(q, k, vt, *, batch, seq_len, chunk_mask):
    n = q.shape[0]
    t = ATT_TILE
    tk = vt.shape[2]
    nq = seq_len // t
    nkb = seq_len // tk
    hp = ATT_HEADS
    pairs = HEADS // hp
    return pl.pallas_call(
        functools.partial(_attn_kernel, chunk_mask=chunk_mask),
        out_shape=jax.ShapeDtypeStruct((n, HEADS * HEAD_DIM), BF16),
        grid=(batch, pairs, nq),
        in_specs=[pl.BlockSpec((t, hp * LANES), lambda b, p, i: (b * nq + i, p)),
                  pl.BlockSpec((seq_len, hp * LANES), lambda b, p, i: (b, p)),
                  pl.BlockSpec((nkb, hp * HEAD_DIM, tk), lambda b, p, i: (b, p, 0))],
        out_specs=pl.BlockSpec((t, hp * HEAD_DIM), lambda b, p, i: (b * nq + i, p)),
        compiler_params=pltpu.CompilerParams(
            dimension_semantics=("arbitrary", "arbitrary", "arbitrary"),
            vmem_limit_bytes=VMEM_LIMIT),
        name="attn_mla" if chunk_mask else "attn_fox",
    )(q, k, vt)


def _outproj_kernel(yf_ref, ym_ref, gate_ref, x_ref, bm_ref, wfo_ref, wmo_ref, wo_ref, gffn_ref,
                    wrh_ref, wrl_ref, br_ref, lstrict_ref, ustrict_ref,
                    x1_ref, route_ref, dest_ref, meta_ref, xs_hbm,
                    fill_ref, cur_ref, nfree_ref, tbl_ref,
                    hbuf, dbuf, dsm, zbuf, mbuf, msm, ssem, isem, zsem, msem):
    i = pl.program_id(0)
    nt = pl.num_programs(0)
    d = x_ref.shape[1]
    tm = x_ref.shape[0]
    sub = d // LANES
    rows = EXPERT_ROWS
    n_blocks = xs_hbm.shape[0] // (rows * sub)
    slot = i % 2

    def idx_copy(s):
        return pltpu.make_async_copy(dbuf.at[s], dsm.at[s], isem.at[s])

    def start_scatters(s):
        for static_s in range(2):
            @pl.when(s == static_s)
            def _(static_s=static_s):
                def body(t, _):
                    src = hbuf.at[static_s, pl.ds(pl.multiple_of(t * sub, sub), sub), :]
                    for k in range(TOP_K):
                        dst = dsm[static_s, k, t]
                        pltpu.make_async_copy(
                            src, xs_hbm.at[pl.ds(pl.multiple_of(dst, sub), sub), :],
                            ssem.at[static_s]).start(priority=k % 2)
                    return 0

                lax.fori_loop(0, tm, body, 0, unroll=4)

    def wait_scatters(s):
        for _ in range(TOP_K):
            pltpu.make_async_copy(hbuf.at[s], xs_hbm.at[pl.ds(0, tm * sub), :], ssem.at[s]).wait()

    @pl.when(i == 0)
    def _():
        fill_ref[...] = jnp.full_like(fill_ref, float(rows))
        cur_ref[...] = jnp.zeros_like(cur_ref)
        nfree_ref[...] = jnp.zeros_like(nfree_ref)
        tbl_ref[...] = jnp.full_like(tbl_ref, float(N_EXPERTS))
        zbuf[...] = jnp.zeros_like(zbuf)

    def tile_step(scatter_prev):
        if scatter_prev:
            inline_scatters(1 - slot)
        a = _dot(yf_ref[...], wfo_ref[...])
        b = _dot(ym_ref[...], wmo_ref[...])
        g = 1.0 / (1.0 + jnp.exp(-(gate_ref[...].astype(F32) + bm_ref[...])))
        merged = (g[:, :d] * a + g[:, d:] * b).astype(BF16)
        x1 = x_ref[...] + _dot(merged, wo_ref[...])
        x1_ref[...] = x1
        h2 = _rms(x1, gffn_ref[...])

        hi = h2.astype(BF16)
        lo = (h2 - hi.astype(F32)).astype(BF16)
        wrh = wrh_ref[...]
        logits = _dot(hi, wrh) + _dot(lo, wrh) + _dot(hi, wrl_ref[...]) + br_ref[...]

        lane = lax.broadcasted_iota(I32, (tm, LANES), 1)
        vals = logits
        sels, tops = [], []
        for _ in range(TOP_K):
            mx = jnp.max(vals, axis=-1, keepdims=True)
            idx = jnp.min(jnp.where(vals == mx, lane, LANES), axis=-1, keepdims=True)
            sel = lane == idx
            vals = jnp.where(sel, NEG, vals)
            sels.append(sel)
            tops.append(mx)
        es = [jnp.exp(tv - tops[0]) for tv in tops]
        den = es[0] + es[1] + es[2] + es[3]

        onehot = jnp.zeros((tm, LANES), F32)
        for sel in sels:
            onehot = onehot + sel.astype(F32)
        before = _dot(lstrict_ref[...], onehot.astype(BF16))
        cnt = jnp.sum(onehot, axis=0, keepdims=True)

        fill = fill_ref[...]
        cur = cur_ref[...]
        nfree = nfree_ref[...]
        need = ((fill + cnt) > float(rows)).astype(F32)
        need8 = jnp.broadcast_to(need, (SUBLANES, LANES)).astype(BF16)
        newid = nfree + _dot(need8, ustrict_ref[...])[0:1, :]
        pos = fill + before
        dest = jnp.where(pos < float(rows), cur * rows + pos, newid * rows + pos - float(rows))
        fill_ref[...] = fill + cnt - need * float(rows)
        cur_ref[...] = jnp.where(need > 0, newid, cur)
        nfree_ref[...] = nfree + jnp.sum(need, axis=-1, keepdims=True)
        blk_id = (lax.broadcasted_iota(I32, (SUBLANES, LANES), 0) * LANES
                  + lax.broadcasted_iota(I32, (SUBLANES, LANES), 1)).astype(F32)
        tbl = tbl_ref[...]
        for e in range(N_EXPERTS):
            hit = (blk_id == newid[:, e:e + 1]) & (need[:, e:e + 1] > 0)
            tbl = jnp.where(hit, float(e), tbl)
        tbl_ref[...] = tbl

        route = jnp.zeros((tm, LANES), F32)
        for k in range(TOP_K):
            dest_k = jnp.sum(jnp.where(sels[k], dest, 0.0), axis=-1, keepdims=True)
            route = jnp.where(lane == ROUTE_DEST + k, dest_k, route)
            route = jnp.where(lane == ROUTE_W + k, es[k] / den, route)
        route_ref[...] = route
        dest_t = (route.T[0:SUBLANES, :] * float(sub)).astype(I32)
        dest_ref[0] = dest_t

        for s in range(sub):
            hbuf[slot, pl.ds(s, tm, stride=sub), :] = h2[:, s * LANES:(s + 1) * LANES]
        dbuf[slot] = dest_t

    def flush_last():
        idx_copy(slot).wait()
        start_scatters(slot)

        @pl.when(i >= 1)
        def _():
            wait_scatters(1 - slot)

        wait_scatters(slot)

        meta = jnp.concatenate([tbl_ref[...], nfree_ref[...], fill_ref[...], cur_ref[...],
                                jnp.zeros((SUBLANES - 3, LANES), F32)], axis=0).astype(I32)
        meta_ref[...] = meta
        state = jnp.concatenate([fill_ref[...], cur_ref[...], nfree_ref[...],
                                 jnp.zeros((SUBLANES - 3, LANES), F32)], axis=0).astype(I32)
        mbuf[...] = state
        mcopy = pltpu.make_async_copy(mbuf, msm, msem.at[0])
        mcopy.start()
        mcopy.wait()

        def zero_copy(first_row, n_rows):
            return pltpu.make_async_copy(
                zbuf.at[pl.ds(0, n_rows * sub), :],
                xs_hbm.at[pl.ds(pl.multiple_of(first_row * sub, sub), n_rows * sub), :], zsem.at[0])

        chunks = [rows >> (s + 1) for s in range(rows.bit_length() - 1)]
        plans = []
        for e in range(N_EXPERTS):
            rem = rows - msm[0, e]
            at = msm[1, e] * rows + msm[0, e]
            for c in chunks:
                take = (rem & c) != 0
                plans.append((take, zero_copy(at, c)))
                at = at + jnp.where(take, c, 0)
        for j in range(N_EXPERTS):
            blk = msm[2, 0] + j
            safe = jnp.minimum(blk, n_blocks - 1)
            plans.append((blk < n_blocks, zero_copy(safe * rows, rows)))
        for take, cp in plans:
            pl.when(take)(cp.start)
        for take, cp in plans:
            pl.when(take)(cp.wait)

    @pl.when(i > 0)
    def _():
        idx_copy(1 - slot).wait()

    @pl.when(i >= 2)
    def _():
        wait_scatters(slot)

    def inline_scatters(s):
        for t in range(tm):
            src = hbuf.at[s, pl.ds(t * sub, sub), :]
            for k in range(TOP_K):
                dst = dsm[s, k, t]
                pltpu.make_async_copy(src, xs_hbm.at[pl.ds(pl.multiple_of(dst, sub), sub), :],
                                      ssem.at[s]).start(priority=k % 2)

    @pl.when(i == 0)
    def _():
        tile_step(False)

    @pl.when(i > 0)
    def _():
        tile_step(True)

    idx_copy(slot).start()

    @pl.when(i == nt - 1)
    def _():
        flush_last()


def _outproj(yf, ym, gates, x2, bm, wfo, wmo, wo, gffn, wrh, wrl, br, lstrict, ustrict, *, n_blocks):
    n, d = x2.shape
    tm = MOE_TILE
    rows = EXPERT_ROWS
    sub = d // LANES
    const = lambda i: (0, 0)
    row = lambda i: (i, 0)
    full = lambda a: pl.BlockSpec(a.shape, const)
    consts = (bm, wfo, wmo, wo, gffn, wrh, wrl, br, lstrict, ustrict)
    return pl.pallas_call(
        _outproj_kernel,
        out_shape=[jax.ShapeDtypeStruct((n, d), F32),
                   jax.ShapeDtypeStruct((n, LANES), F32),
                   jax.ShapeDtypeStruct((n // tm, SUBLANES, tm), I32),
                   jax.ShapeDtypeStruct((2 * SUBLANES, LANES), I32),
                   jax.ShapeDtypeStruct((n_blocks * rows * sub, LANES), F32)],
        grid=(n // tm,),
        in_specs=[pl.BlockSpec((tm, yf.shape[1]), row), pl.BlockSpec((tm, ym.shape[1]), row),
                  pl.BlockSpec((tm, gates.shape[1]), row), pl.BlockSpec((tm, d), row)]
                 + [full(a) for a in consts],
        out_specs=[pl.BlockSpec((tm, d), row), pl.BlockSpec((tm, LANES), row),
                   pl.BlockSpec((1, SUBLANES, tm), lambda i: (i, 0, 0)),
                   pl.BlockSpec((2 * SUBLANES, LANES), const),
                   pl.BlockSpec(memory_space=pl.ANY)],
        scratch_shapes=[pltpu.VMEM((1, LANES), F32), pltpu.VMEM((1, LANES), F32),
                        pltpu.VMEM((1, LANES), F32), pltpu.VMEM((SUBLANES, LANES), F32),
                        pltpu.VMEM((2, tm * sub, LANES), F32),
                        pltpu.VMEM((2, SUBLANES, tm), I32),
                        pltpu.SMEM((2, SUBLANES, tm), I32),
                        pltpu.VMEM((rows * sub, LANES), F32),
                        pltpu.VMEM((SUBLANES, LANES), I32),
                        pltpu.SMEM((SUBLANES, LANES), I32),
                        pltpu.SemaphoreType.DMA((2,)), pltpu.SemaphoreType.DMA((2,)),
                        pltpu.SemaphoreType.DMA((1,)), pltpu.SemaphoreType.DMA((1,))],
        compiler_params=pltpu.CompilerParams(dimension_semantics=("arbitrary",),
                                             vmem_limit_bytes=VMEM_LIMIT),
        name="outproj_router",
    )(yf, ym, gates, x2, *consts)


def _expert_kernel(order_ref, be_ref, nused_ref, half_ref,
                   xs_ref, wgu_ref, bgu_ref, wd_ref, bd_ref, ys_ref, wgu_b, wd_b):
    b = pl.program_id(0)
    rows = EXPERT_ROWS
    sub = xs_ref.shape[0] // rows

    def ffn(n):
        x = jnp.concatenate(
            [xs_ref[pl.ds(s, n, stride=sub), :].astype(BF16) for s in range(sub)], axis=1)
        gu = _dot(x, wgu_b[...]) + bgu_ref[0]
        gate = jnp.minimum(gu[:, :EXPERT_FF], SWIGLU_LIMIT)
        up = jnp.clip(gu[:, EXPERT_FF:], -SWIGLU_LIMIT, SWIGLU_LIMIT)
        glu = gate * (1.0 / (1.0 + jnp.exp(-SWIGLU_ALPHA * gate)))
        y = _dot(((up + 1.0) * glu).astype(BF16), wd_b[...]) + bd_ref[0]
        for s in range(sub):
            ys_ref[pl.ds(s, n, stride=sub), :] = y[:, s * LANES:(s + 1) * LANES]

    @pl.when(b < nused_ref[0])
    def _():
        changed = jnp.logical_or(b == 0, be_ref[b] != be_ref[jnp.maximum(b - 1, 0)])

        @pl.when(changed)
        def _():
            wgu_b[...] = wgu_ref[0].astype(BF16)
            wd_b[...] = wd_ref[0].astype(BF16)

        @pl.when(half_ref[b] == 0)
        def _():
            ffn(rows)

        @pl.when(half_ref[b] != 0)
        def _():
            ffn(rows // 2)
            ys_ref[pl.ds(rows // 2 * sub, rows // 2 * sub), :] = jnp.zeros(
                (rows // 2 * sub, LANES), ys_ref.dtype)

    @pl.when(b >= nused_ref[0])
    def _():
        ys_ref[...] = jnp.zeros_like(ys_ref)


def _experts(order, block_e, nused, half, xs, w_gu, b_gu, w_d, b_d):
    e, d, ff2 = w_gu.shape
    rows = EXPERT_ROWS
    sub = d // LANES
    n_blocks = xs.shape[0] // (rows * sub)
    wmap = lambda b, od, be, nu, hf: (be[b], 0, 0)
    xmap = lambda b, od, be, nu, hf: (od[b], 0)
    grid_spec = pltpu.PrefetchScalarGridSpec(
        num_scalar_prefetch=4,
        grid=(n_blocks,),
        in_specs=[pl.BlockSpec((rows * sub, LANES), xmap),
                  pl.BlockSpec((1, d, ff2), wmap), pl.BlockSpec((1, 1, ff2), wmap),
                  pl.BlockSpec((1, ff2 // 2, d), wmap), pl.BlockSpec((1, 1, d), wmap)],
        out_specs=pl.BlockSpec((rows * sub, LANES), xmap),
        scratch_shapes=[pltpu.VMEM((d, ff2), BF16), pltpu.VMEM((ff2 // 2, d), BF16)],
    )
    return pl.pallas_call(
        _expert_kernel,
        out_shape=jax.ShapeDtypeStruct(xs.shape, F32),
        grid_spec=grid_spec,
        compiler_params=pltpu.CompilerParams(dimension_semantics=("arbitrary",),
                                             vmem_limit_bytes=VMEM_LIMIT),
        name="experts",
    )(order, block_e, nused, half, xs, w_gu, b_gu.reshape(e, 1, ff2), w_d, b_d.reshape(e, 1, d))


def _combine_kernel(x1_ref, route_ref, dest_hbm, ys_hbm, g_ref, o_ref, ybuf, dsm, gsem, isem):
    i = pl.program_id(0)
    nt = pl.num_programs(0)
    tm, d = x1_ref.shape
    sub = d // LANES
    slot = i % 2

    def idx_copy(tile):
        s = tile % 2
        return pltpu.make_async_copy(dest_hbm.at[tile], dsm.at[s], isem.at[s])

    def start_gathers(tile):
        for static_s in range(2):
            @pl.when(tile % 2 == static_s)
            def _(static_s=static_s):
                def body(t, _):
                    for k in range(TOP_K):
                        src = dsm[static_s, k, t]
                        pltpu.make_async_copy(
                            ys_hbm.at[pl.ds(pl.multiple_of(src, sub), sub), :],
                            ybuf.at[static_s, pl.ds(pl.multiple_of((k * tm + t) * sub, sub), sub), :],
                            gsem.at[static_s]).start(priority=k % 2)
                    return 0

                lax.fori_loop(0, tm, body, 0, unroll=4)

    def wait_gathers(s):
        pltpu.make_async_copy(ys_hbm.at[pl.ds(0, TOP_K * tm * sub), :], ybuf.at[s], gsem.at[s]).wait()

    @pl.when(i == 0)
    def _():
        idx_copy(0).start()
        idx_copy(0).wait()
        start_gathers(0)

        @pl.when(nt > 1)
        def _():
            idx_copy(1).start()

    @pl.when(i + 1 < nt)
    def _():
        idx_copy(i + 1).wait()

    wait_gathers(slot)

    def inline_gathers(s):
        for t in range(tm):
            for k in range(TOP_K):
                src = dsm[s, k, t]
                pltpu.make_async_copy(
                    ys_hbm.at[pl.ds(pl.multiple_of(src, sub), sub), :],
                    ybuf.at[s, pl.ds((k * tm + t) * sub, sub), :],
                    gsem.at[s]).start(priority=k % 2)

    def tile_step(prefetch_next):
        if prefetch_next:
            inline_gathers(1 - slot)
        route = route_ref[...]
        ws = [route[:, ROUTE_W + k:ROUTE_W + k + 1] for k in range(TOP_K)]
        chunks = []
        ssq = jnp.zeros((tm, 1), F32)
        for s in range(sub):
            acc = x1_ref[:, s * LANES:(s + 1) * LANES]
            for k in range(TOP_K):
                acc = acc + ws[k] * ybuf[slot, pl.ds(k * tm * sub + s, tm, stride=sub), :]
            chunks.append(acc)
            ssq = ssq + jnp.sum(acc * acc, axis=-1, keepdims=True)
        inv = lax.rsqrt(ssq / d + NORM_EPS)
        for s in range(sub):
            sl = slice(s * LANES, (s + 1) * LANES)
            o_ref[:, sl] = chunks[s] * inv * g_ref[:, sl]

    @pl.when(i + 1 < nt)
    def _():
        tile_step(True)

    @pl.when(i + 1 >= nt)
    def _():
        tile_step(False)

    @pl.when(i + 2 < nt)
    def _():
        idx_copy(i + 2).start()


def _combine(x1, route, dest_t, ys, g_final):
    n, d = x1.shape
    tm = MOE_TILE
    sub = d // LANES
    row = lambda i: (i, 0)
    return pl.pallas_call(
        _combine_kernel,
        out_shape=jax.ShapeDtypeStruct((n, d), F32),
        grid=(n // tm,),
        in_specs=[pl.BlockSpec((tm, d), row), pl.BlockSpec((tm, LANES), row),
                  pl.BlockSpec(memory_space=pl.ANY), pl.BlockSpec(memory_space=pl.ANY),
                  pl.BlockSpec((1, d), lambda i: (0, 0))],
        out_specs=pl.BlockSpec((tm, d), row),
        scratch_shapes=[pltpu.VMEM((2, TOP_K * tm * sub, LANES), F32),
                        pltpu.SMEM((2, SUBLANES, tm), I32),
                        pltpu.SemaphoreType.DMA((2,)), pltpu.SemaphoreType.DMA((2,))],
        compiler_params=pltpu.CompilerParams(dimension_semantics=("arbitrary",),
                                             vmem_limit_bytes=VMEM_LIMIT),
        name="combine_norm",
    )(x1, route, dest_t, ys, g_final)


def _pad_heads(w, per_head, width=LANES):
    k = w.shape[0]
    w = w.reshape(k, HEADS, per_head)
    w = jnp.pad(w, ((0, 0), (0, 0), (0, width - per_head)))
    return w.reshape(k, HEADS * width)


def _aug_constants():
    place = np.zeros((LANES, 2 * LANES), np.float32)
    ones = np.zeros((1, 2 * LANES), np.float32)
    for hd in range(HEADS):
        for piece in range(3):
            src = FG_LO + piece * HEADS + hd
            place[src, 16 * hd + piece] = 1.0
            place[src, LANES + 16 * hd + 3 + piece] = -1.0
            ones[0, 16 * hd + 3 + piece] = 1.0
            ones[0, LANES + 16 * hd + piece] = 1.0
    return jnp.asarray(place, BF16), jnp.asarray(ones, F32)


def _layer(x2, pos2, batch, seq_len, g_attn_norm, w_in, b_fgate, g_q_a, w_q_b, g_kv_a, w_kv_b,
           w_fox_out, w_mla_out, b_merge, w_o, g_ffn_norm, w_router, b_router, w_gu, b_gu,
           w_down, b_down, g_out):
    n, d = x2.shape
    fw = HEADS * HEAD_DIM
    o = 0
    w_qf = w_in[:, o:o + fw]; o += fw
    w_kf = w_in[:, o:o + fw]; o += fw
    w_vf = w_in[:, o:o + fw]; o += fw
    w_f = w_in[:, o:o + HEADS]; o += HEADS
    w_ql = w_in[:, o:o + Q_RANK]; o += Q_RANK
    w_ckv = w_in[:, o:o + KV_RANK]; o += KV_RANK
    w_kpe = w_in[:, o:o + ROPE_DIM]; o += ROPE_DIM
    w_gate = w_in[:, o:]

    wq = (w_qf * (HEAD_DIM ** -0.5)).astype(BF16)
    wk = w_kf.astype(BF16)
    wmisc = jnp.concatenate([jnp.zeros((d, PE_LO), F32), w_kpe, w_f,
                             jnp.zeros((d, LANES - FG_LO - HEADS), F32)], axis=1)
    wlat = jnp.concatenate([w_ql, w_ckv, wmisc], axis=1).astype(BF16)
    bf128 = jnp.zeros((1, LANES), F32).at[0, FG_LO:FG_LO + HEADS].set(b_fgate)
    wqb = _pad_heads(w_q_b, HEAD_DIM + ROPE_DIM).astype(BF16)
    wkv = w_kv_b.reshape(KV_RANK, HEADS, 2 * HEAD_DIM)
    wkbk = wkv[:, :, :HEAD_DIM].reshape(KV_RANK, fw).astype(BF16)
    wkbv = wkv[:, :, HEAD_DIM:].reshape(KV_RANK, fw).T.astype(BF16)
    ltri = jnp.asarray(np.tril(np.ones((IN_TILE, IN_TILE), np.float32)), BF16)
    lstrict = jnp.asarray(np.tril(np.ones((MOE_TILE, MOE_TILE), np.float32), -1), BF16)
    ustrict = jnp.asarray(np.triu(np.ones((LANES, LANES), np.float32), 1), BF16)
    place, ones = _aug_constants()
    half = ROPE_DIM // 2
    inv_freq = ROPE_THETA ** (-jnp.arange(half, dtype=F32) / half)
    freq = jnp.zeros((1, LANES), F32).at[0, PE_LO:PE_MID].set(inv_freq).at[0, PE_MID:PE_HI].set(inv_freq)

    qf, kf, vf, qm, km, vm, gates = _inproj(
        x2, pos2, g_attn_norm.reshape(1, d), wq, wk, w_vf.T.astype(BF16), wlat, w_gate.astype(BF16),
        bf128, g_q_a.reshape(1, -1), wqb, g_kv_a.reshape(1, -1), wkbk, wkbv, ltri, place, ones,
        freq, seq_len=seq_len)

    y_fox = _attention(qf, kf, vf, batch=batch, seq_len=seq_len, chunk_mask=False)
    y_mla = _attention(qm, km, vm, batch=batch, seq_len=seq_len, chunk_mask=True)

    wr = jnp.pad(w_router, ((0, 0), (0, LANES - N_EXPERTS)))
    wrh = wr.astype(BF16)
    wrl = (wr - wrh.astype(F32)).astype(BF16)
    br = jnp.full((1, LANES), NEG, F32).at[0, :N_EXPERTS].set(b_router)
    n_blocks = n * TOP_K // EXPERT_ROWS + N_EXPERTS
    x1, route, dest_t, meta, xs = _outproj(
        y_fox, y_mla, gates, x2, b_merge.reshape(1, -1), w_fox_out.astype(BF16),
        w_mla_out.astype(BF16), w_o.astype(BF16), g_ffn_norm.reshape(1, d), wrh, wrl, br, lstrict,
        ustrict, n_blocks=n_blocks)

    block_e = meta[:SUBLANES].reshape(-1)[:n_blocks]
    order = jnp.argsort(block_e, stable=True).astype(I32)
    nused = meta[SUBLANES, 0:1]
    be_sorted = jnp.minimum(block_e[order], N_EXPERTS - 1).astype(I32)
    be_sorted = jnp.where(jnp.arange(n_blocks) < nused[0], be_sorted,
                          be_sorted[jnp.maximum(nused[0] - 1, 0)])
    fill_e = meta[SUBLANES + 1, :N_EXPERTS]
    cur_e = meta[SUBLANES + 2, :N_EXPERTS]
    be_clip = jnp.minimum(block_e, N_EXPERTS - 1)
    used_rows = jnp.where(cur_e[be_clip] == jnp.arange(n_blocks), fill_e[be_clip], EXPERT_ROWS)
    half = (used_rows[order] <= EXPERT_ROWS // 2).astype(I32)
    ys = _experts(order, be_sorted, nused, half, xs, w_gu, b_gu, w_down, b_down)
    return _combine(x1, route, dest_t, ys, g_out.reshape(1, d))


def kernel(x, positions, g_attn_norm, w_in, b_fgate, g_q_a, w_q_b, g_kv_a, w_kv_b, w_fox_out, w_mla_out, b_merge, w_o, g_ffn_norm, w_router, b_router, w_gu, b_gu, w_down, b_down, g_final):
    batch, seq_len, d = x.shape
    depth = w_in.shape[0]
    assert depth == 1, "the fused combine + final-norm kernel assumes a single layer"
    assert seq_len % ATT_TILE == 0 and d % LANES == 0
    assert (batch * seq_len * TOP_K) % EXPERT_ROWS == 0
    assert batch * seq_len * TOP_K // EXPERT_ROWS + N_EXPERTS <= SUBLANES * LANES
    x2 = x.reshape(batch * seq_len, d)
    pos2 = positions.reshape(batch * seq_len, 1).astype(I32)
    out = _layer(x2, pos2, batch, seq_len, g_attn_norm[0], w_in[0], b_fgate[0], g_q_a[0], w_q_b[0],
                 g_kv_a[0], w_kv_b[0], w_fox_out[0], w_mla_out[0], b_merge[0], w_o[0],
                 g_ffn_norm[0], w_router[0], b_router[0], w_gu[0], b_gu[0], w_down[0], b_down[0],
                 g_final)
    return out.reshape(batch, seq_len, d)
```

```python
import functools

import jax
import jax.numpy as jnp
import numpy as np
from jax import lax
from jax.experimental import pallas as pl
from jax.experimental.pallas import tpu as pltpu

F32 = jnp.float32
BF16 = jnp.bfloat16
I32 = jnp.int32

LANES = 128
SUBLANES = 8
VMEM_LIMIT = 56 * 1024 * 1024

NORM_EPS = 1e-6
HEADS = 8
HEAD_DIM = 64
ROPE_DIM = 32
Q_RANK = 256
KV_RANK = 128
N_EXPERTS = 32
TOP_K = 4
EXPERT_FF = 1024
SWIGLU_LIMIT = 7.0
SWIGLU_ALPHA = 1.702
ROPE_THETA = 10000.0
CHUNK = 64

NEG = -1e30
LOG2E = 1.4426950408889634

ROW_TILE = 256
IN_TILE = 512
MOE_TILE = 512
ATT_TILE = 512
ATT_HEADS = 2
ATT_GROUP = 8
EXPERT_ROWS = MOE_TILE

PE_LO, PE_MID, PE_HI = 64, 80, 96
FG_LO = 96
AUG_LO = 64
ROUTE_DEST = 0
ROUTE_W = 8


def _dot(a, b):
    return jnp.dot(a, b, preferred_element_type=F32)


def _dot_nt(a, b):
    return lax.dot_general(a, b, (((1,), (1,)), ((), ())), preferred_element_type=F32)


def _split3(a):
    hi = a.astype(BF16)
    r1 = a - hi.astype(F32)
    mid = r1.astype(BF16)
    lo = (r1 - mid.astype(F32)).astype(BF16)
    return hi, mid, lo


def _rms(x, g):
    return x * lax.rsqrt(jnp.mean(x * x, axis=-1, keepdims=True) + NORM_EPS) * g


def _inproj_kernel(x_ref, pos_ref, g_ref, wq_ref, wk_ref, wv_ref, wlat_ref, wg_ref, bf_ref,
                   gq_ref, wqb_ref, gkv_ref, wkbk_ref, wkbv_ref, ltri_ref, place_ref, ones_ref,
                   freq_ref,
                   qf_ref, kf_ref, vf_ref, qm_ref, km_ref, vm_ref, gate_ref,
                   carry_ref, *, tiles_per_seq, mla_scale):
    i = pl.program_id(0)

    @pl.when(i % tiles_per_seq == 0)
    def _():
        carry_ref[...] = jnp.zeros_like(carry_ref)

    h = _rms(x_ref[...], g_ref[...]).astype(BF16)
    tm = h.shape[0]
    lane = lax.broadcasted_iota(I32, (tm, LANES), 1)

    lat = _dot(h, wlat_ref[...])
    q_lat = lat[:, :Q_RANK]
    c_kv = lat[:, Q_RANK:Q_RANK + KV_RANK]
    misc = lat[:, Q_RANK + KV_RANK:]

    qn = _rms(q_lat, gq_ref[...]).astype(BF16)
    kvn = _rms(c_kv, gkv_ref[...]).astype(BF16)
    qfull = _dot(qn, wqb_ref[...])
    knope = _dot(kvn, wkbk_ref[...])
    vmt = _dot_nt(wkbv_ref[...], kvn).astype(BF16)
    for c in range(tm // ROW_TILE):
        vm_ref[c] = vmt[:, c * ROW_TILE:(c + 1) * ROW_TILE]

    z = misc + bf_ref[...]
    logf = jnp.minimum(z, 0.0) - jnp.log1p(jnp.exp(-jnp.abs(z)))
    fmask = (lane >= FG_LO) & (lane < FG_LO + HEADS)
    logf = jnp.where(fmask, logf, 0.0)
    l_hi, l_mid, l_lo = _split3(logf)
    ltri = ltri_ref[...]
    c = _dot(ltri, l_hi) + _dot(ltri, l_mid) + _dot(ltri, l_lo) + carry_ref[...]
    carry_ref[...] = c[tm - 1:tm, :]
    c_hi, c_mid, c_lo = _split3(c * LOG2E)
    c3 = (c_hi.astype(F32) + pltpu.roll(c_mid.astype(F32), HEADS, 1)
          + pltpu.roll(c_lo.astype(F32), 2 * HEADS, 1)).astype(BF16)
    augc = _dot(c3, place_ref[...]) + ones_ref[...]
    low = lane < HEAD_DIM
    amask = (lane >= AUG_LO) & (lane < AUG_LO + 6)

    def aug_block(half, hd):
        src = augc[:, half * LANES:(half + 1) * LANES]
        return jnp.where(amask, pltpu.roll(src, (AUG_LO - 16 * hd) % LANES, 1), 0.0)

    def head_block(compact, hd):
        pair = compact[:, (hd // 2) * LANES:(hd // 2 + 1) * LANES]
        if hd % 2:
            pair = pltpu.roll(pair, HEAD_DIM, 1)
        return jnp.where(low, pair, 0.0)

    qc = _dot(h, wq_ref[...]) * LOG2E
    kc = _dot(h, wk_ref[...])
    vft = _dot_nt(wv_ref[...], h).astype(BF16)
    for c in range(tm // ROW_TILE):
        vf_ref[c] = vft[:, c * ROW_TILE:(c + 1) * ROW_TILE]
    gate_ref[...] = _dot(h, wg_ref[...]).astype(BF16)

    for hd in range(HEADS):
        sl = slice(hd * LANES, (hd + 1) * LANES)
        qf_ref[:, sl] = (head_block(qc, hd) + aug_block(0, hd)).astype(BF16)
        kf_ref[:, sl] = (head_block(kc, hd) + aug_block(1, hd)).astype(BF16)

    ang = pos_ref[...].astype(F32) * freq_ref[...]
    cosv = jnp.cos(ang)
    sinv = jnp.sin(ang)
    s1 = jnp.where((lane >= PE_LO) & (lane < PE_MID), -sinv, 0.0)
    s2 = jnp.where((lane >= PE_MID) & (lane < PE_HI), sinv, 0.0)

    def rope(v):
        return v * cosv + pltpu.roll(v, LANES - 16, 1) * s1 + pltpu.roll(v, 16, 1) * s2

    kpe = jnp.where((lane >= PE_LO) & (lane < PE_HI), rope(misc), 0.0)
    for hd in range(HEADS):
        sl = slice(hd * LANES, (hd + 1) * LANES)
        qm_ref[:, sl] = (rope(qfull[:, sl]) * mla_scale).astype(BF16)
        km_ref[:, sl] = (head_block(knope, hd) + kpe).astype(BF16)


def _inproj(x2, pos2, g_attn, wq, wk, wv, wlat, wg, bf128, gq, wqb, gkv, wkbk, wkbv, ltri,
            place, ones, freq, *, seq_len):
    n, d = x2.shape
    tm = IN_TILE
    hw = HEADS * LANES
    vw = HEADS * HEAD_DIM
    const = lambda i: (0, 0)
    row = lambda i: (i, 0)
    full = lambda a: pl.BlockSpec(a.shape, const)
    rows_out = lambda w: (jax.ShapeDtypeStruct((n, w), BF16), pl.BlockSpec((tm, w), row))
    vt_out = (jax.ShapeDtypeStruct((n // ROW_TILE, vw, ROW_TILE), BF16),
              pl.BlockSpec((tm // ROW_TILE, vw, ROW_TILE), lambda i: (i, 0, 0)))
    outs = [rows_out(hw), rows_out(hw), vt_out, rows_out(hw), rows_out(hw), vt_out,
            rows_out(wg.shape[1])]
    consts = (g_attn, wq, wk, wv, wlat, wg, bf128, gq, wqb, gkv, wkbk, wkbv, ltri, place, ones, freq)
    return pl.pallas_call(
        functools.partial(_inproj_kernel, tiles_per_seq=seq_len // tm,
                          mla_scale=float((HEAD_DIM + ROPE_DIM) ** -0.5) * LOG2E),
        out_shape=[o[0] for o in outs],
        grid=(n // tm,),
        in_specs=[pl.BlockSpec((tm, d), row), pl.BlockSpec((tm, 1), row)] + [full(a) for a in consts],
        out_specs=[o[1] for o in outs],
        scratch_shapes=[pltpu.VMEM((1, LANES), F32)],
        compiler_params=pltpu.CompilerParams(
            dimension_semantics=("arbitrary",), vmem_limit_bytes=VMEM_LIMIT,
            allow_input_fusion=[False, False] + [True] * len(consts)),
        name="inproj",
    )(x2, pos2, *consts)


def _attn_kernel(q_ref, k_ref, vt_ref, o_ref, *, chunk_mask):
    i = pl.program_id(2)
    tq = q_ref.shape[0]
    tk = vt_ref.shape[2]

    def scores(j, masked, lo=0):
        start = pl.multiple_of(j * tk, tk)
        ss = [_dot_nt(k_ref[pl.ds(start, tk), hd * LANES:(hd + 1) * LANES],
                      q_ref[lo:, hd * LANES:(hd + 1) * LANES])
              for hd in range(ATT_HEADS)]
        if masked:
            keyg = lax.broadcasted_iota(I32, (tk, tq - lo), 0) + j * tk
            qlo = lax.broadcasted_iota(I32, (tk, tq - lo), 1) + (i * tq + lo)
            allowed = (keyg // CHUNK) <= (qlo // CHUNK) if chunk_mask else keyg <= qlo
            ss = [jnp.where(allowed, s, NEG) for s in ss]
        return ss

    ones = jnp.ones((2 * SUBLANES, tk), BF16)

    def update(j, ss, state, lo=0):
        vt = vt_ref[j]
        new = []
        for hd, s in enumerate(ss):
            m_all, acc_all = state[hd]
            m, acc = m_all[:, lo:], acc_all[:, lo:]
            m_new = jnp.maximum(m, jnp.max(s, axis=0, keepdims=True))
            alpha = jnp.exp2(m - m_new)
            p = jnp.exp2((s - m_new[0:1]).astype(BF16))
            va = jnp.concatenate([vt[hd * HEAD_DIM:(hd + 1) * HEAD_DIM, :], ones], axis=0)
            acc = alpha[0:1] * acc + _dot(va, p)
            if lo:
                m_new = jnp.concatenate([m_all[:, :lo], m_new], axis=1)
                acc = jnp.concatenate([acc_all[:, :lo], acc], axis=1)
            new.append((m_new, acc))
        return tuple(new)

    n_full = (i * tq) // tk
    group = max(1, tq // tk)

    def run_group(first, state, n_plain, n_masked=0):
        los = [0] * n_plain + [u * tk for u in range(n_masked)]
        msk = [False] * n_plain + [True] * n_masked
        nblk = n_plain + n_masked
        ss = scores(first, msk[0], los[0])
        for u in range(nblk):
            nxt = scores(first + u + 1, msk[u + 1], los[u + 1]) if u + 1 < nblk else None
            state = update(first + u, ss, state, los[u])
            ss = nxt
        return state

    long = ATT_GROUP * group
    n_long = n_full // long
    init1 = (jnp.full((SUBLANES, tq), NEG, F32), jnp.zeros((HEAD_DIM + 2 * SUBLANES, tq), F32))
    state = lax.fori_loop(0, n_long, lambda jj, c: run_group(jj * long, c, long),
                          (init1,) * ATT_HEADS)
    done = n_long * long
    left = (n_full - done) // group
    tails = [functools.partial(run_group, done, n_plain=r * group, n_masked=group)
             for r in range(ATT_GROUP)]

    def pick(lo, hi, st):
        if hi - lo == 1:
            return tails[lo](st)
        mid = (lo + hi) // 2
        return lax.cond(left < mid, lambda s: pick(lo, mid, s), lambda s: pick(mid, hi, s), st)

    state = pick(0, ATT_GROUP, state)
    out_t = jnp.concatenate([acc[:HEAD_DIM] / acc[HEAD_DIM:HEAD_DIM + 1] for _, acc in state],
                            axis=0)
    o_ref[...] = out_t.T.astype(o_ref.dtype)


def _attention(q, k, vt, *, batch, seq_len, chunk_mask):
    n = q.shape[0]
    t = ATT_TILE
    tk = vt.shape[2]
    nq = seq_len // t
    nkb = seq_len // tk
    hp = ATT_HEADS
    pairs = HEADS // hp
    return pl.pallas_call(
        functools.partial(_attn_kernel, chunk_mask=chunk_mask),
        out_shape=jax.ShapeDtypeStruct((n, HEADS * HEAD_DIM), BF16),
        grid=(batch, pairs, nq),
        in_specs=[pl.BlockSpec((t, hp * LANES), lambda b, p, i: (b * nq + i, p)),
                  pl.BlockSpec((seq_len, hp * LANES), lambda b, p, i: (b, p)),
                  pl.BlockSpec((nkb, hp * HEAD_DIM, tk), lambda b, p, i: (b, p, 0))],
        out_specs=pl.BlockSpec((t, hp * HEAD_DIM), lambda b, p, i: (b * nq + i, p)),
        compiler_params=pltpu.CompilerParams(
            dimension_semantics=("arbitrary", "arbitrary", "arbitrary"),
            vmem_limit_bytes=VMEM_LIMIT),
        name="attn_mla" if chunk_mask else "attn_fox",
    )(q, k, vt)


def _outproj_kernel(yf_ref, ym_ref, gate_ref, x_ref, bm_ref, wfo_ref, wmo_ref, wo_ref, gffn_ref,
                    wrh_ref, wrl_ref, br_ref, lstrict_ref, ustrict_ref,
                    x1_ref, route_ref, dest_ref, meta_ref, xs_hbm,
                    fill_ref, cur_ref, nfree_ref, tbl_ref,
                    hbuf, dbuf, dsm, zbuf, mbuf, msm, ssem, isem, zsem, msem):
    i = pl.program_id(0)
    nt = pl.num_programs(0)
    d = x_ref.shape[1]
    tm = x_ref.shape[0]
    sub = d // LANES
    rows = EXPERT_ROWS
    n_blocks = xs_hbm.shape[0] // (rows * sub)
    slot = i % 2

    def idx_copy(s):
        return pltpu.make_async_copy(dbuf.at[s], dsm.at[s], isem.at[s])

    def start_scatters(s):
        for static_s in range(2):
            @pl.when(s == static_s)
            def _(static_s=static_s):
                def body(t, _):
                    src = hbuf.at[static_s, pl.ds(pl.multiple_of(t * sub, sub), sub), :]
                    for k in range(TOP_K):
                        dst = dsm[static_s, k, t]
                        pltpu.make_async_copy(
                            src, xs_hbm.at[pl.ds(pl.multiple_of(dst, sub), sub), :],
                            ssem.at[static_s]).start(priority=k % 2)
                    return 0

                lax.fori_loop(0, tm, body, 0, unroll=4)

    def wait_scatters(s):
        for _ in range(TOP_K):
            pltpu.make_async_copy(hbuf.at[s], xs_hbm.at[pl.ds(0, tm * sub), :], ssem.at[s]).wait()

    @pl.when(i == 0)
    def _():
        fill_ref[...] = jnp.full_like(fill_ref, float(rows))
        cur_ref[...] = jnp.zeros_like(cur_ref)
        nfree_ref[...] = jnp.zeros_like(nfree_ref)
        tbl_ref[...] = jnp.full_like(tbl_ref, float(N_EXPERTS))
        zbuf[...] = jnp.zeros_like(zbuf)

    def tile_step(scatter_prev):
        if scatter_prev:
            inline_scatters(1 - slot)
        a = _dot(yf_ref[...], wfo_ref[...])
        b = _dot(ym_ref[...], wmo_ref[...])
        g = 1.0 / (1.0 + jnp.exp(-(gate_ref[...].astype(F32) + bm_ref[...])))
        merged = (g[:, :d] * a + g[:, d:] * b).astype(BF16)
        x1 = x_ref[...] + _dot(merged, wo_ref[...])
        x1_ref[...] = x1
        h2 = _rms(x1, gffn_ref[...])

        hi = h2.astype(BF16)
        lo = (h2 - hi.astype(F32)).astype(BF16)
        wrh = wrh_ref[...]
        logits = _dot(hi, wrh) + _dot(lo, wrh) + _dot(hi, wrl_ref[...]) + br_ref[...]

        lane = lax.broadcasted_iota(I32, (tm, LANES), 1)
        vals = logits
        sels, tops = [], []
        for _ in range(TOP_K):
            mx = jnp.max(vals, axis=-1, keepdims=True)
            idx = jnp.min(jnp.where(vals == mx, lane, LANES), axis=-1, keepdims=True)
            sel = lane == idx
            vals = jnp.where(sel, NEG, vals)
            sels.append(sel)
            tops.append(mx)
        es = [jnp.exp(tv - tops[0]) for tv in tops]
        den = es[0] + es[1] + es[2] + es[3]

        onehot = jnp.zeros((tm, LANES), F32)
        for sel in sels:
            onehot = onehot + sel.astype(F32)
        before = _dot(lstrict_ref[...], onehot.astype(BF16))
        cnt = jnp.sum(onehot, axis=0, keepdims=True)

        fill = fill_ref[...]
        cur = cur_ref[...]
        nfree = nfree_ref[...]
        need = ((fill + cnt) > float(rows)).astype(F32)
        need8 = jnp.broadcast_to(need, (SUBLANES, LANES)).astype(BF16)
        newid = nfree + _dot(need8, ustrict_ref[...])[0:1, :]
        pos = fill + before
        dest = jnp.where(pos < float(rows), cur * rows + pos, newid * rows + pos - float(rows))
        fill_ref[...] = fill + cnt - need * float(rows)
        cur_ref[...] = jnp.where(need > 0, newid, cur)
        nfree_ref[...] = nfree + jnp.sum(need, axis=-1, keepdims=True)
        blk_id = (lax.broadcasted_iota(I32, (SUBLANES, LANES), 0) * LANES
                  + lax.broadcasted_iota(I32, (SUBLANES, LANES), 1)).astype(F32)
        tbl = tbl_ref[...]
        for e in range(N_EXPERTS):
            hit = (blk_id == newid[:, e:e + 1]) & (need[:, e:e + 1] > 0)
            tbl = jnp.where(hit, float(e), tbl)
        tbl_ref[...] = tbl

        route = jnp.zeros((tm, LANES), F32)
        for k in range(TOP_K):
            dest_k = jnp.sum(jnp.where(sels[k], dest, 0.0), axis=-1, keepdims=True)
            route = jnp.where(lane == ROUTE_DEST + k, dest_k, route)
            route = jnp.where(lane == ROUTE_W + k, es[k] / den, route)
        route_ref[...] = route
        dest_t = (route.T[0:SUBLANES, :] * float(sub)).astype(I32)
        dest_ref[0] = dest_t

        for s in range(sub):
            hbuf[slot, pl.ds(s, tm, stride=sub), :] = h2[:, s * LANES:(s + 1) * LANES]
        dbuf[slot] = dest_t

    def flush_last():
        idx_copy(slot).wait()
        start_scatters(slot)

        @pl.when(i >= 1)
        def _():
            wait_scatters(1 - slot)

        wait_scatters(slot)

        meta = jnp.concatenate([tbl_ref[...], jnp.broadcast_to(nfree_ref[...], (SUBLANES, LANES))],
                               axis=0).astype(I32)
        meta_ref[...] = meta
        state = jnp.concatenate([fill_ref[...], cur_ref[...], nfree_ref[...],
                                 jnp.zeros((SUBLANES - 3, LANES), F32)], axis=0).astype(I32)
        mbuf[...] = state
        mcopy = pltpu.make_async_copy(mbuf, msm, msem.at[0])
        mcopy.start()
        mcopy.wait()

        def zero_copy(first_row, n_rows):
            return pltpu.make_async_copy(
                zbuf.at[pl.ds(0, n_rows * sub), :],
                xs_hbm.at[pl.ds(pl.multiple_of(first_row * sub, sub), n_rows * sub), :], zsem.at[0])

        chunks = [rows >> (s + 1) for s in range(rows.bit_length() - 1)]
        plans = []
        for e in range(N_EXPERTS):
            rem = rows - msm[0, e]
            at = msm[1, e] * rows + msm[0, e]
            for c in chunks:
                take = (rem & c) != 0
                plans.append((take, zero_copy(at, c)))
                at = at + jnp.where(take, c, 0)
        for j in range(N_EXPERTS):
            blk = msm[2, 0] + j
            safe = jnp.minimum(blk, n_blocks - 1)
            plans.append((blk < n_blocks, zero_copy(safe * rows, rows)))
        for take, cp in plans:
            pl.when(take)(cp.start)
        for take, cp in plans:
            pl.when(take)(cp.wait)

    @pl.when(i > 0)
    def _():
        idx_copy(1 - slot).wait()

    @pl.when(i >= 2)
    def _():
        wait_scatters(slot)

    def inline_scatters(s):
        for t in range(tm):
            src = hbuf.at[s, pl.ds(t * sub, sub), :]
            for k in range(TOP_K):
                dst = dsm[s, k, t]
                pltpu.make_async_copy(src, xs_hbm.at[pl.ds(pl.multiple_of(dst, sub), sub), :],
                                      ssem.at[s]).start(priority=k % 2)

    @pl.when(i == 0)
    def _():
        tile_step(False)

    @pl.when(i > 0)
    def _():
        tile_step(True)

    idx_copy(slot).start()

    @pl.when(i == nt - 1)
    def _():
        flush_last()


def _outproj(yf, ym, gates, x2, bm, wfo, wmo, wo, gffn, wrh, wrl, br, lstrict, ustrict, *, n_blocks):
    n, d = x2.shape
    tm = MOE_TILE
    rows = EXPERT_ROWS
    sub = d // LANES
    const = lambda i: (0, 0)
    row = lambda i: (i, 0)
    full = lambda a: pl.BlockSpec(a.shape, const)
    consts = (bm, wfo, wmo, wo, gffn, wrh, wrl, br, lstrict, ustrict)
    return pl.pallas_call(
        _outproj_kernel,
        out_shape=[jax.ShapeDtypeStruct((n, d), F32),
                   jax.ShapeDtypeStruct((n, LANES), F32),
                   jax.ShapeDtypeStruct((n // tm, SUBLANES, tm), I32),
                   jax.ShapeDtypeStruct((2 * SUBLANES, LANES), I32),
                   jax.ShapeDtypeStruct((n_blocks * rows * sub, LANES), F32)],
        grid=(n // tm,),
        in_specs=[pl.BlockSpec((tm, yf.shape[1]), row), pl.BlockSpec((tm, ym.shape[1]), row),
                  pl.BlockSpec((tm, gates.shape[1]), row), pl.BlockSpec((tm, d), row)]
                 + [full(a) for a in consts],
        out_specs=[pl.BlockSpec((tm, d), row), pl.BlockSpec((tm, LANES), row),
                   pl.BlockSpec((1, SUBLANES, tm), lambda i: (i, 0, 0)),
                   pl.BlockSpec((2 * SUBLANES, LANES), const),
                   pl.BlockSpec(memory_space=pl.ANY)],
        scratch_shapes=[pltpu.VMEM((1, LANES), F32), pltpu.VMEM((1, LANES), F32),
                        pltpu.VMEM((1, LANES), F32), pltpu.VMEM((SUBLANES, LANES), F32),
                        pltpu.VMEM((2, tm * sub, LANES), F32),
                        pltpu.VMEM((2, SUBLANES, tm), I32),
                        pltpu.SMEM((2, SUBLANES, tm), I32),
                        pltpu.VMEM((rows * sub, LANES), F32),
                        pltpu.VMEM((SUBLANES, LANES), I32),
                        pltpu.SMEM((SUBLANES, LANES), I32),
                        pltpu.SemaphoreType.DMA((2,)), pltpu.SemaphoreType.DMA((2,)),
                        pltpu.SemaphoreType.DMA((1,)), pltpu.SemaphoreType.DMA((1,))],
        compiler_params=pltpu.CompilerParams(dimension_semantics=("arbitrary",),
                                             vmem_limit_bytes=VMEM_LIMIT),
        name="outproj_router",
    )(yf, ym, gates, x2, *consts)


def _expert_kernel(order_ref, be_ref, nused_ref,
                   xs_ref, wgu_ref, bgu_ref, wd_ref, bd_ref, ys_ref, wgu_b, wd_b):
    b = pl.program_id(0)
    rows = EXPERT_ROWS
    sub = xs_ref.shape[0] // rows

    @pl.when(b < nused_ref[0])
    def _():
        changed = jnp.logical_or(b == 0, be_ref[b] != be_ref[jnp.maximum(b - 1, 0)])

        @pl.when(changed)
        def _():
            wgu_b[...] = wgu_ref[0].astype(BF16)
            wd_b[...] = wd_ref[0].astype(BF16)

        x = jnp.concatenate(
            [xs_ref[pl.ds(s, rows, stride=sub), :].astype(BF16) for s in range(sub)], axis=1)
        gu = _dot(x, wgu_b[...]) + bgu_ref[0]
        gate = jnp.minimum(gu[:, :EXPERT_FF], SWIGLU_LIMIT)
        up = jnp.clip(gu[:, EXPERT_FF:], -SWIGLU_LIMIT, SWIGLU_LIMIT)
        glu = gate * (1.0 / (1.0 + jnp.exp(-SWIGLU_ALPHA * gate)))
        y = _dot(((up + 1.0) * glu).astype(BF16), wd_b[...]) + bd_ref[0]
        for s in range(sub):
            ys_ref[pl.ds(s, rows, stride=sub), :] = y[:, s * LANES:(s + 1) * LANES]

    @pl.when(b >= nused_ref[0])
    def _():
        ys_ref[...] = jnp.zeros_like(ys_ref)


def _experts(order, block_e, nused, xs, w_gu, b_gu, w_d, b_d):
    e, d, ff2 = w_gu.shape
    rows = EXPERT_ROWS
    sub = d // LANES
    n_blocks = xs.shape[0] // (rows * sub)
    wmap = lambda b, od, be, nu: (be[b], 0, 0)
    xmap = lambda b, od, be, nu: (od[b], 0)
    grid_spec = pltpu.PrefetchScalarGridSpec(
        num_scalar_prefetch=3,
        grid=(n_blocks,),
        in_specs=[pl.BlockSpec((rows * sub, LANES), xmap),
                  pl.BlockSpec((1, d, ff2), wmap), pl.BlockSpec((1, 1, ff2), wmap),
                  pl.BlockSpec((1, ff2 // 2, d), wmap), pl.BlockSpec((1, 1, d), wmap)],
        out_specs=pl.BlockSpec((rows * sub, LANES), xmap),
        scratch_shapes=[pltpu.VMEM((d, ff2), BF16), pltpu.VMEM((ff2 // 2, d), BF16)],
    )
    return pl.pallas_call(
        _expert_kernel,
        out_shape=jax.ShapeDtypeStruct(xs.shape, F32),
        grid_spec=grid_spec,
        compiler_params=pltpu.CompilerParams(dimension_semantics=("arbitrary",),
                                             vmem_limit_bytes=VMEM_LIMIT),
        name="experts",
    )(order, block_e, nused, xs, w_gu, b_gu.reshape(e, 1, ff2), w_d, b_d.reshape(e, 1, d))


def _combine_kernel(x1_ref, route_ref, dest_hbm, ys_hbm, g_ref, o_ref, ybuf, dsm, gsem, isem):
    i = pl.program_id(0)
    nt = pl.num_programs(0)
    tm, d = x1_ref.shape
    sub = d // LANES
    slot = i % 2

    def idx_copy(tile):
        s = tile % 2
        return pltpu.make_async_copy(dest_hbm.at[tile], dsm.at[s], isem.at[s])

    def start_gathers(tile):
        for static_s in range(2):
            @pl.when(tile % 2 == static_s)
            def _(static_s=static_s):
                def body(t, _):
                    for k in range(TOP_K):
                        src = dsm[static_s, k, t]
                        pltpu.make_async_copy(
                            ys_hbm.at[pl.ds(pl.multiple_of(src, sub), sub), :],
                            ybuf.at[static_s, pl.ds(pl.multiple_of((k * tm + t) * sub, sub), sub), :],
                            gsem.at[static_s]).start(priority=k % 2)
                    return 0

                lax.fori_loop(0, tm, body, 0, unroll=4)

    def wait_gathers(s):
        pltpu.make_async_copy(ys_hbm.at[pl.ds(0, TOP_K * tm * sub), :], ybuf.at[s], gsem.at[s]).wait()

    @pl.when(i == 0)
    def _():
        idx_copy(0).start()
        idx_copy(0).wait()
        start_gathers(0)

        @pl.when(nt > 1)
        def _():
            idx_copy(1).start()

    @pl.when(i + 1 < nt)
    def _():
        idx_copy(i + 1).wait()

    wait_gathers(slot)

    def inline_gathers(s):
        for t in range(tm):
            for k in range(TOP_K):
                src = dsm[s, k, t]
                pltpu.make_async_copy(
                    ys_hbm.at[pl.ds(pl.multiple_of(src, sub), sub), :],
                    ybuf.at[s, pl.ds((k * tm + t) * sub, sub), :],
                    gsem.at[s]).start(priority=k % 2)

    def tile_step(prefetch_next):
        if prefetch_next:
            inline_gathers(1 - slot)
        route = route_ref[...]
        ws = [route[:, ROUTE_W + k:ROUTE_W + k + 1] for k in range(TOP_K)]
        chunks = []
        ssq = jnp.zeros((tm, 1), F32)
        for s in range(sub):
            acc = x1_ref[:, s * LANES:(s + 1) * LANES]
            for k in range(TOP_K):
                acc = acc + ws[k] * ybuf[slot, pl.ds(k * tm * sub + s, tm, stride=sub), :]
            chunks.append(acc)
            ssq = ssq + jnp.sum(acc * acc, axis=-1, keepdims=True)
        inv = lax.rsqrt(ssq / d + NORM_EPS)
        for s in range(sub):
            sl = slice(s * LANES, (s + 1) * LANES)
            o_ref[:, sl] = chunks[s] * inv * g_ref[:, sl]

    @pl.when(i + 1 < nt)
    def _():
        tile_step(True)

    @pl.when(i + 1 >= nt)
    def _():
        tile_step(False)

    @pl.when(i + 2 < nt)
    def _():
        idx_copy(i + 2).start()


def _combine(x1, route, dest_t, ys, g_final):
    n, d = x1.shape
    tm = MOE_TILE
    sub = d // LANES
    row = lambda i: (i, 0)
    return pl.pallas_call(
        _combine_kernel,
        out_shape=jax.ShapeDtypeStruct((n, d), F32),
        grid=(n // tm,),
        in_specs=[pl.BlockSpec((tm, d), row), pl.BlockSpec((tm, LANES), row),
                  pl.BlockSpec(memory_space=pl.ANY), pl.BlockSpec(memory_space=pl.ANY),
                  pl.BlockSpec((1, d), lambda i: (0, 0))],
        out_specs=pl.BlockSpec((tm, d), row),
        scratch_shapes=[pltpu.VMEM((2, TOP_K * tm * sub, LANES), F32),
                        pltpu.SMEM((2, SUBLANES, tm), I32),
                        pltpu.SemaphoreType.DMA((2,)), pltpu.SemaphoreType.DMA((2,))],
        compiler_params=pltpu.CompilerParams(dimension_semantics=("arbitrary",),
                                             vmem_limit_bytes=VMEM_LIMIT),
        name="combine_norm",
    )(x1, route, dest_t, ys, g_final)


def _pad_heads(w, per_head, width=LANES):
    k = w.shape[0]
    w = w.reshape(k, HEADS, per_head)
    w = jnp.pad(w, ((0, 0), (0, 0), (0, width - per_head)))
    return w.reshape(k, HEADS * width)


def _aug_constants():
    place = np.zeros((LANES, 2 * LANES), np.float32)
    ones = np.zeros((1, 2 * LANES), np.float32)
    for hd in range(HEADS):
        for piece in range(3):
            src = FG_LO + piece * HEADS + hd
            place[src, 16 * hd + piece] = 1.0
            place[src, LANES + 16 * hd + 3 + piece] = -1.0
            ones[0, 16 * hd + 3 + piece] = 1.0
            ones[0, LANES + 16 * hd + piece] = 1.0
    return jnp.asarray(place, BF16), jnp.asarray(ones, F32)


def _layer(x2, pos2, batch, seq_len, g_attn_norm, w_in, b_fgate, g_q_a, w_q_b, g_kv_a, w_kv_b,
           w_fox_out, w_mla_out, b_merge, w_o, g_ffn_norm, w_router, b_router, w_gu, b_gu,
           w_down, b_down, g_out):
    n, d = x2.shape
    fw = HEADS * HEAD_DIM
    o = 0
    w_qf = w_in[:, o:o + fw]; o += fw
    w_kf = w_in[:, o:o + fw]; o += fw
    w_vf = w_in[:, o:o + fw]; o += fw
    w_f = w_in[:, o:o + HEADS]; o += HEADS
    w_ql = w_in[:, o:o + Q_RANK]; o += Q_RANK
    w_ckv = w_in[:, o:o + KV_RANK]; o += KV_RANK
    w_kpe = w_in[:, o:o + ROPE_DIM]; o += ROPE_DIM
    w_gate = w_in[:, o:]

    wq = (w_qf * (HEAD_DIM ** -0.5)).astype(BF16)
    wk = w_kf.astype(BF16)
    wmisc = jnp.concatenate([jnp.zeros((d, PE_LO), F32), w_kpe, w_f,
                             jnp.zeros((d, LANES - FG_LO - HEADS), F32)], axis=1)
    wlat = jnp.concatenate([w_ql, w_ckv, wmisc], axis=1).astype(BF16)
    bf128 = jnp.zeros((1, LANES), F32).at[0, FG_LO:FG_LO + HEADS].set(b_fgate)
    wqb = _pad_heads(w_q_b, HEAD_DIM + ROPE_DIM).astype(BF16)
    wkv = w_kv_b.reshape(KV_RANK, HEADS, 2 * HEAD_DIM)
    wkbk = wkv[:, :, :HEAD_DIM].reshape(KV_RANK, fw).astype(BF16)
    wkbv = wkv[:, :, HEAD_DIM:].reshape(KV_RANK, fw).T.astype(BF16)
    ltri = jnp.asarray(np.tril(np.ones((IN_TILE, IN_TILE), np.float32)), BF16)
    lstrict = jnp.asarray(np.tril(np.ones((MOE_TILE, MOE_TILE), np.float32), -1), BF16)
    ustrict = jnp.asarray(np.triu(np.ones((LANES, LANES), np.float32), 1), BF16)
    place, ones = _aug_constants()
    half = ROPE_DIM // 2
    inv_freq = ROPE_THETA ** (-jnp.arange(half, dtype=F32) / half)
    freq = jnp.zeros((1, LANES), F32).at[0, PE_LO:PE_MID].set(inv_freq).at[0, PE_MID:PE_HI].set(inv_freq)

    qf, kf, vf, qm, km, vm, gates = _inproj(
        x2, pos2, g_attn_norm.reshape(1, d), wq, wk, w_vf.T.astype(BF16), wlat, w_gate.astype(BF16),
        bf128, g_q_a.reshape(1, -1), wqb, g_kv_a.reshape(1, -1), wkbk, wkbv, ltri, place, ones,
        freq, seq_len=seq_len)

    y_fox = _attention(qf, kf, vf, batch=batch, seq_len=seq_len, chunk_mask=False)
    y_mla = _attention(qm, km, vm, batch=batch, seq_len=seq_len, chunk_mask=True)

    wr = jnp.pad(w_router, ((0, 0), (0, LANES - N_EXPERTS)))
    wrh = wr.astype(BF16)
    wrl = (wr - wrh.astype(F32)).astype(BF16)
    br = jnp.full((1, LANES), NEG, F32).at[0, :N_EXPERTS].set(b_router)
    n_blocks = n * TOP_K // EXPERT_ROWS + N_EXPERTS
    x1, route, dest_t, meta, xs = _outproj(
        y_fox, y_mla, gates, x2, b_merge.reshape(1, -1), w_fox_out.astype(BF16),
        w_mla_out.astype(BF16), w_o.astype(BF16), g_ffn_norm.reshape(1, d), wrh, wrl, br, lstrict,
        ustrict, n_blocks=n_blocks)

    block_e = meta[:SUBLANES].reshape(-1)[:n_blocks]
    order = jnp.argsort(block_e, stable=True).astype(I32)
    nused = meta[SUBLANES, 0:1]
    be_sorted = jnp.minimum(block_e[order], N_EXPERTS - 1).astype(I32)
    be_sorted = jnp.where(jnp.arange(n_blocks) < nused[0], be_sorted,
                          be_sorted[jnp.maximum(nused[0] - 1, 0)])
    ys = _experts(order, be_sorted, nused, xs, w_gu, b_gu, w_down, b_down)
    return _combine(x1, route, dest_t, ys, g_out.reshape(1, d))


def kernel(x, positions, g_attn_norm, w_in, b_fgate, g_q_a, w_q_b, g_kv_a, w_kv_b, w_fox_out, w_mla_out, b_merge, w_o, g_ffn_norm, w_router, b_router, w_gu, b_gu, w_down, b_down, g_final):
    batch, seq_len, d = x.shape
    depth = w_in.shape[0]
    assert depth == 1, "the fused combine + final-norm kernel assumes a single layer"
    assert seq_len % ATT_TILE == 0 and d % LANES == 0
    assert (batch * seq_len * TOP_K) % EXPERT_ROWS == 0
    assert batch * seq_len * TOP_K // EXPERT_ROWS + N_EXPERTS <= SUBLANES * LANES
    x2 = x.reshape(batch * seq_len, d)
    pos2 = positions.reshape(batch * seq_len, 1).astype(I32)
    out = _layer(x2, pos2, batch, seq_len, g_attn_norm[0], w_in[0], b_fgate[0], g_q_a[0], w_q_b[0],
                 g_kv_a[0], w_kv_b[0], w_fox_out[0], w_mla_out[0], b_merge[0], w_o[0],
                 g_ffn_norm[0], w_router[0], b_router[0], w_gu[0], b_gu[0], w_down[0], b_down[0],
                 g_final)
    return out.reshape(batch, seq_len, d)
```
